```python
import jax, jax.numpy as jnp
from jax import lax
import numpy as np

D_MODEL = 1024
BATCH = 32
SEQ = 2048
DEPTH = 2
DEC_BATCH = 16
DEC_SEQ = 32
PAST_LEN = 2048

CHUNK = 64
Q_BLOCK = 128
EPS = 1e-6

GLA_HEADS = 4
GLA_DK = 64
GLA_DV = 128
GLA_LOWRANK = 16
GLA_GATE_NORM = 16.0

FOX_HEADS = 8
FOX_HD = 64
FOX_SCALE = FOX_HD ** -0.5

MIX_WIDTH = GLA_HEADS * GLA_DV + FOX_HEADS * FOX_HD

IN_SIZES = (GLA_HEADS * GLA_DK, GLA_HEADS * GLA_DK, GLA_HEADS * GLA_DV, GLA_HEADS * GLA_DV,
            GLA_LOWRANK, FOX_HEADS * FOX_HD, FOX_HEADS * FOX_HD, FOX_HEADS * FOX_HD, FOX_HEADS)
IN_WIDTH = sum(IN_SIZES)

PEER_HEADS = 8
PEER_NKEYS = 128
PEER_NEXP = PEER_NKEYS * PEER_NKEYS
PEER_DKEY = 256
PEER_TOPK = 16
PEER_BLOCK = 512

kernel_name = 'hybrid_gla_fox_peer_stream_step'


def _split_points():
    pts, acc = [], 0
    for s in IN_SIZES[:-1]:
        acc += s
        pts.append(acc)
    return pts


def rmsnorm(x, g):
    xf = x.astype(jnp.float32)
    y = xf * lax.rsqrt(jnp.mean(xf * xf, axis=-1, keepdims=True) + EPS)
    return (y * g.astype(jnp.float32)).astype(x.dtype)


def gla_recurrence(q, k, v, log_a, S0):
    B, T, H, Dk = q.shape
    Dv = v.shape[-1]
    L = min(T, CHUNK)
    n = T // L

    def to_chunks(a):
        return a.reshape(B, n, L, H, a.shape[-1]).transpose(1, 0, 3, 2, 4)

    qc = to_chunks(q.astype(jnp.float32) * (GLA_DK ** -0.5))
    kc = to_chunks(k.astype(jnp.float32))
    vc = to_chunks(v.astype(jnp.float32))
    ac = to_chunks(log_a)
    causal = jnp.tril(jnp.ones((L, L), dtype=bool))

    def step(S, inp):
        qb, kb, vb, ab = inp
        b = jnp.cumsum(ab, axis=2)
        b_last = b[:, :, -1, :]
        o_inter = jnp.einsum('bhld,bhdv->bhlv', qb * jnp.exp(b), S)
        diff = b[:, :, :, None, :] - b[:, :, None, :, :]
        decay = jnp.exp(jnp.where(causal[:, :, None], diff, -jnp.inf))
        A = jnp.einsum('bhtd,bhsd,bhtsd->bhts', qb, kb, decay)
        o = o_inter + jnp.einsum('bhts,bhsv->bhtv', A, vb)
        S_new = jnp.exp(b_last)[..., None] * S + jnp.einsum(
            'bhsd,bhsv->bhdv', kb * jnp.exp(b_last[:, :, None, :] - b), vb)
        return S_new, o

    S_fin, o = lax.scan(step, S0.astype(jnp.float32), (qc, kc, vc, ac))
    o = o.transpose(1, 0, 3, 2, 4).reshape(B, T, H, Dv)
    return o, S_fin


def fox_attention(q, k_all, v_all, c_all, P):
    B, T, H, D = q.shape
    S = k_all.shape[1]
    qb = min(T, Q_BLOCK)
    nb = T // qb
    cT = c_all.transpose(0, 2, 1)
    c_q = cT[:, :, P:]
    q_blocks = q.reshape(B, nb, qb, H, D).transpose(1, 0, 2, 3, 4)
    cq_blocks = c_q.reshape(B, H, nb, qb).transpose(2, 0, 1, 3)
    pos_blocks = (P + jnp.arange(T)).reshape(nb, qb)
    pos_k = jnp.arange(S)

    def block(inp):
        qi, ci, pi = inp
        s = jnp.einsum('bqhd,bkhd->bhqk', qi, k_all, preferred_element_type=jnp.float32) * FOX_SCALE
        s = s + ci[..., None] - cT[:, :, None, :]
        s = jnp.where(pos_k[None, :] <= pi[:, None], s, -jnp.inf)
        p = jax.nn.softmax(s, axis=-1)
        return jnp.einsum('bhqk,bkhd->bqhd', p.astype(v_all.dtype), v_all)

    out = lax.map(block, (q_blocks, cq_blocks, pos_blocks))
    return out.transpose(1, 0, 2, 3, 4).reshape(B, T, H, D)


def peer(h, w_q, subkeys, u_tab, v_tab):
    B, T, D = h.shape
    n = B * T
    blk = min(PEER_BLOCK, n)
    nb = -(-n // blk)
    pad = nb * blk - n
    xf = jnp.pad(h.reshape(n, D), ((0, pad), (0, 0))).reshape(nb, blk, D)

    def block(xb):
        q = (xb @ w_q).reshape(blk, PEER_HEADS, 2, PEER_DKEY // 2)
        s = jnp.einsum('nhcd,hckd->nhck', q, subkeys, preferred_element_type=jnp.float32)
        s1, i1 = lax.top_k(s[:, :, 0], PEER_TOPK)
        s2, i2 = lax.top_k(s[:, :, 1], PEER_TOPK)
        cand = (s1[..., :, None] + s2[..., None, :]).reshape(blk, PEER_HEADS, PEER_TOPK * PEER_TOPK)
        top, ti = lax.top_k(cand, PEER_TOPK)
        e1 = jnp.take_along_axis(i1, ti // PEER_TOPK, axis=-1)
        e2 = jnp.take_along_axis(i2, ti % PEER_TOPK, axis=-1)
        idx = e1 * PEER_NKEYS + e2
        g = jax.nn.softmax(top, axis=-1)
        u = jnp.take(u_tab, idx, axis=0)
        a = jax.nn.gelu(jnp.einsum('nhkd,nd->nhk', u, xb, preferred_element_type=jnp.float32),
                        approximate=False)
        coeff = (g * a).astype(v_tab.dtype)
        vsel = jnp.take(v_tab, idx, axis=0)
        return jnp.einsum('nhk,nhkd->nd', coeff, vsel)

    out = lax.map(block, xf).reshape(nb * blk, D)[:n]
    return out.reshape(B, T, D).astype(h.dtype)


def trunk_layer(x, S0, k_past, v_past, logf_past, g_norm1, w_in, w_gla_a2, b_gla_a, g_gla_onorm,
                g_fox_qnorm, g_fox_knorm, b_fox_f, w_out, g_norm2, w_peer_q, peer_subkeys,
                peer_u, peer_v):
    B, T, _ = x.shape
    P = k_past.shape[1]
    h = rmsnorm(x, g_norm1)
    z = h @ w_in
    gq, gk, gv, gg, glr, fq, fk, fv, ff = jnp.split(z, _split_points(), axis=-1)
    gq = gq.reshape(B, T, GLA_HEADS, GLA_DK)
    gk = gk.reshape(B, T, GLA_HEADS, GLA_DK)
    gv = gv.reshape(B, T, GLA_HEADS, GLA_DV)
    log_a = jax.nn.log_sigmoid((glr @ w_gla_a2 + b_gla_a).astype(jnp.float32)) / GLA_GATE_NORM
    log_a = log_a.reshape(B, T, GLA_HEADS, GLA_DK)
    o_gla, S_new = gla_recurrence(gq, gk, gv, log_a, S0)
    o_gla = rmsnorm(o_gla.astype(x.dtype), g_gla_onorm) * jax.nn.silu(gg.reshape(B, T, GLA_HEADS, GLA_DV))
    fq = rmsnorm(fq.reshape(B, T, FOX_HEADS, FOX_HD), g_fox_qnorm)
    fk = rmsnorm(fk.reshape(B, T, FOX_HEADS, FOX_HD), g_fox_knorm)
    fv = fv.reshape(B, T, FOX_HEADS, FOX_HD)
    logf = jax.nn.log_sigmoid((ff + b_fox_f).astype(jnp.float32))
    k_all = jnp.concatenate([k_past.astype(fk.dtype), fk], axis=1)
    v_all = jnp.concatenate([v_past.astype(fv.dtype), fv], axis=1)
    c_all = jnp.cumsum(jnp.concatenate([logf_past.astype(jnp.float32), logf], axis=1), axis=1)
    o_fox = fox_attention(fq, k_all, v_all, c_all, P)
    mixed = jnp.concatenate([o_gla.reshape(B, T, GLA_HEADS * GLA_DV),
                             o_fox.reshape(B, T, FOX_HEADS * FOX_HD).astype(x.dtype)], axis=-1) @ w_out
    x = x + mixed
    x = x + peer(rmsnorm(x, g_norm2), w_peer_q, peer_subkeys, peer_u, peer_v)
    return x, S_new, fk, fv, logf


def setup_inputs(seed: int = 0) -> dict:
    key = jax.random.key(seed)
    ks = jax.random.split(key, 20)
    f32 = jnp.float32

    def nrm(k, shape, scale):
        return scale * jax.random.normal(k, shape, f32)

    return dict(
        x_prompt=nrm(ks[0], (BATCH, SEQ, D_MODEL), 1.0),
        x_sample=nrm(ks[1], (DEC_BATCH, DEC_SEQ, D_MODEL), 1.0),
        cache_fox_k=nrm(ks[2], (DEPTH, DEC_BATCH, PAST_LEN, FOX_HEADS, FOX_HD), 1.0),
        cache_fox_v=nrm(ks[3], (DEPTH, DEC_BATCH, PAST_LEN, FOX_HEADS, FOX_HD), 1.0),
        cache_fox_logf=jax.nn.log_sigmoid(1.0 + jax.random.normal(ks[4], (DEPTH, DEC_BATCH, PAST_LEN, FOX_HEADS), f32)),
        state_gla=nrm(ks[5], (DEPTH, DEC_BATCH, GLA_HEADS, GLA_DK, GLA_DV), 0.5),
        g_norm1=1.0 + nrm(ks[6], (DEPTH, D_MODEL), 0.02),
        w_in=nrm(ks[7], (DEPTH, D_MODEL, IN_WIDTH), D_MODEL ** -0.5),
        w_gla_a2=nrm(ks[8], (DEPTH, GLA_LOWRANK, GLA_HEADS * GLA_DK), GLA_LOWRANK ** -0.5),
        b_gla_a=nrm(ks[9], (DEPTH, GLA_HEADS * GLA_DK), 0.1),
        g_gla_onorm=1.0 + nrm(ks[10], (DEPTH, GLA_DV), 0.02),
        g_fox_qnorm=1.0 + nrm(ks[11], (DEPTH, FOX_HD), 0.02),
        g_fox_knorm=1.0 + nrm(ks[12], (DEPTH, FOX_HD), 0.02),
        b_fox_f=1.0 + nrm(ks[13], (DEPTH, FOX_HEADS), 0.1),
        w_out=nrm(ks[14], (DEPTH, MIX_WIDTH, D_MODEL), MIX_WIDTH ** -0.5),
        g_norm2=1.0 + nrm(ks[15], (DEPTH, D_MODEL), 0.02),
        w_peer_q=nrm(ks[16], (DEPTH, D_MODEL, PEER_HEADS * PEER_DKEY), D_MODEL ** -0.5),
        peer_subkeys=nrm(ks[17], (DEPTH, PEER_HEADS, 2, PEER_NKEYS, PEER_DKEY // 2), (PEER_DKEY // 2) ** -0.5),
        peer_u=nrm(ks[18], (DEPTH, PEER_NEXP, D_MODEL), D_MODEL ** -0.5),
        peer_v=nrm(ks[19], (DEPTH, PEER_NEXP, D_MODEL), PEER_HEADS ** -0.5),
    )


def reference(x_prompt, x_sample, cache_fox_k, cache_fox_v, cache_fox_logf, state_gla,
              g_norm1, w_in, w_gla_a2, b_gla_a, g_gla_onorm, g_fox_qnorm, g_fox_knorm, b_fox_f,
              w_out, g_norm2, w_peer_q, peer_subkeys, peer_u, peer_v):
    yp, ys = x_prompt, x_sample
    Bp = x_prompt.shape[0]
    pk, pv, pf, pS = [], [], [], []
    sk, sv, sf, sS = [], [], [], []
    for l in range(DEPTH):
        params = (g_norm1[l], w_in[l], w_gla_a2[l], b_gla_a[l], g_gla_onorm[l], g_fox_qnorm[l],
                  g_fox_knorm[l], b_fox_f[l], w_out[l], g_norm2[l], w_peer_q[l], peer_subkeys[l],
                  peer_u[l], peer_v[l])
        S0 = jnp.zeros((Bp, GLA_HEADS, GLA_DK, GLA_DV), jnp.float32)
        k0 = jnp.zeros((Bp, 0, FOX_HEADS, FOX_HD), yp.dtype)
        f0 = jnp.zeros((Bp, 0, FOX_HEADS), jnp.float32)
        yp, S_p, k_p, v_p, f_p = trunk_layer(yp, S0, k0, k0, f0, *params)
        ys, S_s, k_s, v_s, f_s = trunk_layer(ys, state_gla[l], cache_fox_k[l], cache_fox_v[l],
                                             cache_fox_logf[l], *params)
        pk.append(k_p); pv.append(v_p); pf.append(f_p); pS.append(S_p)
        sk.append(k_s); sv.append(v_s); sf.append(f_s); sS.append(S_s)
    return (yp, ys, jnp.stack(pk), jnp.stack(pv), jnp.stack(pf), jnp.stack(pS),
            jnp.stack(sk), jnp.stack(sv), jnp.stack(sf), jnp.stack(sS))
```

```python
import functools

import jax
import jax.numpy as jnp
from jax import lax
from jax.experimental import pallas as pl
from jax.experimental.pallas import tpu as pltpu

F32 = jnp.float32
BF16 = jnp.bfloat16
I32 = jnp.int32

D_MODEL = 1024
EPS = 1e-6
GLA_CHUNK = 64
GLA_HEADS = 4
GLA_DK = 64
GLA_DV = 128
GLA_LOWRANK = 16
GLA_GATE_NORM = 16.0
GLA_SUB = 16
FOX_HEADS = 8
FOX_HD = 64
FOX_SCALE = FOX_HD ** -0.5
FOX_BLOCK = 256
IN_SIZES = (256, 256, 512, 512, GLA_LOWRANK, 512, 512, 512, FOX_HEADS)
PEER_HEADS = 8
PEER_NKEYS = 128
PEER_TOPK = 16
PEER_NEXP = PEER_NKEYS * PEER_NKEYS
PEER_J = PEER_HEADS * PEER_TOPK

LANES = 128
Z_MAIN = 3072
Z_SMALL = 128
VMEM_LIMIT = 56 * 1024 * 1024

ZC_GQ, ZC_GK, ZC_GV, ZC_GG, ZC_FQ, ZC_FK, ZC_FV = 0, 2, 4, 8, 12, 16, 20


def _params(sem):
    return pltpu.CompilerParams(dimension_semantics=sem, vmem_limit_bytes=VMEM_LIMIT)


def _tile(n, cap):
    t = cap
    while n % t:
        t //= 2
    return t


def _dot(a, b):
    return jnp.dot(a, b, preferred_element_type=F32)


def _dot_nt(a, b):
    return lax.dot_general(a, b, (((1,), (1,)), ((), ())), preferred_element_type=F32)


def _dot_tn(a, b):
    return lax.dot_general(a, b, (((0,), (0,)), ((), ())), preferred_element_type=F32)


def _split(x):
    hi = x.astype(BF16)
    lo = (x - hi.astype(F32)).astype(BF16)
    return hi, lo


def _dot_exact_rhs(x, m):
    hi, lo = _split(x)
    return _dot(hi, m) + _dot(lo, m)


def _dot_exact_lhs(m, x):
    hi, lo = _split(x)
    return _dot(m, hi) + _dot(m, lo)


def _log_sigmoid(y):
    return jnp.minimum(y, 0.0) - jnp.log(1.0 + jnp.exp(-jnp.abs(y)))


def _const_spec(shape):
    nd = len(shape)
    return pl.BlockSpec(shape, lambda *_: (0,) * nd)


def _inproj_kernel(x_ref, g1_ref, w_ref, wa2_ref, ba_ref, bsm_ref, gq_ref, gk_ref, ind_ref, indt_ref,
                   z_ref, la_ref, k32_ref, v32_ref, lf_ref):
    x = x_ref[...]
    ms = jnp.mean(x * x, axis=-1, keepdims=True)
    h = (x * lax.rsqrt(ms + EPS) * g1_ref[...]).astype(BF16)
    z = _dot(h, w_ref[...])

    def headnorm(t, gain):
        ss = _dot_exact_rhs(t * t, ind_ref[...])
        r = lax.rsqrt(ss * (1.0 / FOX_HD) + EPS)
        return t * _dot_exact_rhs(r, indt_ref[...]) * gain

    fq = headnorm(z[:, 1536:2048], gq_ref[...])
    fk = headnorm(z[:, 2048:2560], gk_ref[...])
    fv = z[:, 2560:3072]
    z_ref[:, 0:1536] = z[:, 0:1536].astype(BF16)
    z_ref[:, 1536:2048] = fq.astype(BF16)
    z_ref[:, 2048:2560] = fk.astype(BF16)
    z_ref[:, 2560:3072] = fv.astype(BF16)
    k32_ref[...] = fk
    v32_ref[...] = fv
    small = z[:, Z_MAIN:Z_MAIN + Z_SMALL]
    lf_ref[...] = _log_sigmoid(small + bsm_ref[...])[:, 0:FOX_HEADS]
    y = _dot(small.astype(BF16), wa2_ref[...]) + ba_ref[...]
    la_ref[...] = _log_sigmoid(y) * (1.0 / GLA_GATE_NORM)


def _inproj(x, g1, w_cat, wa2p, ba, bsm, gq, gk, ind, indt):
    n = x.shape[0]
    tm = _tile(n, 512)
    row = lambda i: (i, 0)
    consts = (g1, w_cat, wa2p, ba, bsm, gq, gk, ind, indt)
    return pl.pallas_call(
        _inproj_kernel,
        grid=(n // tm,),
        in_specs=[pl.BlockSpec((tm, D_MODEL), row)] + [_const_spec(c.shape) for c in consts],
        out_specs=[pl.BlockSpec((tm, Z_MAIN), row), pl.BlockSpec((tm, 256), row),
                   pl.BlockSpec((tm, 512), row), pl.BlockSpec((tm, 512), row),
                   pl.BlockSpec((tm, FOX_HEADS), row)],
        out_shape=[jax.ShapeDtypeStruct((n, Z_MAIN), BF16), jax.ShapeDtypeStruct((n, 256), F32),
                   jax.ShapeDtypeStruct((n, 512), F32), jax.ShapeDtypeStruct((n, 512), F32),
                   jax.ShapeDtypeStruct((n, FOX_HEADS), F32)],
        compiler_params=_params(("parallel",)),
    )(x, *consts)


def _gla_kernel(q_ref, k_ref, v_ref, gg_ref, la_ref, s0_ref, gon_ref, tri_ref, o_ref, sfin_ref, st_scr,
                *, chunk, nchunk):
    t = pl.program_id(1)
    nsub = chunk // GLA_SUB

    @pl.when(t == 0)
    def _():
        for p in range(2):
            st_scr[p] = s0_ref[0, p].T

    rowi = lax.broadcasted_iota(I32, (chunk, LANES), 0)
    lane = lax.broadcasted_iota(I32, (chunk, LANES), 1)
    lane_st = lax.broadcasted_iota(I32, (chunk, nsub * LANES), 1)
    lane_sq = lax.broadcasted_iota(I32, (LANES, LANES), 1)
    arow = lax.broadcasted_iota(I32, (chunk, chunk), 0)
    acol = lax.broadcasted_iota(I32, (chunk, chunk), 1)

    def body(c, carry):
        r0 = pl.multiple_of(c * chunk, chunk)
        rows = pl.ds(r0, chunk)
        bcum = _dot_exact_lhs(tri_ref[...], la_ref[rows, :])
        q = q_ref[rows, :].astype(F32) * (GLA_DK ** -0.5)
        k = k_ref[rows, :].astype(F32)
        for p in range(2):
            cols = slice(p * LANES, (p + 1) * LANES)
            bp, qp, kp = bcum[:, cols], q[:, cols], k[:, cols]
            blast = bp[chunk - 1:chunk, :]
            qs, ks = [], []
            for sb in range(nsub):
                beta = bp[sb * GLA_SUB - 1:sb * GLA_SUB, :] if sb else jnp.zeros((1, LANES), F32)
                inblk = (rowi >= sb * GLA_SUB) & (rowi < (sb + 1) * GLA_SUB)
                qs.append(jnp.where(inblk, qp * jnp.exp(jnp.where(inblk, bp - beta, 0.0)), 0.0))
                valid = rowi < (sb + 1) * GLA_SUB
                ks.append(jnp.where(valid, kp * jnp.exp(jnp.where(valid, beta - bp, 0.0)), 0.0))
            qst = jnp.concatenate(qs, axis=1)
            kst = jnp.concatenate(ks, axis=1).astype(BF16)
            qinter = qp * jnp.exp(bp)
            kdec = (kp * jnp.exp(blast - bp)).astype(BF16)
            st = st_scr[p]
            stb = st.astype(BF16)
            upd = []
            for i in range(2):
                head = 2 * p + i
                hcols = slice(head * GLA_DV, (head + 1) * GLA_DV)
                a = _dot_nt(jnp.where((lane_st & GLA_DK) == i * GLA_DK, qst, 0.0).astype(BF16), kst)
                a = jnp.where(acol <= arow, a, 0.0)
                vh = v_ref[rows, hcols]
                o = _dot(a.astype(BF16), vh)
                o = o + _dot_nt(jnp.where((lane & GLA_DK) == i * GLA_DK, qinter, 0.0).astype(BF16), stb)
                on = o * lax.rsqrt(jnp.mean(o * o, axis=-1, keepdims=True) + EPS) * gon_ref[...]
                gate = gg_ref[rows, hcols].astype(F32)
                o_ref[rows, hcols] = (on * gate * (1.0 / (1.0 + jnp.exp(-gate)))).astype(BF16)
                upd.append(_dot_tn(vh, kdec))
            st_scr[p] = st * jnp.exp(blast) + jnp.where(lane_sq < GLA_DK, upd[0], upd[1])
        return carry

    lax.fori_loop(0, nchunk, body, 0)

    @pl.when(t == pl.num_programs(1) - 1)
    def _():
        for p in range(2):
            sfin_ref[0, p] = st_scr[p].T


def _gla(z, la, s0, gon, *, batch, seq, row0):
    chunk = min(seq, GLA_CHUNK)
    tb = min(seq, 512)
    nt = seq // tb
    blk0 = row0 // tb
    tri = jnp.tril(jnp.ones((chunk, chunk), F32)).astype(BF16)
    rows = lambda col: (lambda b, t: (blk0 + b * nt + t, col))
    state_spec = pl.BlockSpec((1, 2, LANES, LANES), lambda b, t: (b, 0, 0, 0))
    o, sfin = pl.pallas_call(
        functools.partial(_gla_kernel, chunk=chunk, nchunk=tb // chunk),
        grid=(batch, nt),
        in_specs=[pl.BlockSpec((tb, 256), rows(0)), pl.BlockSpec((tb, 256), rows(1)),
                  pl.BlockSpec((tb, 512), rows(1)), pl.BlockSpec((tb, 512), rows(2)),
                  pl.BlockSpec((tb, 256), rows(0)), state_spec,
                  _const_spec((1, GLA_DV)), _const_spec((chunk, chunk))],
        out_specs=[pl.BlockSpec((tb, 512), lambda b, t: (b * nt + t, 0)), state_spec],
        out_shape=[jax.ShapeDtypeStruct((batch * seq, 512), BF16),
                   jax.ShapeDtypeStruct((batch, 2, LANES, LANES), F32)],
        scratch_shapes=[pltpu.VMEM((2, LANES, LANES), F32)],
        compiler_params=_params(("parallel", "arbitrary")),
    )(z, z, z, z, la, s0.reshape(batch, 2, LANES, LANES), gon, tri)
    return o, sfin.reshape(batch, GLA_HEADS, GLA_DK, GLA_DV)


def _fox_kernel(*refs, seq, past, tq, has_past):
    if has_past:
        (q_ref, kc_ref, vc_ref, lfc_ref, triu_ref, kp_ref, vp_ref, lfp_ref, triup_ref,
         o_ref, cc_scr, cp_scr) = refs
    else:
        q_ref, kc_ref, vc_ref, lfc_ref, triu_ref, o_ref, cc_scr = refs
    i = pl.program_id(2)
    nq = seq // tq
    pblk = FOX_BLOCK

    @pl.when(i == 0)
    def _():
        carry = jnp.zeros((2, 1), F32)
        if has_past:
            for jb in range(past // pblk):
                x = lfp_ref[0, 0, :, jb * pblk:(jb + 1) * pblk]
                cp_scr[:, jb * pblk:(jb + 1) * pblk] = _dot_exact_rhs(x, triup_ref[...]) + carry
                carry = carry + jnp.sum(x, axis=1, keepdims=True)
        for jb in range(nq):
            x = lfc_ref[0, 0, :, jb * tq:(jb + 1) * tq]
            cc_scr[:, jb * tq:(jb + 1) * tq] = _dot_exact_rhs(x, triu_ref[...]) + carry
            carry = carry + jnp.sum(x, axis=1, keepdims=True)

    q = q_ref[...]
    lane = lax.broadcasted_iota(I32, (tq, LANES), 1)
    qh = [jnp.where(lane < FOX_HD, q, jnp.zeros_like(q)), jnp.where(lane >= FOX_HD, q, jnp.zeros_like(q))]

    def kv_step(state, kb, vb, crows, mask):
        out = []
        for h in range(2):
            m, l, acc = state[h]
            s = _dot_nt(qh[h], kb) * FOX_SCALE - crows[h]
            if mask is not None:
                s = jnp.where(mask, s, -jnp.inf)
            m_new = jnp.maximum(m, jnp.max(s, axis=1, keepdims=True))
            pr = jnp.exp(s - m_new)
            alpha = jnp.exp(m - m_new)
            l = alpha * l + jnp.sum(pr, axis=1, keepdims=True)
            acc = alpha * acc + _dot(pr.astype(BF16), vb)
            out.append((m_new, l, acc))
        return tuple(out)

    init = tuple((jnp.full((tq, 1), -jnp.inf, F32), jnp.zeros((tq, 1), F32), jnp.zeros((tq, LANES), F32))
                 for _ in range(2))
    state = init
    if has_past:
        def past_body(j, st):
            off = pl.multiple_of(j * pblk, pblk)
            kb = kp_ref[0, pl.ds(off, pblk), :].astype(BF16)
            vb = vp_ref[0, pl.ds(off, pblk), :].astype(BF16)
            crows = [cp_scr[h:h + 1, pl.ds(off, pblk)] for h in range(2)]
            return kv_step(st, kb, vb, crows, None)
        state = lax.fori_loop(0, past // pblk, past_body, state)
    if nq > 1:
        def cur_body(j, st):
            off = pl.multiple_of(j * tq, tq)
            crows = [cc_scr[h:h + 1, pl.ds(off, tq)] for h in range(2)]
            return kv_step(st, kc_ref[pl.ds(off, tq), :], vc_ref[pl.ds(off, tq), :], crows, None)
        state = lax.fori_loop(0, i, cur_body, state)
        off = pl.multiple_of(i * tq, tq)
    else:
        off = 0
    causal = lax.broadcasted_iota(I32, (tq, tq), 1) <= lax.broadcasted_iota(I32, (tq, tq), 0)
    crows = [cc_scr[h:h + 1, pl.ds(off, tq)] for h in range(2)]
    state = kv_step(state, kc_ref[pl.ds(off, tq), :], vc_ref[pl.ds(off, tq), :], crows, causal)
    o0 = state[0][2] / state[0][1]
    o1 = state[1][2] / state[1][1]
    o_ref[...] = jnp.where(lane < FOX_HD, o0, o1).astype(BF16)


def _fox(z, lf_cur_t, *, batch, seq, row0, k_past=None, v_past=None, lf_past_t=None):
    has_past = k_past is not None
    past = k_past.shape[1] if has_past else 0
    tq = min(seq, FOX_BLOCK)
    nq = seq // tq
    qblk0 = row0 // tq
    sblk0 = row0 // seq
    triu = jnp.triu(jnp.ones((tq, tq), F32)).astype(BF16)
    in_specs = [pl.BlockSpec((tq, LANES), lambda b, p, i: (qblk0 + b * nq + i, ZC_FQ + p)),
                pl.BlockSpec((seq, LANES), lambda b, p, i: (sblk0 + b, ZC_FK + p)),
                pl.BlockSpec((seq, LANES), lambda b, p, i: (sblk0 + b, ZC_FV + p)),
                pl.BlockSpec((1, 1, 2, seq), lambda b, p, i: (b, p, 0, 0)),
                _const_spec((tq, tq))]
    args = [z, z, z, lf_cur_t, triu]
    scratch = [pltpu.VMEM((2, seq), F32)]
    if has_past:
        triup = jnp.triu(jnp.ones((FOX_BLOCK, FOX_BLOCK), F32)).astype(BF16)
        in_specs += [pl.BlockSpec((1, past, LANES), lambda b, p, i: (b, 0, p)),
                     pl.BlockSpec((1, past, LANES), lambda b, p, i: (b, 0, p)),
                     pl.BlockSpec((1, 1, 2, past), lambda b, p, i: (b, p, 0, 0)),
                     _const_spec((FOX_BLOCK, FOX_BLOCK))]
        args += [k_past, v_past, lf_past_t, triup]
        scratch.append(pltpu.VMEM((2, past), F32))
    return pl.pallas_call(
        functools.partial(_fox_kernel, seq=seq, past=past, tq=tq, has_past=has_past),
        grid=(batch, FOX_HEADS // 2, nq),
        in_specs=in_specs,
        out_specs=pl.BlockSpec((tq, LANES), lambda b, p, i: (b * nq + i, p)),
        out_shape=jax.ShapeDtypeStruct((batch * seq, 512), BF16),
        scratch_shapes=scratch,
        compiler_params=_params(("parallel", "parallel", "arbitrary")),
    )(*args)


def _outproj_kernel(og_ref, of_ref, x_ref, w_ref, g2_ref, xo_ref, h2_ref):
    mixed = _dot(og_ref[...], w_ref[0:512, :]) + _dot(of_ref[...], w_ref[512:1024, :])
    x = x_ref[...] + mixed
    xo_ref[...] = x
    ms = jnp.mean(x * x, axis=-1, keepdims=True)
    h2_ref[...] = (x * lax.rsqrt(ms + EPS) * g2_ref[...]).astype(BF16)


def _outproj(og, of, x, w, g2):
    n = x.shape[0]
    tm = _tile(n, 512)
    row = lambda i: (i, 0)
    return pl.pallas_call(
        _outproj_kernel,
        grid=(n // tm,),
        in_specs=[pl.BlockSpec((tm, 512), row), pl.BlockSpec((tm, 512), row),
                  pl.BlockSpec((tm, D_MODEL), row), _const_spec(w.shape), _const_spec(g2.shape)],
        out_specs=[pl.BlockSpec((tm, D_MODEL), row), pl.BlockSpec((tm, D_MODEL), row)],
        out_shape=[jax.ShapeDtypeStruct((n, D_MODEL), F32), jax.ShapeDtypeStruct((n, D_MODEL), BF16)],
        compiler_params=_params(("parallel",)),
    )(og, of, x, w, g2)


def _topk_rows(s, k):
    nrows = s.shape[0]
    iota = lax.broadcasted_iota(I32, s.shape, 0)
    vals, idxs = [], []
    for _ in range(k):
        m = jnp.max(s, axis=0, keepdims=True)
        idx = jnp.min(jnp.where(s == m, iota, nrows), axis=0, keepdims=True)
        vals.append(m)
        idxs.append(idx)
        s = jnp.where(iota == idx, -jnp.inf, s)
    return jnp.concatenate(vals, axis=0), jnp.concatenate(idxs, axis=0)


def _route_kernel(h_ref, wq_ref, sk_ref, e1_ref, e2_ref, g_ref, q_scr, e1_scr, e2_scr, g_scr):
    tn = h_ref.shape[0]
    q = _dot(h_ref[...], wq_ref[...])
    for hc in range(2 * PEER_HEADS):
        q_scr[hc] = q[:, hc * LANES:(hc + 1) * LANES].astype(BF16)
    iota_k = lax.broadcasted_iota(I32, (PEER_TOPK, tn), 0)

    def pick(table, sel):
        rows = []
        for r in range(PEER_TOPK):
            rows.append(jnp.sum(jnp.where(iota_k == sel[r:r + 1, :], table, 0), axis=0, keepdims=True))
        return jnp.concatenate(rows, axis=0)

    def head(h, carry):
        s1 = _dot_nt(sk_ref[2 * h], q_scr[2 * h])
        s2 = _dot_nt(sk_ref[2 * h + 1], q_scr[2 * h + 1])
        v1, i1 = _topk_rows(s1, PEER_TOPK)
        v2, i2 = _topk_rows(s2, PEER_TOPK)
        cand = jnp.concatenate([v1[a:a + 1, :] + v2 for a in range(PEER_TOPK)], axis=0)
        top, ti = _topk_rows(cand, PEER_TOPK)
        e1 = pick(i1, ti // PEER_TOPK)
        e2 = pick(i2, ti % PEER_TOPK)
        ex = jnp.exp(top - jnp.max(top, axis=0, keepdims=True))
        g = ex / jnp.sum(ex, axis=0, keepdims=True)
        rows = pl.ds(pl.multiple_of(h * PEER_TOPK, PEER_TOPK), PEER_TOPK)
        e1_scr[rows, :] = e1
        e2_scr[rows, :] = e2
        g_scr[rows, :] = g
        return carry

    lax.fori_loop(0, PEER_HEADS, head, 0)
    e1_ref[...] = e1_scr[...].T
    e2_ref[...] = e2_scr[...].T
    g_ref[...] = g_scr[...].T


def _route(h2, wq, sk):
    n = h2.shape[0]
    tn = _tile(n, 256)
    row = lambda i: (i, 0)
    return pl.pallas_call(
        _route_kernel,
        grid=(n // tn,),
        in_specs=[pl.BlockSpec((tn, D_MODEL), row), _const_spec(wq.shape), _const_spec(sk.shape)],
        out_specs=[pl.BlockSpec((tn, PEER_J), row)] * 3,
        out_shape=[jax.ShapeDtypeStruct((n, PEER_J), I32), jax.ShapeDtypeStruct((n, PEER_J), I32),
                   jax.ShapeDtypeStruct((n, PEER_J), F32)],
        scratch_shapes=[pltpu.VMEM((2 * PEER_HEADS, tn, LANES), BF16), pltpu.VMEM((PEER_J, tn), I32),
                        pltpu.VMEM((PEER_J, tn), I32), pltpu.VMEM((PEER_J, tn), F32)],
        compiler_params=_params(("parallel",)),
    )(h2, wq, sk)


PEER_ROWS = 16
PEER_EC = PEER_ROWS * PEER_NKEYS


def _gelu(a):
    return 0.5 * a * (1.0 + lax.erf(a * (2.0 ** -0.5)))


def _peer_act_kernel(h_ref, ut_ref, e1_ref, e2_ref, g_ref, c_ref, acc_scr):
    c = pl.program_id(1)

    @pl.when(c == 0)
    def _():
        acc_scr[...] = jnp.zeros_like(acc_scr)

    a_all = _dot(h_ref[...], ut_ref[...])
    e1 = e1_ref[...]
    e2 = e2_ref[...]
    acc = acc_scr[...]
    for r in range(PEER_ROWS):
        picked = jnp.take_along_axis(a_all[:, r * LANES:(r + 1) * LANES], e2, axis=1)
        acc = acc + jnp.where(e1 == c * PEER_ROWS + r, picked, 0.0)
    acc_scr[...] = acc

    @pl.when(c == pl.num_programs(1) - 1)
    def _():
        c_ref[...] = g_ref[...] * _gelu(acc)


def _peer_act(h2, ut, e1, e2, g):
    n = h2.shape[0]
    tn = _tile(n, 512)
    row = lambda i, c: (i, 0)
    return pl.pallas_call(
        _peer_act_kernel,
        grid=(n // tn, PEER_NEXP // PEER_EC),
        in_specs=[pl.BlockSpec((tn, D_MODEL), row), pl.BlockSpec((D_MODEL, PEER_EC), lambda i, c: (0, c)),
                  pl.BlockSpec((tn, PEER_J), row), pl.BlockSpec((tn, PEER_J), row),
                  pl.BlockSpec((tn, PEER_J), row)],
        out_specs=pl.BlockSpec((tn, PEER_J), row),
        out_shape=jax.ShapeDtypeStruct((n, PEER_J), F32),
        scratch_shapes=[pltpu.VMEM((tn, PEER_J), F32)],
        compiler_params=_params(("parallel", "arbitrary")),
    )(h2, ut, e1, e2, g)


PEER_TOK_STRIDE = PEER_ROWS + 8


def _peer_out_kernel(e1_ref, e2_ref, c_ref, v_ref, x_ref, o_ref, y_scr, acc_scr):
    tn = x_ref.shape[0]
    c = pl.program_id(1)
    nchunks = PEER_NEXP // PEER_EC

    @pl.when(c == 0)
    def _():
        acc_scr[...] = jnp.zeros_like(acc_scr)
        key = lax.broadcasted_iota(I32, (PEER_NKEYS, PEER_J), 0)

        def token(t, carry):
            row = pl.ds(t, 1)
            d = jnp.where(key == e1_ref[row, :], c_ref[row, :], 0.0).astype(BF16)
            w = jnp.where(key == e2_ref[row, :], 1.0, 0.0).astype(BF16)
            y = _dot_nt(d, w)
            base = pl.multiple_of(t * PEER_TOK_STRIDE, 8)
            for cc in range(nchunks):
                y_scr[cc, pl.ds(base, PEER_ROWS), :] = y[cc * PEER_ROWS:(cc + 1) * PEER_ROWS, :]
            return carry

        lax.fori_loop(0, tn, token, 0)

    slabs = [y_scr[c, pl.ds(r, tn, stride=PEER_TOK_STRIDE), :].astype(BF16) for r in range(PEER_ROWS)]
    acc_scr[...] += _dot(jnp.concatenate(slabs, axis=1), v_ref[...])

    @pl.when(c == nchunks - 1)
    def _():
        o_ref[...] = x_ref[...] + acc_scr[...]


def _peer_out(e1, e2, cj, vt, x):
    n = x.shape[0]
    tn = _tile(n, 256)
    row = lambda i, c: (i, 0)
    nchunks = PEER_NEXP // PEER_EC
    return pl.pallas_call(
        _peer_out_kernel,
        grid=(n // tn, nchunks),
        in_specs=[pl.BlockSpec((tn, PEER_J), row), pl.BlockSpec((tn, PEER_J), row),
                  pl.BlockSpec((tn, PEER_J), row), pl.BlockSpec((PEER_EC, D_MODEL), lambda i, c: (c, 0)),
                  pl.BlockSpec((tn, D_MODEL), row)],
        out_specs=pl.BlockSpec((tn, D_MODEL), row),
        out_shape=jax.ShapeDtypeStruct((n, D_MODEL), F32),
        scratch_shapes=[pltpu.VMEM((nchunks, tn * PEER_TOK_STRIDE, LANES), F32),
                        pltpu.VMEM((tn, D_MODEL), F32)],
        compiler_params=_params(("parallel", "arbitrary")),
    )(e1, e2, cj, vt, x)


def _pair_major(lf, batch, length):
    return lf.reshape(batch, length, FOX_HEADS // 2, 2).transpose(0, 2, 3, 1)


def _layer(x, dims, s0_s, k_past, v_past, lf_past, g_norm1, w_in, w_gla_a2, b_gla_a, g_gla_onorm,
           g_fox_qnorm, g_fox_knorm, b_fox_f, w_out, g_norm2, w_peer_q, peer_subkeys, peer_u, peer_v):
    bp, tp, bs, ts = dims
    n_p = bp * tp
    past = k_past.shape[1]
    bounds = [0]
    for s in IN_SIZES:
        bounds.append(bounds[-1] + s)
    cols = [w_in[:, bounds[i]:bounds[i + 1]] for i in range(len(IN_SIZES))]
    gq, gk, gv, gg, glr, fq, fk, fv, ff = cols
    pad = jnp.zeros((D_MODEL, Z_SMALL - FOX_HEADS - GLA_LOWRANK), w_in.dtype)
    w_cat = jnp.concatenate([gq, gk, gv, gg, fq, fk, fv, ff, glr, pad], axis=1).astype(BF16)
    wa2p = jnp.zeros((Z_SMALL, 256), F32).at[FOX_HEADS:FOX_HEADS + GLA_LOWRANK].set(w_gla_a2).astype(BF16)
    bsm = jnp.zeros((1, Z_SMALL), F32).at[0, :FOX_HEADS].set(b_fox_f)
    head_of_col = jnp.arange(FOX_HEADS * FOX_HD) // FOX_HD
    ind = (head_of_col[:, None] == jnp.arange(LANES)[None, :]).astype(BF16)
    z, la, k32, v32, lf = _inproj(
        x, g_norm1[None], w_cat, wa2p, b_gla_a[None], bsm,
        jnp.tile(g_fox_qnorm, FOX_HEADS)[None], jnp.tile(g_fox_knorm, FOX_HEADS)[None], ind, ind.T)

    gon = g_gla_onorm[None]
    og_p, st_p = _gla(z, la, jnp.zeros((bp, GLA_HEADS, GLA_DK, GLA_DV), F32), gon, batch=bp, seq=tp, row0=0)
    og_s, st_s = _gla(z, la, s0_s, gon, batch=bs, seq=ts, row0=n_p)
    of_p = _fox(z, _pair_major(lf[:n_p], bp, tp), batch=bp, seq=tp, row0=0)
    of_s = _fox(z, _pair_major(lf[n_p:], bs, ts), batch=bs, seq=ts, row0=n_p,
                k_past=k_past.reshape(bs, past, FOX_HEADS * FOX_HD),
                v_past=v_past.reshape(bs, past, FOX_HEADS * FOX_HD),
                lf_past_t=lf_past.reshape(bs, past, FOX_HEADS // 2, 2).transpose(0, 2, 3, 1))
    og = jnp.concatenate([og_p, og_s], axis=0)
    of = jnp.concatenate([of_p, of_s], axis=0)
    x, h2 = _outproj(og, of, x, w_out.astype(BF16), g_norm2[None])

    sk = peer_subkeys.reshape(2 * PEER_HEADS, PEER_NKEYS, LANES).astype(BF16)
    e1, e2, g = _route(h2, w_peer_q.astype(BF16), sk)
    cj = _peer_act(h2, peer_u.T.astype(BF16), e1, e2, g)
    x = _peer_out(e1, e2, cj, peer_v.astype(BF16), x)

    def heads(a, b, t):
        return a.reshape(b, t, FOX_HEADS, FOX_HD)

    outs_p = (heads(k32[:n_p], bp, tp), heads(v32[:n_p], bp, tp), lf[:n_p].reshape(bp, tp, FOX_HEADS), st_p)
    outs_s = (heads(k32[n_p:], bs, ts), heads(v32[n_p:], bs, ts), lf[n_p:].reshape(bs, ts, FOX_HEADS), st_s)
    return x, outs_p, outs_s


def kernel(x_prompt, x_sample, cache_fox_k, cache_fox_v, cache_fox_logf, state_gla, g_norm1, w_in, w_gla_a2,
           b_gla_a, g_gla_onorm, g_fox_qnorm, g_fox_knorm, b_fox_f, w_out, g_norm2, w_peer_q, peer_subkeys,
           peer_u, peer_v):
    bp, tp, _ = x_prompt.shape
    bs, ts, _ = x_sample.shape
    n_p = bp * tp
    x = jnp.concatenate([x_prompt.reshape(n_p, D_MODEL), x_sample.reshape(bs * ts, D_MODEL)], axis=0)
    per_p, per_s = [], []
    for l in range(w_in.shape[0]):
        x, outs_p, outs_s = _layer(
            x, (bp, tp, bs, ts), state_gla[l], cache_fox_k[l], cache_fox_v[l], cache_fox_logf[l],
            g_norm1[l], w_in[l], w_gla_a2[l], b_gla_a[l], g_gla_onorm[l], g_fox_qnorm[l], g_fox_knorm[l],
            b_fox_f[l], w_out[l], g_norm2[l], w_peer_q[l], peer_subkeys[l], peer_u[l], peer_v[l])
        per_p.append(outs_p)
        per_s.append(outs_s)
    stack = lambda per, i: jnp.stack([o[i] for o in per])
    return (x[:n_p].reshape(bp, tp, D_MODEL), x[n_p:].reshape(bs, ts, D_MODEL),
            stack(per_p, 0), stack(per_p, 1), stack(per_p, 2), stack(per_p, 3),
            stack(per_s, 0), stack(per_s, 1), stack(per_s, 2), stack(per_s, 3))
```

```python
import functools

import jax
import jax.numpy as jnp
from jax import lax
from jax.experimental import pallas as pl
from jax.experimental.pallas import tpu as pltpu

F32 = jnp.float32
BF16 = jnp.bfloat16
I32 = jnp.int32

D_MODEL = 1024
EPS = 1e-6
GLA_CHUNK = 64
GLA_HEADS = 4
GLA_DK = 64
GLA_DV = 128
GLA_LOWRANK = 16
GLA_GATE_NORM = 16.0
GLA_SUB = 16
FOX_HEADS = 8
FOX_HD = 64
FOX_SCALE = FOX_HD ** -0.5
FOX_BLOCK = 512
IN_SIZES = (256, 256, 512, 512, GLA_LOWRANK, 512, 512, 512, FOX_HEADS)
PEER_HEADS = 8
PEER_NKEYS = 128
PEER_TOPK = 16
PEER_NEXP = PEER_NKEYS * PEER_NKEYS
PEER_J = PEER_HEADS * PEER_TOPK

LANES = 128
Z_MAIN = 3072
Z_SMALL = 128
VMEM_LIMIT = 56 * 1024 * 1024

ZC_GQ, ZC_GK, ZC_GV, ZC_GG, ZC_FQ, ZC_FK, ZC_FV = 0, 2, 4, 8, 12, 16, 20


def _params(sem):
    return pltpu.CompilerParams(dimension_semantics=sem, vmem_limit_bytes=VMEM_LIMIT)


def _tile(n, cap):
    t = cap
    while n % t:
        t //= 2
    return t


def _dot(a, b):
    return jnp.dot(a, b, preferred_element_type=F32)


def _dot_nt(a, b):
    return lax.dot_general(a, b, (((1,), (1,)), ((), ())), preferred_element_type=F32)


def _dot_tn(a, b):
    return lax.dot_general(a, b, (((0,), (0,)), ((), ())), preferred_element_type=F32)


def _split(x):
    hi = x.astype(BF16)
    lo = (x - hi.astype(F32)).astype(BF16)
    return hi, lo


def _dot_exact_rhs(x, m):
    hi, lo = _split(x)
    return _dot(hi, m) + _dot(lo, m)


def _dot_exact_lhs(m, x):
    hi, lo = _split(x)
    return _dot(m, hi) + _dot(m, lo)


def _log_sigmoid(y):
    return jnp.minimum(y, 0.0) - jnp.log(1.0 + jnp.exp(-jnp.abs(y)))


def _const_spec(shape):
    nd = len(shape)
    return pl.BlockSpec(shape, lambda *_: (0,) * nd)


def _inproj_kernel(x_ref, g1_ref, w_ref, wa2_ref, ba_ref, bsm_ref, gq_ref, gk_ref, ind_ref, indt_ref,
                   z_ref, la_ref, k32_ref, v32_ref, lf_ref):
    x = x_ref[...]
    ms = jnp.mean(x * x, axis=-1, keepdims=True)
    h = (x * lax.rsqrt(ms + EPS) * g1_ref[...]).astype(BF16)
    z = _dot(h, w_ref[...])

    def headnorm(t, gain):
        ss = _dot_exact_rhs(t * t, ind_ref[...])
        r = lax.rsqrt(ss * (1.0 / FOX_HD) + EPS)
        return t * _dot_exact_rhs(r, indt_ref[...]) * gain

    fq = headnorm(z[:, 1536:2048], gq_ref[...])
    fk = headnorm(z[:, 2048:2560], gk_ref[...])
    fv = z[:, 2560:3072]
    z_ref[:, 0:1536] = z[:, 0:1536].astype(BF16)
    z_ref[:, 1536:2048] = fq.astype(BF16)
    z_ref[:, 2048:2560] = fk.astype(BF16)
    z_ref[:, 2560:3072] = fv.astype(BF16)
    k32_ref[...] = fk
    v32_ref[...] = fv
    small = z[:, Z_MAIN:Z_MAIN + Z_SMALL]
    lf_ref[...] = _log_sigmoid(small + bsm_ref[...])[:, 0:FOX_HEADS]
    y = _dot(small.astype(BF16), wa2_ref[...]) + ba_ref[...]
    la_ref[...] = _log_sigmoid(y) * (1.0 / GLA_GATE_NORM)


def _inproj(x, g1, w_cat, wa2p, ba, bsm, gq, gk, ind, indt):
    n = x.shape[0]
    tm = _tile(n, 512)
    row = lambda i: (i, 0)
    consts = (g1, w_cat, wa2p, ba, bsm, gq, gk, ind, indt)
    return pl.pallas_call(
        _inproj_kernel,
        grid=(n // tm,),
        in_specs=[pl.BlockSpec((tm, D_MODEL), row)] + [_const_spec(c.shape) for c in consts],
        out_specs=[pl.BlockSpec((tm, Z_MAIN), row), pl.BlockSpec((tm, 256), row),
                   pl.BlockSpec((tm, 512), row), pl.BlockSpec((tm, 512), row),
                   pl.BlockSpec((tm, FOX_HEADS), row)],
        out_shape=[jax.ShapeDtypeStruct((n, Z_MAIN), BF16), jax.ShapeDtypeStruct((n, 256), F32),
                   jax.ShapeDtypeStruct((n, 512), F32), jax.ShapeDtypeStruct((n, 512), F32),
                   jax.ShapeDtypeStruct((n, FOX_HEADS), F32)],
        compiler_params=_params(("parallel",)),
    )(x, *consts)


def _gla_kernel(q_ref, k_ref, v_ref, gg_ref, la_ref, s0_ref, gon_ref, tri_ref, o_ref, sfin_ref, st_scr,
                *, chunk, nchunk):
    t = pl.program_id(1)
    nsub = chunk // GLA_SUB

    @pl.when(t == 0)
    def _():
        for p in range(2):
            st_scr[p] = s0_ref[0, p].T

    rowi = lax.broadcasted_iota(I32, (chunk, LANES), 0)
    lane = lax.broadcasted_iota(I32, (chunk, LANES), 1)
    lane_st = lax.broadcasted_iota(I32, (chunk, nsub * LANES), 1)
    lane_sq = lax.broadcasted_iota(I32, (LANES, LANES), 1)
    arow = lax.broadcasted_iota(I32, (chunk, chunk), 0)
    acol = lax.broadcasted_iota(I32, (chunk, chunk), 1)

    def body(c, carry):
        r0 = pl.multiple_of(c * chunk, chunk)
        rows = pl.ds(r0, chunk)
        bcum = _dot_exact_lhs(tri_ref[...], la_ref[rows, :])
        q = q_ref[rows, :].astype(F32) * (GLA_DK ** -0.5)
        k = k_ref[rows, :].astype(F32)
        for p in range(2):
            cols = slice(p * LANES, (p + 1) * LANES)
            bp, qp, kp = bcum[:, cols], q[:, cols], k[:, cols]
            blast = bp[chunk - 1:chunk, :]
            qs, ks = [], []
            for sb in range(nsub):
                beta = bp[sb * GLA_SUB - 1:sb * GLA_SUB, :] if sb else jnp.zeros((1, LANES), F32)
                inblk = (rowi >= sb * GLA_SUB) & (rowi < (sb + 1) * GLA_SUB)
                qs.append(jnp.where(inblk, qp * jnp.exp(jnp.where(inblk, bp - beta, 0.0)), 0.0))
                valid = rowi < (sb + 1) * GLA_SUB
                ks.append(jnp.where(valid, kp * jnp.exp(jnp.where(valid, beta - bp, 0.0)), 0.0))
            qst = jnp.concatenate(qs, axis=1)
            kst = jnp.concatenate(ks, axis=1).astype(BF16)
            qinter = qp * jnp.exp(bp)
            kdec = (kp * jnp.exp(blast - bp)).astype(BF16)
            st = st_scr[p]
            stb = st.astype(BF16)
            upd = []
            for i in range(2):
                head = 2 * p + i
                hcols = slice(head * GLA_DV, (head + 1) * GLA_DV)
                a = _dot_nt(jnp.where((lane_st & GLA_DK) == i * GLA_DK, qst, 0.0).astype(BF16), kst)
                a = jnp.where(acol <= arow, a, 0.0)
                vh = v_ref[rows, hcols]
                o = _dot(a.astype(BF16), vh)
                o = o + _dot_nt(jnp.where((lane & GLA_DK) == i * GLA_DK, qinter, 0.0).astype(BF16), stb)
                on = o * lax.rsqrt(jnp.mean(o * o, axis=-1, keepdims=True) + EPS) * gon_ref[...]
                gate = gg_ref[rows, hcols].astype(F32)
                o_ref[rows, hcols] = (on * gate * (1.0 / (1.0 + jnp.exp(-gate)))).astype(BF16)
                upd.append(_dot_tn(vh, kdec))
            st_scr[p] = st * jnp.exp(blast) + jnp.where(lane_sq < GLA_DK, upd[0], upd[1])
        return carry

    lax.fori_loop(0, nchunk, body, 0)

    @pl.when(t == pl.num_programs(1) - 1)
    def _():
        for p in range(2):
            sfin_ref[0, p] = st_scr[p].T


def _gla(z, la, s0, gon, *, batch, seq, row0):
    chunk = min(seq, GLA_CHUNK)
    tb = min(seq, 512)
    nt = seq // tb
    blk0 = row0 // tb
    tri = jnp.tril(jnp.ones((chunk, chunk), F32)).astype(BF16)
    rows = lambda col: (lambda b, t: (blk0 + b * nt + t, col))
    state_spec = pl.BlockSpec((1, 2, LANES, LANES), lambda b, t: (b, 0, 0, 0))
    o, sfin = pl.pallas_call(
        functools.partial(_gla_kernel, chunk=chunk, nchunk=tb // chunk),
        grid=(batch, nt),
        in_specs=[pl.BlockSpec((tb, 256), rows(0)), pl.BlockSpec((tb, 256), rows(1)),
                  pl.BlockSpec((tb, 512), rows(1)), pl.BlockSpec((tb, 512), rows(2)),
                  pl.BlockSpec((tb, 256), rows(0)), state_spec,
                  _const_spec((1, GLA_DV)), _const_spec((chunk, chunk))],
        out_specs=[pl.BlockSpec((tb, 512), lambda b, t: (b * nt + t, 0)), state_spec],
        out_shape=[jax.ShapeDtypeStruct((batch * seq, 512), BF16),
                   jax.ShapeDtypeStruct((batch, 2, LANES, LANES), F32)],
        scratch_shapes=[pltpu.VMEM((2, LANES, LANES), F32)],
        compiler_params=_params(("parallel", "arbitrary")),
    )(z, z, z, z, la, s0.reshape(batch, 2, LANES, LANES), gon, tri)
    return o, sfin.reshape(batch, GLA_HEADS, GLA_DK, GLA_DV)


def _fox_kernel(*refs, seq, past, tq, has_past):
    if has_past:
        (q_ref, kc_ref, vc_ref, lfc_ref, triu_ref, kp_ref, vp_ref, lfp_ref, triup_ref,
         o_ref, cc_scr, cp_scr) = refs
    else:
        q_ref, kc_ref, vc_ref, lfc_ref, triu_ref, o_ref, cc_scr = refs
    i = pl.program_id(2)
    nq = seq // tq
    pblk = FOX_BLOCK

    @pl.when(i == 0)
    def _():
        carry = jnp.zeros((2, 1), F32)
        if has_past:
            for jb in range(past // pblk):
                x = lfp_ref[0, 0, :, jb * pblk:(jb + 1) * pblk]
                cp_scr[:, jb * pblk:(jb + 1) * pblk] = _dot_exact_rhs(x, triup_ref[...]) + carry
                carry = carry + jnp.sum(x, axis=1, keepdims=True)
        for jb in range(nq):
            x = lfc_ref[0, 0, :, jb * tq:(jb + 1) * tq]
            cc_scr[:, jb * tq:(jb + 1) * tq] = _dot_exact_rhs(x, triu_ref[...]) + carry
            carry = carry + jnp.sum(x, axis=1, keepdims=True)

    q = q_ref[...]
    lane = lax.broadcasted_iota(I32, (tq, LANES), 1)
    qh = [jnp.where(lane < FOX_HD, q, jnp.zeros_like(q)), jnp.where(lane >= FOX_HD, q, jnp.zeros_like(q))]

    def kv_step(state, kb, vb, crows, mask):
        out = []
        for h in range(2):
            m, l, acc = state[h]
            s = _dot_nt(qh[h], kb) * FOX_SCALE - crows[h]
            if mask is not None:
                s = jnp.where(mask, s, -jnp.inf)
            m_new = jnp.maximum(m, jnp.max(s, axis=1, keepdims=True))
            pr = jnp.exp(s - m_new)
            alpha = jnp.exp(m - m_new)
            l = alpha * l + jnp.sum(pr, axis=1, keepdims=True)
            acc = alpha * acc + _dot(pr.astype(BF16), vb)
            out.append((m_new, l, acc))
        return tuple(out)

    init = tuple((jnp.full((tq, 1), -jnp.inf, F32), jnp.zeros((tq, 1), F32), jnp.zeros((tq, LANES), F32))
                 for _ in range(2))
    state = init
    if has_past:
        def past_body(j, st):
            off = pl.multiple_of(j * pblk, pblk)
            kb = kp_ref[0, pl.ds(off, pblk), :].astype(BF16)
            vb = vp_ref[0, pl.ds(off, pblk), :].astype(BF16)
            crows = [cp_scr[h:h + 1, pl.ds(off, pblk)] for h in range(2)]
            return kv_step(st, kb, vb, crows, None)
        state = lax.fori_loop(0, past // pblk, past_body, state)
    if nq > 1:
        def cur_body(j, st):
            off = pl.multiple_of(j * tq, tq)
            crows = [cc_scr[h:h + 1, pl.ds(off, tq)] for h in range(2)]
            return kv_step(st, kc_ref[pl.ds(off, tq), :], vc_ref[pl.ds(off, tq), :], crows, None)
        state = lax.fori_loop(0, i, cur_body, state)
        off = pl.multiple_of(i * tq, tq)
    else:
        off = 0
    causal = lax.broadcasted_iota(I32, (tq, tq), 1) <= lax.broadcasted_iota(I32, (tq, tq), 0)
    crows = [cc_scr[h:h + 1, pl.ds(off, tq)] for h in range(2)]
    state = kv_step(state, kc_ref[pl.ds(off, tq), :], vc_ref[pl.ds(off, tq), :], crows, causal)
    o0 = state[0][2] / state[0][1]
    o1 = state[1][2] / state[1][1]
    o_ref[...] = jnp.where(lane < FOX_HD, o0, o1).astype(BF16)


def _fox(z, lf_cur_t, *, batch, seq, row0, k_past=None, v_past=None, lf_past_t=None):
    has_past = k_past is not None
    past = k_past.shape[1] if has_past else 0
    tq = min(seq, FOX_BLOCK)
    nq = seq // tq
    qblk0 = row0 // tq
    sblk0 = row0 // seq
    triu = jnp.triu(jnp.ones((tq, tq), F32)).astype(BF16)
    in_specs = [pl.BlockSpec((tq, LANES), lambda b, p, i: (qblk0 + b * nq + i, ZC_FQ + p)),
                pl.BlockSpec((seq, LANES), lambda b, p, i: (sblk0 + b, ZC_FK + p)),
                pl.BlockSpec((seq, LANES), lambda b, p, i: (sblk0 + b, ZC_FV + p)),
                pl.BlockSpec((1, 1, 2, seq), lambda b, p, i: (b, p, 0, 0)),
                _const_spec((tq, tq))]
    args = [z, z, z, lf_cur_t, triu]
    scratch = [pltpu.VMEM((2, seq), F32)]
    if has_past:
        triup = jnp.triu(jnp.ones((FOX_BLOCK, FOX_BLOCK), F32)).astype(BF16)
        in_specs += [pl.BlockSpec((1, past, LANES), lambda b, p, i: (b, 0, p)),
                     pl.BlockSpec((1, past, LANES), lambda b, p, i: (b, 0, p)),
                     pl.BlockSpec((1, 1, 2, past), lambda b, p, i: (b, p, 0, 0)),
                     _const_spec((FOX_BLOCK, FOX_BLOCK))]
        args += [k_past, v_past, lf_past_t, triup]
        scratch.append(pltpu.VMEM((2, past), F32))
    return pl.pallas_call(
        functools.partial(_fox_kernel, seq=seq, past=past, tq=tq, has_past=has_past),
        grid=(batch, FOX_HEADS // 2, nq),
        in_specs=in_specs,
        out_specs=pl.BlockSpec((tq, LANES), lambda b, p, i: (b * nq + i, p)),
        out_shape=jax.ShapeDtypeStruct((batch * seq, 512), BF16),
        scratch_shapes=scratch,
        compiler_params=_params(("parallel", "parallel", "arbitrary")),
    )(*args)


def _outproj_kernel(og_ref, of_ref, x_ref, w_ref, g2_ref, xo_ref, h2_ref):
    mixed = _dot(og_ref[...], w_ref[0:512, :]) + _dot(of_ref[...], w_ref[512:1024, :])
    x = x_ref[...] + mixed
    xo_ref[...] = x
    ms = jnp.mean(x * x, axis=-1, keepdims=True)
    h2_ref[...] = (x * lax.rsqrt(ms + EPS) * g2_ref[...]).astype(BF16)


def _outproj(og, of, x, w, g2):
    n = x.shape[0]
    tm = _tile(n, 512)
    row = lambda i: (i, 0)
    return pl.pallas_call(
        _outproj_kernel,
        grid=(n // tm,),
        in_specs=[pl.BlockSpec((tm, 512), row), pl.BlockSpec((tm, 512), row),
                  pl.BlockSpec((tm, D_MODEL), row), _const_spec(w.shape), _const_spec(g2.shape)],
        out_specs=[pl.BlockSpec((tm, D_MODEL), row), pl.BlockSpec((tm, D_MODEL), row)],
        out_shape=[jax.ShapeDtypeStruct((n, D_MODEL), F32), jax.ShapeDtypeStruct((n, D_MODEL), BF16)],
        compiler_params=_params(("parallel",)),
    )(og, of, x, w, g2)


def _topk_rows(s, k):
    nrows = s.shape[0]
    iota = lax.broadcasted_iota(I32, s.shape, 0)
    vals, idxs = [], []
    for _ in range(k):
        m = jnp.max(s, axis=0, keepdims=True)
        idx = jnp.min(jnp.where(s == m, iota, nrows), axis=0, keepdims=True)
        vals.append(m)
        idxs.append(idx)
        s = jnp.where(iota == idx, -jnp.inf, s)
    return jnp.concatenate(vals, axis=0), jnp.concatenate(idxs, axis=0)


def _route_kernel(h_ref, wq_ref, sk_ref, e1_ref, e2_ref, g_ref, q_scr, e1_scr, e2_scr, g_scr):
    tn = h_ref.shape[0]
    q = _dot(h_ref[...], wq_ref[...])
    for hc in range(2 * PEER_HEADS):
        q_scr[hc] = q[:, hc * LANES:(hc + 1) * LANES].astype(BF16)
    iota_k = lax.broadcasted_iota(I32, (PEER_TOPK, tn), 0)
    iota_8 = lax.broadcasted_iota(I32, (8, tn), 0)

    def pick(table, sel):
        rows = []
        for r in range(PEER_TOPK):
            rows.append(jnp.sum(jnp.where(iota_k == sel[r:r + 1, :], table, 0), axis=0, keepdims=True))
        return jnp.concatenate(rows, axis=0)

    def head(h, carry):
        s1 = _dot_nt(sk_ref[2 * h], q_scr[2 * h])
        s2 = _dot_nt(sk_ref[2 * h + 1], q_scr[2 * h + 1])
        v1, i1 = _topk_rows(s1, PEER_TOPK)
        v2, i2 = _topk_rows(s2, PEER_TOPK)
        blocks = [v1[0:1, :] + v2]
        for a in range(1, 8):
            blocks.append(jnp.where(iota_8 < PEER_TOPK // (a + 1), v1[a:a + 1, :] + v2[0:8, :], -jnp.inf))
        blocks.append(v1[8:16, :] + v2[0:1, :])
        top, ti = _topk_rows(jnp.concatenate(blocks, axis=0), PEER_TOPK)
        mid = ti - PEER_TOPK
        e1 = pick(i1, jnp.where(ti < 16, 0, jnp.where(ti < 72, (mid >> 3) + 1, ti - 64)))
        e2 = pick(i2, jnp.where(ti < 16, ti, jnp.where(ti < 72, mid & 7, 0)))
        ex = jnp.exp(top - jnp.max(top, axis=0, keepdims=True))
        g = ex / jnp.sum(ex, axis=0, keepdims=True)
        rows = pl.ds(pl.multiple_of(h * PEER_TOPK, PEER_TOPK), PEER_TOPK)
        e1_scr[rows, :] = e1
        e2_scr[rows, :] = e2
        g_scr[rows, :] = g
        return carry

    lax.fori_loop(0, PEER_HEADS, head, 0)
    e1_ref[...] = e1_scr[...].T
    e2_ref[...] = e2_scr[...].T
    g_ref[...] = g_scr[...].T


def _route(h2, wq, sk):
    n = h2.shape[0]
    tn = _tile(n, 256)
    row = lambda i: (i, 0)
    return pl.pallas_call(
        _route_kernel,
        grid=(n // tn,),
        in_specs=[pl.BlockSpec((tn, D_MODEL), row), _const_spec(wq.shape), _const_spec(sk.shape)],
        out_specs=[pl.BlockSpec((tn, PEER_J), row)] * 3,
        out_shape=[jax.ShapeDtypeStruct((n, PEER_J), I32), jax.ShapeDtypeStruct((n, PEER_J), I32),
                   jax.ShapeDtypeStruct((n, PEER_J), F32)],
        scratch_shapes=[pltpu.VMEM((2 * PEER_HEADS, tn, LANES), BF16), pltpu.VMEM((PEER_J, tn), I32),
                        pltpu.VMEM((PEER_J, tn), I32), pltpu.VMEM((PEER_J, tn), F32)],
        compiler_params=_params(("parallel",)),
    )(h2, wq, sk)


PEER_ROWS = 16
PEER_EC = PEER_ROWS * PEER_NKEYS


def _gelu(a):
    return 0.5 * a * (1.0 + lax.erf(a * (2.0 ** -0.5)))


def _peer_act_kernel(h_ref, ut_ref, e1_ref, e2_ref, g_ref, c_ref, acc_scr):
    c = pl.program_id(1)

    @pl.when(c == 0)
    def _():
        acc_scr[...] = jnp.zeros_like(acc_scr)

    a_all = _dot(h_ref[...], ut_ref[...])
    e1 = e1_ref[...]
    e2 = e2_ref[...]
    acc = acc_scr[...]
    for r in range(PEER_ROWS):
        picked = jnp.take_along_axis(a_all[:, r * LANES:(r + 1) * LANES], e2, axis=1)
        acc = acc + jnp.where(e1 == c * PEER_ROWS + r, picked, 0.0)
    acc_scr[...] = acc

    @pl.when(c == pl.num_programs(1) - 1)
    def _():
        c_ref[...] = g_ref[...] * _gelu(acc)


def _peer_act(h2, ut, e1, e2, g):
    n = h2.shape[0]
    tn = _tile(n, 512)
    row = lambda i, c: (i, 0)
    return pl.pallas_call(
        _peer_act_kernel,
        grid=(n // tn, PEER_NEXP // PEER_EC),
        in_specs=[pl.BlockSpec((tn, D_MODEL), row), pl.BlockSpec((D_MODEL, PEER_EC), lambda i, c: (0, c)),
                  pl.BlockSpec((tn, PEER_J), row), pl.BlockSpec((tn, PEER_J), row),
                  pl.BlockSpec((tn, PEER_J), row)],
        out_specs=pl.BlockSpec((tn, PEER_J), row),
        out_shape=jax.ShapeDtypeStruct((n, PEER_J), F32),
        scratch_shapes=[pltpu.VMEM((tn, PEER_J), F32)],
        compiler_params=_params(("parallel", "arbitrary")),
    )(h2, ut, e1, e2, g)


PEER_TOK_STRIDE = PEER_ROWS + 8


def _peer_out_kernel(e1_ref, e2_ref, c_ref, v_ref, x_ref, o_ref, y_scr, acc_scr):
    tn = x_ref.shape[0]
    c = pl.program_id(1)
    nchunks = PEER_NEXP // PEER_EC

    @pl.when(c == 0)
    def _():
        acc_scr[...] = jnp.zeros_like(acc_scr)
        key = lax.broadcasted_iota(I32, (PEER_NKEYS, PEER_J), 0)

        def token(t, carry):
            row = pl.ds(t, 1)
            d = jnp.where(key == e1_ref[row, :], c_ref[row, :], 0.0).astype(BF16)
            w = jnp.where(key == e2_ref[row, :], 1.0, 0.0).astype(BF16)
            y = _dot_nt(d, w)
            base = pl.multiple_of(t * PEER_TOK_STRIDE, 8)
            for cc in range(nchunks):
                y_scr[cc, pl.ds(base, PEER_ROWS), :] = y[cc * PEER_ROWS:(cc + 1) * PEER_ROWS, :]
            return carry

        lax.fori_loop(0, tn, token, 0, unroll=8)

    slabs = [y_scr[c, pl.ds(r, tn, stride=PEER_TOK_STRIDE), :].astype(BF16) for r in range(PEER_ROWS)]
    acc_scr[...] += _dot(jnp.concatenate(slabs, axis=1), v_ref[...])

    @pl.when(c == nchunks - 1)
    def _():
        o_ref[...] = x_ref[...] + acc_scr[...]


def _peer_out(e1, e2, cj, vt, x):
    n = x.shape[0]
    tn = _tile(n, 256)
    row = lambda i, c: (i, 0)
    nchunks = PEER_NEXP // PEER_EC
    return pl.pallas_call(
        _peer_out_kernel,
        grid=(n // tn, nchunks),
        in_specs=[pl.BlockSpec((tn, PEER_J), row), pl.BlockSpec((tn, PEER_J), row),
                  pl.BlockSpec((tn, PEER_J), row), pl.BlockSpec((PEER_EC, D_MODEL), lambda i, c: (c, 0)),
                  pl.BlockSpec((tn, D_MODEL), row)],
        out_specs=pl.BlockSpec((tn, D_MODEL), row),
        out_shape=jax.ShapeDtypeStruct((n, D_MODEL), F32),
        scratch_shapes=[pltpu.VMEM((nchunks, tn * PEER_TOK_STRIDE, LANES), F32),
                        pltpu.VMEM((tn, D_MODEL), F32)],
        compiler_params=_params(("parallel", "arbitrary")),
    )(e1, e2, cj, vt, x)


def _pair_major(lf, batch, length):
    return lf.reshape(batch, length, FOX_HEADS // 2, 2).transpose(0, 2, 3, 1)


def _layer(x, dims, s0_s, k_past, v_past, lf_past, g_norm1, w_in, w_gla_a2, b_gla_a, g_gla_onorm,
           g_fox_qnorm, g_fox_knorm, b_fox_f, w_out, g_norm2, w_peer_q, peer_subkeys, peer_u, peer_v):
    bp, tp, bs, ts = dims
    n_p = bp * tp
    past = k_past.shape[1]
    bounds = [0]
    for s in IN_SIZES:
        bounds.append(bounds[-1] + s)
    cols = [w_in[:, bounds[i]:bounds[i + 1]] for i in range(len(IN_SIZES))]
    gq, gk, gv, gg, glr, fq, fk, fv, ff = cols
    pad = jnp.zeros((D_MODEL, Z_SMALL - FOX_HEADS - GLA_LOWRANK), w_in.dtype)
    w_cat = jnp.concatenate([gq, gk, gv, gg, fq, fk, fv, ff, glr, pad], axis=1).astype(BF16)
    wa2p = jnp.zeros((Z_SMALL, 256), F32).at[FOX_HEADS:FOX_HEADS + GLA_LOWRANK].set(w_gla_a2).astype(BF16)
    bsm = jnp.zeros((1, Z_SMALL), F32).at[0, :FOX_HEADS].set(b_fox_f)
    head_of_col = jnp.arange(FOX_HEADS * FOX_HD) // FOX_HD
    ind = (head_of_col[:, None] == jnp.arange(LANES)[None, :]).astype(BF16)
    z, la, k32, v32, lf = _inproj(
        x, g_norm1[None], w_cat, wa2p, b_gla_a[None], bsm,
        jnp.tile(g_fox_qnorm, FOX_HEADS)[None], jnp.tile(g_fox_knorm, FOX_HEADS)[None], ind, ind.T)

    gon = g_gla_onorm[None]
    og_p, st_p = _gla(z, la, jnp.zeros((bp, GLA_HEADS, GLA_DK, GLA_DV), F32), gon, batch=bp, seq=tp, row0=0)
    og_s, st_s = _gla(z, la, s0_s, gon, batch=bs, seq=ts, row0=n_p)
    of_p = _fox(z, _pair_major(lf[:n_p], bp, tp), batch=bp, seq=tp, row0=0)
    of_s = _fox(z, _pair_major(lf[n_p:], bs, ts), batch=bs, seq=ts, row0=n_p,
                k_past=k_past.reshape(bs, past, FOX_HEADS * FOX_HD),
                v_past=v_past.reshape(bs, past, FOX_HEADS * FOX_HD),
                lf_past_t=lf_past.reshape(bs, past, FOX_HEADS // 2, 2).transpose(0, 2, 3, 1))
    og = jnp.concatenate([og_p, og_s], axis=0)
    of = jnp.concatenate([of_p, of_s], axis=0)
    x, h2 = _outproj(og, of, x, w_out.astype(BF16), g_norm2[None])

    sk = peer_subkeys.reshape(2 * PEER_HEADS, PEER_NKEYS, LANES).astype(BF16)
    e1, e2, g = _route(h2, w_peer_q.astype(BF16), sk)
    cj = _peer_act(h2, peer_u.T.astype(BF16), e1, e2, g)
    x = _peer_out(e1, e2, cj, peer_v.astype(BF16), x)

    def heads(a, b, t):
        return a.reshape(b, t, FOX_HEADS, FOX_HD)

    outs_p = (heads(k32[:n_p], bp, tp), heads(v32[:n_p], bp, tp), lf[:n_p].reshape(bp, tp, FOX_HEADS), st_p)
    outs_s = (heads(k32[n_p:], bs, ts), heads(v32[n_p:], bs, ts), lf[n_p:].reshape(bs, ts, FOX_HEADS), st_s)
    return x, outs_p, outs_s


def kernel(x_prompt, x_sample, cache_fox_k, cache_fox_v, cache_fox_logf, state_gla, g_norm1, w_in, w_gla_a2,
           b_gla_a, g_gla_onorm, g_fox_qnorm, g_fox_knorm, b_fox_f, w_out, g_norm2, w_peer_q, peer_subkeys,
           peer_u, peer_v):
    bp, tp, _ = x_prompt.shape
    bs, ts, _ = x_sample.shape
    n_p = bp * tp
    x = jnp.concatenate([x_prompt.reshape(n_p, D_MODEL), x_sample.reshape(bs * ts, D_MODEL)], axis=0)
    per_p, per_s = [], []
    for l in range(w_in.shape[0]):
        x, outs_p, outs_s = _layer(
            x, (bp, tp, bs, ts), state_gla[l], cache_fox_k[l], cache_fox_v[l], cache_fox_logf[l],
            g_norm1[l], w_in[l], w_gla_a2[l], b_gla_a[l], g_gla_onorm[l], g_fox_qnorm[l], g_fox_knorm[l],
            b_fox_f[l], w_out[l], g_norm2[l], w_peer_q[l], peer_subkeys[l], peer_u[l], peer_v[l])
        per_p.append(outs_p)
        per_s.append(outs_s)
    stack = lambda per, i: jnp.stack([o[i] for o in per])
    return (x[:n_p].reshape(bp, tp, D_MODEL), x[n_p:].reshape(bs, ts, D_MODEL),
            stack(per_p, 0), stack(per_p, 1), stack(per_p, 2), stack(per_p, 3),
            stack(per_s, 0), stack(per_s, 1), stack(per_s, 2), stack(per_s, 3))
```

```python
import functools

import jax
import jax.numpy as jnp
from jax import lax
from jax.experimental import pallas as pl
from jax.experimental.pallas import tpu as pltpu

F32 = jnp.float32
BF16 = jnp.bfloat16
I32 = jnp.int32

D_MODEL = 1024
EPS = 1e-6
GLA_CHUNK = 64
GLA_HEADS = 4
GLA_DK = 64
GLA_DV = 128
GLA_LOWRANK = 16
GLA_GATE_NORM = 16.0
GLA_SUB = 16
FOX_HEADS = 8
FOX_HD = 64
FOX_SCALE = FOX_HD ** -0.5
FOX_BLOCK = 512
IN_SIZES = (256, 256, 512, 512, GLA_LOWRANK, 512, 512, 512, FOX_HEADS)
PEER_HEADS = 8
PEER_NKEYS = 128
PEER_TOPK = 16
PEER_NEXP = PEER_NKEYS * PEER_NKEYS
PEER_J = PEER_HEADS * PEER_TOPK

LANES = 128
Z_MAIN = 3072
Z_SMALL = 128
VMEM_LIMIT = 56 * 1024 * 1024

ZC_GQ, ZC_GK, ZC_GV, ZC_GG, ZC_FQ, ZC_FK, ZC_FV = 0, 2, 4, 8, 12, 16, 20


def _params(sem):
    return pltpu.CompilerParams(dimension_semantics=sem, vmem_limit_bytes=VMEM_LIMIT)


def _tile(n, cap):
    t = cap
    while n % t:
        t //= 2
    return t


def _dot(a, b):
    return jnp.dot(a, b, preferred_element_type=F32)


def _dot_nt(a, b):
    return lax.dot_general(a, b, (((1,), (1,)), ((), ())), preferred_element_type=F32)


def _dot_tn(a, b):
    return lax.dot_general(a, b, (((0,), (0,)), ((), ())), preferred_element_type=F32)


def _split(x):
    hi = x.astype(BF16)
    lo = (x - hi.astype(F32)).astype(BF16)
    return hi, lo


def _dot_exact_rhs(x, m):
    hi, lo = _split(x)
    return _dot(hi, m) + _dot(lo, m)


def _dot_exact_lhs(m, x):
    hi, lo = _split(x)
    return _dot(m, hi) + _dot(m, lo)


def _log_sigmoid(y):
    return jnp.minimum(y, 0.0) - jnp.log(1.0 + jnp.exp(-jnp.abs(y)))


def _const_spec(shape):
    nd = len(shape)
    return pl.BlockSpec(shape, lambda *_: (0,) * nd)


def _inproj_kernel(*refs, n_alias):
    (x_ref, g1_ref, w_ref, wa2_ref, ba_ref, bsm_ref, gq_ref, gk_ref, ind_ref, indt_ref) = refs[:10]
    z_ref, la_ref, k32_ref, v32_ref, lf_ref = refs[10 + n_alias:]
    x = x_ref[...]
    ms = jnp.mean(x * x, axis=-1, keepdims=True)
    h = (x * lax.rsqrt(ms + EPS) * g1_ref[...]).astype(BF16)
    z = _dot(h, w_ref[...])

    def headnorm(t, gain):
        ss = _dot_exact_rhs(t * t, ind_ref[...])
        r = lax.rsqrt(ss * (1.0 / FOX_HD) + EPS)
        return t * _dot_exact_rhs(r, indt_ref[...]) * gain

    fq = headnorm(z[:, 1536:2048], gq_ref[...])
    fk = headnorm(z[:, 2048:2560], gk_ref[...])
    fv = z[:, 2560:3072]
    z_ref[:, 0:1536] = z[:, 0:1536].astype(BF16)
    z_ref[:, 1536:2048] = fq.astype(BF16)
    z_ref[:, 2048:2560] = fk.astype(BF16)
    z_ref[:, 2560:3072] = fv.astype(BF16)
    k32_ref[0] = fk
    v32_ref[0] = fv
    small = z[:, Z_MAIN:Z_MAIN + Z_SMALL]
    lf_ref[0] = _log_sigmoid(small + bsm_ref[...])[:, 0:FOX_HEADS]
    y = _dot(small.astype(BF16), wa2_ref[...]) + ba_ref[...]
    la_ref[...] = _log_sigmoid(y) * (1.0 / GLA_GATE_NORM)


def _inproj(x, consts, layer, depth, stacked):
    n = x.shape[0]
    tm = _tile(n, 512)
    row = lambda i: (i, 0)
    slab = lambda i: (layer, i, 0)
    n_alias = len(stacked)
    widths = (512, 512, FOX_HEADS)
    return pl.pallas_call(
        functools.partial(_inproj_kernel, n_alias=n_alias),
        grid=(n // tm,),
        in_specs=[pl.BlockSpec((tm, D_MODEL), row)] + [_const_spec(c.shape) for c in consts]
        + [pl.BlockSpec(memory_space=pl.ANY)] * n_alias,
        out_specs=[pl.BlockSpec((tm, Z_MAIN), row), pl.BlockSpec((tm, 256), row)]
        + [pl.BlockSpec((1, tm, w), slab) for w in widths],
        out_shape=[jax.ShapeDtypeStruct((n, Z_MAIN), BF16), jax.ShapeDtypeStruct((n, 256), F32)]
        + [jax.ShapeDtypeStruct((depth, n, w), F32) for w in widths],
        input_output_aliases={1 + len(consts) + i: 2 + i for i in range(n_alias)},
        compiler_params=_params(("parallel",)),
    )(x, *consts, *stacked)


def _gla_kernel(q_ref, k_ref, v_ref, gg_ref, la_ref, s0_ref, gon_ref, tri_ref, o_ref, sfin_ref, st_scr,
                *, chunk, nchunk):
    t = pl.program_id(1)
    nsub = chunk // GLA_SUB

    @pl.when(t == 0)
    def _():
        for p in range(2):
            st_scr[p] = s0_ref[0, p].T

    rowi = lax.broadcasted_iota(I32, (chunk, LANES), 0)
    lane = lax.broadcasted_iota(I32, (chunk, LANES), 1)
    lane_st = lax.broadcasted_iota(I32, (chunk, nsub * LANES), 1)
    lane_sq = lax.broadcasted_iota(I32, (LANES, LANES), 1)
    arow = lax.broadcasted_iota(I32, (chunk, chunk), 0)
    acol = lax.broadcasted_iota(I32, (chunk, chunk), 1)

    def body(c, carry):
        r0 = pl.multiple_of(c * chunk, chunk)
        rows = pl.ds(r0, chunk)
        bcum = _dot_exact_lhs(tri_ref[...], la_ref[rows, :])
        q = q_ref[rows, :].astype(F32) * (GLA_DK ** -0.5)
        k = k_ref[rows, :].astype(F32)
        for p in range(2):
            cols = slice(p * LANES, (p + 1) * LANES)
            bp, qp, kp = bcum[:, cols], q[:, cols], k[:, cols]
            blast = bp[chunk - 1:chunk, :]
            qs, ks = [], []
            for sb in range(nsub):
                beta = bp[sb * GLA_SUB - 1:sb * GLA_SUB, :] if sb else jnp.zeros((1, LANES), F32)
                inblk = (rowi >= sb * GLA_SUB) & (rowi < (sb + 1) * GLA_SUB)
                qs.append(jnp.where(inblk, qp * jnp.exp(jnp.where(inblk, bp - beta, 0.0)), 0.0))
                valid = rowi < (sb + 1) * GLA_SUB
                ks.append(jnp.where(valid, kp * jnp.exp(jnp.where(valid, beta - bp, 0.0)), 0.0))
            qst = jnp.concatenate(qs, axis=1)
            kst = jnp.concatenate(ks, axis=1).astype(BF16)
            qinter = qp * jnp.exp(bp)
            kdec = (kp * jnp.exp(blast - bp)).astype(BF16)
            st = st_scr[p]
            stb = st.astype(BF16)
            upd = []
            for i in range(2):
                head = 2 * p + i
                hcols = slice(head * GLA_DV, (head + 1) * GLA_DV)
                a = _dot_nt(jnp.where((lane_st & GLA_DK) == i * GLA_DK, qst, 0.0).astype(BF16), kst)
                a = jnp.where(acol <= arow, a, 0.0)
                vh = v_ref[rows, hcols]
                o = _dot(a.astype(BF16), vh)
                o = o + _dot_nt(jnp.where((lane & GLA_DK) == i * GLA_DK, qinter, 0.0).astype(BF16), stb)
                on = o * lax.rsqrt(jnp.mean(o * o, axis=-1, keepdims=True) + EPS) * gon_ref[...]
                gate = gg_ref[rows, hcols].astype(F32)
                o_ref[rows, hcols] = (on * gate * (1.0 / (1.0 + jnp.exp(-gate)))).astype(BF16)
                upd.append(_dot_tn(vh, kdec))
            st_scr[p] = st * jnp.exp(blast) + jnp.where(lane_sq < GLA_DK, upd[0], upd[1])
        return carry

    lax.fori_loop(0, nchunk, body, 0)

    @pl.when(t == pl.num_programs(1) - 1)
    def _():
        for p in range(2):
            sfin_ref[0, p] = st_scr[p].T


def _gla(z, la, s0, gon, *, batch, seq, row0):
    chunk = min(seq, GLA_CHUNK)
    tb = min(seq, 512)
    nt = seq // tb
    blk0 = row0 // tb
    tri = jnp.tril(jnp.ones((chunk, chunk), F32)).astype(BF16)
    rows = lambda col: (lambda b, t: (blk0 + b * nt + t, col))
    state_spec = pl.BlockSpec((1, 2, LANES, LANES), lambda b, t: (b, 0, 0, 0))
    o, sfin = pl.pallas_call(
        functools.partial(_gla_kernel, chunk=chunk, nchunk=tb // chunk),
        grid=(batch, nt),
        in_specs=[pl.BlockSpec((tb, 256), rows(0)), pl.BlockSpec((tb, 256), rows(1)),
                  pl.BlockSpec((tb, 512), rows(1)), pl.BlockSpec((tb, 512), rows(2)),
                  pl.BlockSpec((tb, 256), rows(0)), state_spec,
                  _const_spec((1, GLA_DV)), _const_spec((chunk, chunk))],
        out_specs=[pl.BlockSpec((tb, 512), lambda b, t: (b * nt + t, 0)), state_spec],
        out_shape=[jax.ShapeDtypeStruct((batch * seq, 512), BF16),
                   jax.ShapeDtypeStruct((batch, 2, LANES, LANES), F32)],
        scratch_shapes=[pltpu.VMEM((2, LANES, LANES), F32)],
        compiler_params=_params(("parallel", "arbitrary")),
    )(z, z, z, z, la, s0.reshape(batch, 2, LANES, LANES), gon, tri)
    return o, sfin.reshape(batch, GLA_HEADS, GLA_DK, GLA_DV)


def _fox_kernel(*refs, seq, past, tq, has_past):
    if has_past:
        (q_ref, kc_ref, vc_ref, lfc_ref, triu_ref, kp_ref, vp_ref, lfp_ref, triup_ref,
         o_ref, cc_scr, cp_scr) = refs
    else:
        q_ref, kc_ref, vc_ref, lfc_ref, triu_ref, o_ref, cc_scr = refs
    i = pl.program_id(2)
    nq = seq // tq
    pblk = FOX_BLOCK

    @pl.when(i == 0)
    def _():
        carry = jnp.zeros((2, 1), F32)
        if has_past:
            for jb in range(past // pblk):
                x = lfp_ref[0, 0, :, jb * pblk:(jb + 1) * pblk]
                cp_scr[:, jb * pblk:(jb + 1) * pblk] = _dot_exact_rhs(x, triup_ref[...]) + carry
                carry = carry + jnp.sum(x, axis=1, keepdims=True)
        for jb in range(nq):
            x = lfc_ref[0, 0, :, jb * tq:(jb + 1) * tq]
            cc_scr[:, jb * tq:(jb + 1) * tq] = _dot_exact_rhs(x, triu_ref[...]) + carry
            carry = carry + jnp.sum(x, axis=1, keepdims=True)

    q = q_ref[...]
    lane = lax.broadcasted_iota(I32, (tq, LANES), 1)
    qh = [jnp.where(lane < FOX_HD, q, jnp.zeros_like(q)), jnp.where(lane >= FOX_HD, q, jnp.zeros_like(q))]

    def kv_step(state, kb, vb, crows, mask):
        out = []
        for h in range(2):
            m, l, acc = state[h]
            s = _dot_nt(qh[h], kb) * FOX_SCALE - crows[h]
            if mask is not None:
                s = jnp.where(mask, s, -jnp.inf)
            m_new = jnp.maximum(m, jnp.max(s, axis=1, keepdims=True))
            pr = jnp.exp(s - m_new)
            alpha = jnp.exp(m - m_new)
            l = alpha * l + jnp.sum(pr, axis=1, keepdims=True)
            acc = alpha * acc + _dot(pr.astype(BF16), vb)
            out.append((m_new, l, acc))
        return tuple(out)

    init = tuple((jnp.full((tq, 1), -jnp.inf, F32), jnp.zeros((tq, 1), F32), jnp.zeros((tq, LANES), F32))
                 for _ in range(2))
    state = init
    if has_past:
        def past_body(j, st):
            off = pl.multiple_of(j * pblk, pblk)
            kb = kp_ref[0, pl.ds(off, pblk), :].astype(BF16)
            vb = vp_ref[0, pl.ds(off, pblk), :].astype(BF16)
            crows = [cp_scr[h:h + 1, pl.ds(off, pblk)] for h in range(2)]
            return kv_step(st, kb, vb, crows, None)
        state = lax.fori_loop(0, past // pblk, past_body, state)
    if nq > 1:
        def cur_body(j, st):
            off = pl.multiple_of(j * tq, tq)
            crows = [cc_scr[h:h + 1, pl.ds(off, tq)] for h in range(2)]
            return kv_step(st, kc_ref[pl.ds(off, tq), :], vc_ref[pl.ds(off, tq), :], crows, None)
        state = lax.fori_loop(0, i, cur_body, state)
        off = pl.multiple_of(i * tq, tq)
    else:
        off = 0
    causal = lax.broadcasted_iota(I32, (tq, tq), 1) <= lax.broadcasted_iota(I32, (tq, tq), 0)
    crows = [cc_scr[h:h + 1, pl.ds(off, tq)] for h in range(2)]
    state = kv_step(state, kc_ref[pl.ds(off, tq), :], vc_ref[pl.ds(off, tq), :], crows, causal)
    o0 = state[0][2] / state[0][1]
    o1 = state[1][2] / state[1][1]
    o_ref[...] = jnp.where(lane < FOX_HD, o0, o1).astype(BF16)


def _fox(z, lf_cur_t, *, batch, seq, row0, k_past=None, v_past=None, lf_past_t=None):
    has_past = k_past is not None
    past = k_past.shape[1] if has_past else 0
    tq = min(seq, FOX_BLOCK)
    nq = seq // tq
    qblk0 = row0 // tq
    sblk0 = row0 // seq
    triu = jnp.triu(jnp.ones((tq, tq), F32)).astype(BF16)
    in_specs = [pl.BlockSpec((tq, LANES), lambda b, p, i: (qblk0 + b * nq + i, ZC_FQ + p)),
                pl.BlockSpec((seq, LANES), lambda b, p, i: (sblk0 + b, ZC_FK + p)),
                pl.BlockSpec((seq, LANES), lambda b, p, i: (sblk0 + b, ZC_FV + p)),
                pl.BlockSpec((1, 1, 2, seq), lambda b, p, i: (b, p, 0, 0)),
                _const_spec((tq, tq))]
    args = [z, z, z, lf_cur_t, triu]
    scratch = [pltpu.VMEM((2, seq), F32)]
    if has_past:
        triup = jnp.triu(jnp.ones((FOX_BLOCK, FOX_BLOCK), F32)).astype(BF16)
        in_specs += [pl.BlockSpec((1, past, LANES), lambda b, p, i: (b, 0, p)),
                     pl.BlockSpec((1, past, LANES), lambda b, p, i: (b, 0, p)),
                     pl.BlockSpec((1, 1, 2, past), lambda b, p, i: (b, p, 0, 0)),
                     _const_spec((FOX_BLOCK, FOX_BLOCK))]
        args += [k_past, v_past, lf_past_t, triup]
        scratch.append(pltpu.VMEM((2, past), F32))
    return pl.pallas_call(
        functools.partial(_fox_kernel, seq=seq, past=past, tq=tq, has_past=has_past),
        grid=(batch, FOX_HEADS // 2, nq),
        in_specs=in_specs,
        out_specs=pl.BlockSpec((tq, LANES), lambda b, p, i: (b * nq + i, p)),
        out_shape=jax.ShapeDtypeStruct((batch * seq, 512), BF16),
        scratch_shapes=scratch,
        compiler_params=_params(("parallel", "parallel", "arbitrary")),
    )(*args)


def _outproj_kernel(og_ref, of_ref, x_ref, w_ref, g2_ref, xo_ref, h2_ref):
    mixed = _dot(og_ref[...], w_ref[0:512, :]) + _dot(of_ref[...], w_ref[512:1024, :])
    x = x_ref[...] + mixed
    xo_ref[...] = x
    ms = jnp.mean(x * x, axis=-1, keepdims=True)
    h2_ref[...] = (x * lax.rsqrt(ms + EPS) * g2_ref[...]).astype(BF16)


def _outproj(og, of, x, w, g2):
    n = x.shape[0]
    tm = _tile(n, 512)
    row = lambda i: (i, 0)
    return pl.pallas_call(
        _outproj_kernel,
        grid=(n // tm,),
        in_specs=[pl.BlockSpec((tm, 512), row), pl.BlockSpec((tm, 512), row),
                  pl.BlockSpec((tm, D_MODEL), row), _const_spec(w.shape), _const_spec(g2.shape)],
        out_specs=[pl.BlockSpec((tm, D_MODEL), row), pl.BlockSpec((tm, D_MODEL), row)],
        out_shape=[jax.ShapeDtypeStruct((n, D_MODEL), F32), jax.ShapeDtypeStruct((n, D_MODEL), BF16)],
        compiler_params=_params(("parallel",)),
    )(og, of, x, w, g2)


def _topk_rows(s, k):
    nrows = s.shape[0]
    iota = lax.broadcasted_iota(I32, s.shape, 0)
    vals, idxs = [], []
    for _ in range(k):
        m = jnp.max(s, axis=0, keepdims=True)
        idx = jnp.min(jnp.where(s == m, iota, nrows), axis=0, keepdims=True)
        vals.append(m)
        idxs.append(idx)
        s = jnp.where(iota == idx, -jnp.inf, s)
    return jnp.concatenate(vals, axis=0), jnp.concatenate(idxs, axis=0)


def _route_kernel(h_ref, wq_ref, sk_ref, e1_ref, e2_ref, g_ref, q_scr, e1_scr, e2_scr, g_scr):
    tn = h_ref.shape[0]
    q = _dot(h_ref[...], wq_ref[...])
    for hc in range(2 * PEER_HEADS):
        q_scr[hc] = q[:, hc * LANES:(hc + 1) * LANES].astype(BF16)
    iota_k = lax.broadcasted_iota(I32, (PEER_TOPK, tn), 0)
    iota_8 = lax.broadcasted_iota(I32, (8, tn), 0)

    def pick(table, sel):
        rows = []
        for r in range(PEER_TOPK):
            rows.append(jnp.sum(jnp.where(iota_k == sel[r:r + 1, :], table, 0), axis=0, keepdims=True))
        return jnp.concatenate(rows, axis=0)

    def head(h, carry):
        s1 = _dot_nt(sk_ref[2 * h], q_scr[2 * h])
        s2 = _dot_nt(sk_ref[2 * h + 1], q_scr[2 * h + 1])
        v1, i1 = _topk_rows(s1, PEER_TOPK)
        v2, i2 = _topk_rows(s2, PEER_TOPK)
        blocks = [v1[0:1, :] + v2]
        for a in range(1, 8):
            blocks.append(jnp.where(iota_8 < PEER_TOPK // (a + 1), v1[a:a + 1, :] + v2[0:8, :], -jnp.inf))
        blocks.append(v1[8:16, :] + v2[0:1, :])
        top, ti = _topk_rows(jnp.concatenate(blocks, axis=0), PEER_TOPK)
        mid = ti - PEER_TOPK
        e1 = pick(i1, jnp.where(ti < 16, 0, jnp.where(ti < 72, (mid >> 3) + 1, ti - 64)))
        e2 = pick(i2, jnp.where(ti < 16, ti, jnp.where(ti < 72, mid & 7, 0)))
        ex = jnp.exp(top - jnp.max(top, axis=0, keepdims=True))
        g = ex / jnp.sum(ex, axis=0, keepdims=True)
        rows = pl.ds(pl.multiple_of(h * PEER_TOPK, PEER_TOPK), PEER_TOPK)
        e1_scr[rows, :] = e1
        e2_scr[rows, :] = e2
        g_scr[rows, :] = g
        return carry

    lax.fori_loop(0, PEER_HEADS, head, 0)
    e1_ref[...] = e1_scr[...].T
    e2_ref[...] = e2_scr[...].T
    g_ref[...] = g_scr[...].T


def _route(h2, wq, sk):
    n = h2.shape[0]
    tn = _tile(n, 256)
    row = lambda i: (i, 0)
    return pl.pallas_call(
        _route_kernel,
        grid=(n // tn,),
        in_specs=[pl.BlockSpec((tn, D_MODEL), row), _const_spec(wq.shape), _const_spec(sk.shape)],
        out_specs=[pl.BlockSpec((tn, PEER_J), row)] * 3,
        out_shape=[jax.ShapeDtypeStruct((n, PEER_J), I32), jax.ShapeDtypeStruct((n, PEER_J), I32),
                   jax.ShapeDtypeStruct((n, PEER_J), F32)],
        scratch_shapes=[pltpu.VMEM((2 * PEER_HEADS, tn, LANES), BF16), pltpu.VMEM((PEER_J, tn), I32),
                        pltpu.VMEM((PEER_J, tn), I32), pltpu.VMEM((PEER_J, tn), F32)],
        compiler_params=_params(("parallel",)),
    )(h2, wq, sk)


PEER_ROWS = 16
PEER_EC = PEER_ROWS * PEER_NKEYS


def _gelu(a):
    return 0.5 * a * (1.0 + lax.erf(a * (2.0 ** -0.5)))


def _peer_act_kernel(h_ref, ut_ref, e1_ref, e2_ref, g_ref, c_ref, acc_scr):
    c = pl.program_id(1)

    @pl.when(c == 0)
    def _():
        acc_scr[...] = jnp.zeros_like(acc_scr)

    a_all = _dot(h_ref[...], ut_ref[...])
    e1 = e1_ref[...]
    e2 = e2_ref[...]
    acc = acc_scr[...]
    for r in range(PEER_ROWS):
        picked = jnp.take_along_axis(a_all[:, r * LANES:(r + 1) * LANES], e2, axis=1)
        acc = acc + jnp.where(e1 == c * PEER_ROWS + r, picked, 0.0)
    acc_scr[...] = acc

    @pl.when(c == pl.num_programs(1) - 1)
    def _():
        c_ref[...] = g_ref[...] * _gelu(acc)


def _peer_act(h2, ut, e1, e2, g):
    n = h2.shape[0]
    tn = _tile(n, 512)
    row = lambda i, c: (i, 0)
    return pl.pallas_call(
        _peer_act_kernel,
        grid=(n // tn, PEER_NEXP // PEER_EC),
        in_specs=[pl.BlockSpec((tn, D_MODEL), row), pl.BlockSpec((D_MODEL, PEER_EC), lambda i, c: (0, c)),
                  pl.BlockSpec((tn, PEER_J), row), pl.BlockSpec((tn, PEER_J), row),
                  pl.BlockSpec((tn, PEER_J), row)],
        out_specs=pl.BlockSpec((tn, PEER_J), row),
        out_shape=jax.ShapeDtypeStruct((n, PEER_J), F32),
        scratch_shapes=[pltpu.VMEM((tn, PEER_J), F32)],
        compiler_params=_params(("parallel", "arbitrary")),
    )(h2, ut, e1, e2, g)


PEER_HALF = PEER_ROWS // 2


def _peer_out_kernel(e1_ref, e2_ref, c_ref, v_ref, x_ref, o_ref, y_scr, acc_scr):
    tn = x_ref.shape[0]
    c = pl.program_id(1)
    nchunks = PEER_NEXP // PEER_EC

    @pl.when(c == 0)
    def _():
        acc_scr[...] = jnp.zeros_like(acc_scr)
        key = lax.broadcasted_iota(I32, (PEER_NKEYS, PEER_J), 0)

        def token(t, carry):
            row = pl.ds(t, 1)
            d = jnp.where(key == e1_ref[row, :], c_ref[row, :], 0.0).astype(BF16)
            w = jnp.where(key == e2_ref[row, :], 1.0, 0.0).astype(BF16)
            y = _dot_nt(d, w).astype(BF16).astype(F32)
            bits = lax.bitcast_convert_type(y, I32)
            base = pl.multiple_of(t * PEER_HALF, PEER_HALF)
            for cc in range(nchunks):
                hi = bits[cc * PEER_ROWS:cc * PEER_ROWS + PEER_HALF, :]
                lo = bits[cc * PEER_ROWS + PEER_HALF:(cc + 1) * PEER_ROWS, :]
                y_scr[cc, pl.ds(base, PEER_HALF), :] = hi | lax.shift_right_logical(lo, 16)
            return carry

        lax.fori_loop(0, tn, token, 0, unroll=32)

    his, los = [], []
    for i in range(PEER_HALF):
        words = y_scr[c, pl.ds(i, tn, stride=PEER_HALF), :]
        his.append(lax.bitcast_convert_type(words & -65536, F32).astype(BF16))
        los.append(lax.bitcast_convert_type(words << 16, F32).astype(BF16))
    acc_scr[...] += _dot(jnp.concatenate(his + los, axis=1), v_ref[...])

    @pl.when(c == nchunks - 1)
    def _():
        o_ref[...] = x_ref[...] + acc_scr[...]


def _peer_out(e1, e2, cj, vt, x):
    n = x.shape[0]
    tn = _tile(n, 512)
    row = lambda i, c: (i, 0)
    nchunks = PEER_NEXP // PEER_EC
    return pl.pallas_call(
        _peer_out_kernel,
        grid=(n // tn, nchunks),
        in_specs=[pl.BlockSpec((tn, PEER_J), row), pl.BlockSpec((tn, PEER_J), row),
                  pl.BlockSpec((tn, PEER_J), row), pl.BlockSpec((PEER_EC, D_MODEL), lambda i, c: (c, 0)),
                  pl.BlockSpec((tn, D_MODEL), row)],
        out_specs=pl.BlockSpec((tn, D_MODEL), row),
        out_shape=jax.ShapeDtypeStruct((n, D_MODEL), F32),
        scratch_shapes=[pltpu.VMEM((nchunks, tn * PEER_HALF, LANES), I32),
                        pltpu.VMEM((tn, D_MODEL), F32)],
        compiler_params=_params(("parallel", "arbitrary")),
    )(e1, e2, cj, vt, x)


def _pair_major(lf, batch, length):
    return lf.reshape(batch, length, FOX_HEADS // 2, 2).transpose(0, 2, 3, 1)


def _prep_weights(g_norm1, w_in, w_gla_a2, b_gla_a, g_gla_onorm, g_fox_qnorm, g_fox_knorm, b_fox_f, w_out,
                  g_norm2, w_peer_q, peer_subkeys, peer_u, peer_v):
    bounds = [0]
    for size in IN_SIZES:
        bounds.append(bounds[-1] + size)
    gq, gk, gv, gg, glr, fq, fk, fv, ff = [w_in[:, bounds[i]:bounds[i + 1]] for i in range(len(IN_SIZES))]
    pad = jnp.zeros((D_MODEL, Z_SMALL - FOX_HEADS - GLA_LOWRANK), w_in.dtype)
    w_cat = jnp.concatenate([gq, gk, gv, gg, fq, fk, fv, ff, glr, pad], axis=1).astype(BF16)
    wa2p = jnp.zeros((Z_SMALL, 256), F32).at[FOX_HEADS:FOX_HEADS + GLA_LOWRANK].set(w_gla_a2).astype(BF16)
    bsm = jnp.zeros((1, Z_SMALL), F32).at[0, :FOX_HEADS].set(b_fox_f)
    head_of_col = jnp.arange(FOX_HEADS * FOX_HD) // FOX_HD
    ind = (head_of_col[:, None] == jnp.arange(LANES)[None, :]).astype(BF16)
    return dict(
        inproj=(g_norm1[None], w_cat, wa2p, b_gla_a[None], bsm, jnp.tile(g_fox_qnorm, FOX_HEADS)[None],
                jnp.tile(g_fox_knorm, FOX_HEADS)[None], ind, ind.T),
        gon=g_gla_onorm[None], w_out=w_out.astype(BF16), g2=g_norm2[None], wq=w_peer_q.astype(BF16),
        sk=peer_subkeys.reshape(2 * PEER_HEADS, PEER_NKEYS, LANES).astype(BF16),
        ut=peer_u.T.astype(BF16), v=peer_v.astype(BF16))


def _path_layer(x, wts, layer, depth, stacked, batch, seq, s0, k_past=None, v_past=None, lf_past=None):
    z, la, k_st, v_st, lf_st = _inproj(x, wts["inproj"], layer, depth, stacked)
    og, state = _gla(z, la, s0, wts["gon"], batch=batch, seq=seq, row0=0)
    past_args = {}
    if k_past is not None:
        past = k_past.shape[1]
        past_args = dict(k_past=k_past.reshape(batch, past, FOX_HEADS * FOX_HD),
                         v_past=v_past.reshape(batch, past, FOX_HEADS * FOX_HD),
                         lf_past_t=lf_past.reshape(batch, past, FOX_HEADS // 2, 2).transpose(0, 2, 3, 1))
    of = _fox(z, _pair_major(lf_st[layer], batch, seq), batch=batch, seq=seq, row0=0, **past_args)
    x, h2 = _outproj(og, of, x, wts["w_out"], wts["g2"])
    e1, e2, g = _route(h2, wts["wq"], wts["sk"])
    cj = _peer_act(h2, wts["ut"], e1, e2, g)
    x = _peer_out(e1, e2, cj, wts["v"], x)
    return x, (k_st, v_st, lf_st), state


def kernel(x_prompt, x_sample, cache_fox_k, cache_fox_v, cache_fox_logf, state_gla, g_norm1, w_in, w_gla_a2,
           b_gla_a, g_gla_onorm, g_fox_qnorm, g_fox_knorm, b_fox_f, w_out, g_norm2, w_peer_q, peer_subkeys,
           peer_u, peer_v):
    bp, tp, _ = x_prompt.shape
    bs, ts, _ = x_sample.shape
    depth = w_in.shape[0]
    xp = x_prompt.reshape(bp * tp, D_MODEL)
    xs = x_sample.reshape(bs * ts, D_MODEL)
    stk_p, stk_s, st_p, st_s = (), (), [], []
    zero_state = jnp.zeros((bp, GLA_HEADS, GLA_DK, GLA_DV), F32)
    for l in range(depth):
        wts = _prep_weights(g_norm1[l], w_in[l], w_gla_a2[l], b_gla_a[l], g_gla_onorm[l], g_fox_qnorm[l],
                            g_fox_knorm[l], b_fox_f[l], w_out[l], g_norm2[l], w_peer_q[l], peer_subkeys[l],
                            peer_u[l], peer_v[l])
        xp, stk_p, state = _path_layer(xp, wts, l, depth, stk_p, bp, tp, zero_state)
        st_p.append(state)
        xs, stk_s, state = _path_layer(xs, wts, l, depth, stk_s, bs, ts, state_gla[l],
                                       cache_fox_k[l], cache_fox_v[l], cache_fox_logf[l])
        st_s.append(state)

    def unstack(stk, b, t):
        k, v, lf = stk
        return (k.reshape(depth, b, t, FOX_HEADS, FOX_HD), v.reshape(depth, b, t, FOX_HEADS, FOX_HD),
                lf.reshape(depth, b, t, FOX_HEADS))

    return (xp.reshape(bp, tp, D_MODEL), xs.reshape(bs, ts, D_MODEL),
            *unstack(stk_p, bp, tp), jnp.stack(st_p), *unstack(stk_s, bs, ts), jnp.stack(st_s))
```

```python
import functools

import jax
import jax.numpy as jnp
from jax import lax
from jax.experimental import pallas as pl
from jax.experimental.pallas import tpu as pltpu

F32 = jnp.float32
BF16 = jnp.bfloat16
I32 = jnp.int32

D_MODEL = 1024
EPS = 1e-6
GLA_CHUNK = 64
GLA_HEADS = 4
GLA_DK = 64
GLA_DV = 128
GLA_LOWRANK = 16
GLA_GATE_NORM = 16.0
GLA_SUB = 16
FOX_HEADS = 8
FOX_HD = 64
FOX_SCALE = FOX_HD ** -0.5
LOG2E = 1.4426950408889634
FOX_BLOCK = 512
IN_SIZES = (256, 256, 512, 512, GLA_LOWRANK, 512, 512, 512, FOX_HEADS)
PEER_HEADS = 8
PEER_NKEYS = 128
PEER_TOPK = 16
PEER_NEXP = PEER_NKEYS * PEER_NKEYS
PEER_J = PEER_HEADS * PEER_TOPK

LANES = 128
Z_MAIN = 3072
Z_SMALL = 128
VMEM_LIMIT = 56 * 1024 * 1024

ZC_GQ, ZC_GK, ZC_GV, ZC_GG, ZC_FQ, ZC_FK, ZC_FV = 0, 2, 4, 8, 12, 16, 20


def _params(sem):
    return pltpu.CompilerParams(dimension_semantics=sem, vmem_limit_bytes=VMEM_LIMIT)


def _tile(n, cap):
    t = cap
    while n % t:
        t //= 2
    return t


def _dot(a, b):
    return jnp.dot(a, b, preferred_element_type=F32)


def _dot_nt(a, b):
    return lax.dot_general(a, b, (((1,), (1,)), ((), ())), preferred_element_type=F32)


def _dot_tn(a, b):
    return lax.dot_general(a, b, (((0,), (0,)), ((), ())), preferred_element_type=F32)


def _split(x):
    hi = x.astype(BF16)
    lo = (x - hi.astype(F32)).astype(BF16)
    return hi, lo


def _dot_exact_rhs(x, m):
    hi, lo = _split(x)
    return _dot(hi, m) + _dot(lo, m)


def _dot_exact_lhs(m, x):
    hi, lo = _split(x)
    return _dot(m, hi) + _dot(m, lo)


def _log_sigmoid(y):
    return jnp.minimum(y, 0.0) - jnp.log(1.0 + jnp.exp(-jnp.abs(y)))


def _const_spec(shape):
    nd = len(shape)
    return pl.BlockSpec(shape, lambda *_: (0,) * nd)


def _inproj_kernel(*refs, n_alias):
    (x_ref, g1_ref, w_ref, wa2_ref, ba_ref, bsm_ref, gq_ref, gk_ref, ind_ref, indt_ref) = refs[:10]
    z_ref, la_ref, k32_ref, v32_ref, lf_ref = refs[10 + n_alias:]
    x = x_ref[...]
    ms = jnp.mean(x * x, axis=-1, keepdims=True)
    h = (x * lax.rsqrt(ms + EPS) * g1_ref[...]).astype(BF16)
    z = _dot(h, w_ref[...])

    def headnorm(t, gain):
        ss = _dot_exact_rhs(t * t, ind_ref[...])
        r = lax.rsqrt(ss * (1.0 / FOX_HD) + EPS)
        return t * _dot_exact_rhs(r, indt_ref[...]) * gain

    fq = headnorm(z[:, 1536:2048], gq_ref[...])
    fk = headnorm(z[:, 2048:2560], gk_ref[...])
    fv = z[:, 2560:3072]
    z_ref[:, 0:1536] = z[:, 0:1536].astype(BF16)
    z_ref[:, 1536:2048] = fq.astype(BF16)
    z_ref[:, 2048:2560] = fk.astype(BF16)
    z_ref[:, 2560:3072] = fv.astype(BF16)
    k32_ref[0] = fk
    v32_ref[0] = fv
    small = z[:, Z_MAIN:Z_MAIN + Z_SMALL]
    lf_ref[0] = _log_sigmoid(small + bsm_ref[...])[:, 0:FOX_HEADS]
    y = _dot(small.astype(BF16), wa2_ref[...]) + ba_ref[...]
    la_ref[...] = _log_sigmoid(y) * (1.0 / GLA_GATE_NORM)


def _inproj(x, consts, layer, depth, stacked):
    n = x.shape[0]
    tm = _tile(n, 512)
    row = lambda i: (i, 0)
    slab = lambda i: (layer, i, 0)
    n_alias = len(stacked)
    widths = (512, 512, FOX_HEADS)
    return pl.pallas_call(
        functools.partial(_inproj_kernel, n_alias=n_alias),
        grid=(n // tm,),
        in_specs=[pl.BlockSpec((tm, D_MODEL), row)] + [_const_spec(c.shape) for c in consts]
        + [pl.BlockSpec(memory_space=pl.ANY)] * n_alias,
        out_specs=[pl.BlockSpec((tm, Z_MAIN), row), pl.BlockSpec((tm, 256), row)]
        + [pl.BlockSpec((1, tm, w), slab) for w in widths],
        out_shape=[jax.ShapeDtypeStruct((n, Z_MAIN), BF16), jax.ShapeDtypeStruct((n, 256), F32)]
        + [jax.ShapeDtypeStruct((depth, n, w), F32) for w in widths],
        input_output_aliases={1 + len(consts) + i: 2 + i for i in range(n_alias)},
        compiler_params=_params(("parallel",)),
    )(x, *consts, *stacked)


def _gla_kernel(q_ref, k_ref, v_ref, gg_ref, la_ref, s0_ref, gon_ref, tri_ref, o_ref, sfin_ref, st_scr,
                *, chunk, nchunk):
    t = pl.program_id(1)
    nsub = chunk // GLA_SUB

    @pl.when(t == 0)
    def _():
        for p in range(2):
            st_scr[p] = s0_ref[0, p].T

    rowi = lax.broadcasted_iota(I32, (chunk, LANES), 0)
    lane = lax.broadcasted_iota(I32, (chunk, LANES), 1)
    lane_st = lax.broadcasted_iota(I32, (chunk, nsub * LANES), 1)
    lane_sq = lax.broadcasted_iota(I32, (LANES, LANES), 1)
    arow = lax.broadcasted_iota(I32, (chunk, chunk), 0)
    acol = lax.broadcasted_iota(I32, (chunk, chunk), 1)

    def body(c, carry):
        r0 = pl.multiple_of(c * chunk, chunk)
        rows = pl.ds(r0, chunk)
        bcum = _dot_exact_lhs(tri_ref[...], la_ref[rows, :])
        q = q_ref[rows, :].astype(F32) * (GLA_DK ** -0.5)
        k = k_ref[rows, :].astype(F32)
        for p in range(2):
            cols = slice(p * LANES, (p + 1) * LANES)
            bp, qp, kp = bcum[:, cols], q[:, cols], k[:, cols]
            blast = bp[chunk - 1:chunk, :]
            qs, ks = [], []
            for sb in range(nsub):
                beta = bp[sb * GLA_SUB - 1:sb * GLA_SUB, :] if sb else jnp.zeros((1, LANES), F32)
                inblk = (rowi >= sb * GLA_SUB) & (rowi < (sb + 1) * GLA_SUB)
                qs.append(jnp.where(inblk, qp * jnp.exp(jnp.where(inblk, bp - beta, 0.0)), 0.0))
                valid = rowi < (sb + 1) * GLA_SUB
                ks.append(jnp.where(valid, kp * jnp.exp(jnp.where(valid, beta - bp, 0.0)), 0.0))
            qst = jnp.concatenate(qs, axis=1)
            kst = jnp.concatenate(ks, axis=1).astype(BF16)
            qinter = qp * jnp.exp(bp)
            kdec = (kp * jnp.exp(blast - bp)).astype(BF16)
            st = st_scr[p]
            stb = st.astype(BF16)
            upd = []
            for i in range(2):
                head = 2 * p + i
                hcols = slice(head * GLA_DV, (head + 1) * GLA_DV)
                a = _dot_nt(jnp.where((lane_st & GLA_DK) == i * GLA_DK, qst, 0.0).astype(BF16), kst)
                a = jnp.where(acol <= arow, a, 0.0)
                vh = v_ref[rows, hcols]
                o = _dot(a.astype(BF16), vh)
                o = o + _dot_nt(jnp.where((lane & GLA_DK) == i * GLA_DK, qinter, 0.0).astype(BF16), stb)
                on = o * lax.rsqrt(jnp.mean(o * o, axis=-1, keepdims=True) + EPS) * gon_ref[...]
                gate = gg_ref[rows, hcols].astype(F32)
                o_ref[rows, hcols] = (on * gate * (1.0 / (1.0 + jnp.exp(-gate)))).astype(BF16)
                upd.append(_dot_tn(vh, kdec))
            st_scr[p] = st * jnp.exp(blast) + jnp.where(lane_sq < GLA_DK, upd[0], upd[1])
        return carry

    lax.fori_loop(0, nchunk, body, 0)

    @pl.when(t == pl.num_programs(1) - 1)
    def _():
        for p in range(2):
            sfin_ref[0, p] = st_scr[p].T


def _gla(z, la, s0, gon, *, batch, seq, row0):
    chunk = min(seq, GLA_CHUNK)
    tb = min(seq, 512)
    nt = seq // tb
    blk0 = row0 // tb
    tri = jnp.tril(jnp.ones((chunk, chunk), F32)).astype(BF16)
    rows = lambda col: (lambda b, t: (blk0 + b * nt + t, col))
    state_spec = pl.BlockSpec((1, 2, LANES, LANES), lambda b, t: (b, 0, 0, 0))
    o, sfin = pl.pallas_call(
        functools.partial(_gla_kernel, chunk=chunk, nchunk=tb // chunk),
        grid=(batch, nt),
        in_specs=[pl.BlockSpec((tb, 256), rows(0)), pl.BlockSpec((tb, 256), rows(1)),
                  pl.BlockSpec((tb, 512), rows(1)), pl.BlockSpec((tb, 512), rows(2)),
                  pl.BlockSpec((tb, 256), rows(0)), state_spec,
                  _const_spec((1, GLA_DV)), _const_spec((chunk, chunk))],
        out_specs=[pl.BlockSpec((tb, 512), lambda b, t: (b * nt + t, 0)), state_spec],
        out_shape=[jax.ShapeDtypeStruct((batch * seq, 512), BF16),
                   jax.ShapeDtypeStruct((batch, 2, LANES, LANES), F32)],
        scratch_shapes=[pltpu.VMEM((2, LANES, LANES), F32)],
        compiler_params=_params(("parallel", "arbitrary")),
    )(z, z, z, z, la, s0.reshape(batch, 2, LANES, LANES), gon, tri)
    return o, sfin.reshape(batch, GLA_HEADS, GLA_DK, GLA_DV)


def _fox_kernel(*refs, seq, past, tq, has_past):
    if has_past:
        (q_ref, kc_ref, vc_ref, lfc_ref, triu_ref, kp_ref, vp_ref, lfp_ref, triup_ref,
         o_ref, cc_scr, cp_scr) = refs
    else:
        q_ref, kc_ref, vc_ref, lfc_ref, triu_ref, o_ref, cc_scr = refs
    i = pl.program_id(2)
    nq = seq // tq
    pblk = FOX_BLOCK

    @pl.when(i == 0)
    def _():
        carry = jnp.zeros((2, 1), F32)
        if has_past:
            for jb in range(past // pblk):
                x = lfp_ref[0, 0, :, jb * pblk:(jb + 1) * pblk]
                cp_scr[:, jb * pblk:(jb + 1) * pblk] = _dot_exact_rhs(x, triup_ref[...]) + carry
                carry = carry + jnp.sum(x, axis=1, keepdims=True)
        for jb in range(nq):
            x = lfc_ref[0, 0, :, jb * tq:(jb + 1) * tq]
            cc_scr[:, jb * tq:(jb + 1) * tq] = _dot_exact_rhs(x, triu_ref[...]) + carry
            carry = carry + jnp.sum(x, axis=1, keepdims=True)

    q = q_ref[...]
    lane = lax.broadcasted_iota(I32, (tq, LANES), 1)
    qh = [jnp.where(lane < FOX_HD, q, jnp.zeros_like(q)), jnp.where(lane >= FOX_HD, q, jnp.zeros_like(q))]

    def kv_step(state, kb, vb, crows, mask):
        out = []
        for h in range(2):
            m, l, acc = state[h]
            s = _dot_nt(qh[h], kb) * (FOX_SCALE * LOG2E) - crows[h] * LOG2E
            if mask is not None:
                s = jnp.where(mask, s, -jnp.inf)
            m_new = jnp.maximum(m, jnp.max(s, axis=1, keepdims=True))
            pr = jnp.exp2(s - m_new)
            alpha = jnp.exp2(m - m_new)
            l = alpha * l + jnp.sum(pr, axis=1, keepdims=True)
            acc = alpha * acc + _dot(pr.astype(BF16), vb)
            out.append((m_new, l, acc))
        return tuple(out)

    init = tuple((jnp.full((tq, 1), -jnp.inf, F32), jnp.zeros((tq, 1), F32), jnp.zeros((tq, LANES), F32))
                 for _ in range(2))
    state = init
    if has_past:
        def past_body(j, st):
            off = pl.multiple_of(j * pblk, pblk)
            kb = kp_ref[0, pl.ds(off, pblk), :].astype(BF16)
            vb = vp_ref[0, pl.ds(off, pblk), :].astype(BF16)
            crows = [cp_scr[h:h + 1, pl.ds(off, pblk)] for h in range(2)]
            return kv_step(st, kb, vb, crows, None)
        state = lax.fori_loop(0, past // pblk, past_body, state)
    if nq > 1:
        def cur_body(j, st):
            off = pl.multiple_of(j * tq, tq)
            crows = [cc_scr[h:h + 1, pl.ds(off, tq)] for h in range(2)]
            return kv_step(st, kc_ref[pl.ds(off, tq), :], vc_ref[pl.ds(off, tq), :], crows, None)
        state = lax.fori_loop(0, i, cur_body, state)
        off = pl.multiple_of(i * tq, tq)
    else:
        off = 0
    causal = lax.broadcasted_iota(I32, (tq, tq), 1) <= lax.broadcasted_iota(I32, (tq, tq), 0)
    crows = [cc_scr[h:h + 1, pl.ds(off, tq)] for h in range(2)]
    state = kv_step(state, kc_ref[pl.ds(off, tq), :], vc_ref[pl.ds(off, tq), :], crows, causal)
    o0 = state[0][2] / state[0][1]
    o1 = state[1][2] / state[1][1]
    o_ref[...] = jnp.where(lane < FOX_HD, o0, o1).astype(BF16)


def _fox(z, lf_cur_t, *, batch, seq, row0, k_past=None, v_past=None, lf_past_t=None):
    has_past = k_past is not None
    past = k_past.shape[1] if has_past else 0
    tq = min(seq, FOX_BLOCK)
    nq = seq // tq
    qblk0 = row0 // tq
    sblk0 = row0 // seq
    triu = jnp.triu(jnp.ones((tq, tq), F32)).astype(BF16)
    in_specs = [pl.BlockSpec((tq, LANES), lambda b, p, i: (qblk0 + b * nq + i, ZC_FQ + p)),
                pl.BlockSpec((seq, LANES), lambda b, p, i: (sblk0 + b, ZC_FK + p)),
                pl.BlockSpec((seq, LANES), lambda b, p, i: (sblk0 + b, ZC_FV + p)),
                pl.BlockSpec((1, 1, 2, seq), lambda b, p, i: (b, p, 0, 0)),
                _const_spec((tq, tq))]
    args = [z, z, z, lf_cur_t, triu]
    scratch = [pltpu.VMEM((2, seq), F32)]
    if has_past:
        triup = jnp.triu(jnp.ones((FOX_BLOCK, FOX_BLOCK), F32)).astype(BF16)
        in_specs += [pl.BlockSpec((1, past, LANES), lambda b, p, i: (b, 0, p)),
                     pl.BlockSpec((1, past, LANES), lambda b, p, i: (b, 0, p)),
                     pl.BlockSpec((1, 1, 2, past), lambda b, p, i: (b, p, 0, 0)),
                     _const_spec((FOX_BLOCK, FOX_BLOCK))]
        args += [k_past, v_past, lf_past_t, triup]
        scratch.append(pltpu.VMEM((2, past), F32))
    return pl.pallas_call(
        functools.partial(_fox_kernel, seq=seq, past=past, tq=tq, has_past=has_past),
        grid=(batch, FOX_HEADS // 2, nq),
        in_specs=in_specs,
        out_specs=pl.BlockSpec((tq, LANES), lambda b, p, i: (b * nq + i, p)),
        out_shape=jax.ShapeDtypeStruct((batch * seq, 512), BF16),
        scratch_shapes=scratch,
        compiler_params=_params(("parallel", "parallel", "arbitrary")),
    )(*args)


def _outproj_kernel(og_ref, of_ref, x_ref, w_ref, g2_ref, xo_ref, h2_ref):
    mixed = _dot(og_ref[...], w_ref[0:512, :]) + _dot(of_ref[...], w_ref[512:1024, :])
    x = x_ref[...] + mixed
    xo_ref[...] = x
    ms = jnp.mean(x * x, axis=-1, keepdims=True)
    h2_ref[...] = (x * lax.rsqrt(ms + EPS) * g2_ref[...]).astype(BF16)


def _outproj(og, of, x, w, g2):
    n = x.shape[0]
    tm = _tile(n, 512)
    row = lambda i: (i, 0)
    return pl.pallas_call(
        _outproj_kernel,
        grid=(n // tm,),
        in_specs=[pl.BlockSpec((tm, 512), row), pl.BlockSpec((tm, 512), row),
                  pl.BlockSpec((tm, D_MODEL), row), _const_spec(w.shape), _const_spec(g2.shape)],
        out_specs=[pl.BlockSpec((tm, D_MODEL), row), pl.BlockSpec((tm, D_MODEL), row)],
        out_shape=[jax.ShapeDtypeStruct((n, D_MODEL), F32), jax.ShapeDtypeStruct((n, D_MODEL), BF16)],
        compiler_params=_params(("parallel",)),
    )(og, of, x, w, g2)


def _extract_max(s, iota):
    m = jnp.max(s, axis=0, keepdims=True)
    idx = jnp.min(jnp.where(s == m, iota, s.shape[0]), axis=0, keepdims=True)
    return m, idx, jnp.where(iota == idx, -jnp.inf, s)


def _topk_rows(s, k):
    iota = lax.broadcasted_iota(I32, s.shape, 0)
    vals, idxs = [], []
    for _ in range(k):
        m, idx, s = _extract_max(s, iota)
        vals.append(m)
        idxs.append(idx)
    return jnp.concatenate(vals, axis=0), jnp.concatenate(idxs, axis=0)


def _pair_topk(v1, i1, v2, i2):
    tn = v1.shape[1]
    iota_k = lax.broadcasted_iota(I32, (PEER_TOPK, tn), 0)
    iota_8 = lax.broadcasted_iota(I32, (8, tn), 0)

    def pick(table, sel):
        rows = []
        for r in range(PEER_TOPK):
            rows.append(jnp.sum(jnp.where(iota_k == sel[r:r + 1, :], table, 0), axis=0, keepdims=True))
        return jnp.concatenate(rows, axis=0)

    blocks = [v1[0:1, :] + v2]
    for a in range(1, 8):
        blocks.append(jnp.where(iota_8 < PEER_TOPK // (a + 1), v1[a:a + 1, :] + v2[0:8, :], -jnp.inf))
    blocks.append(v1[8:16, :] + v2[0:1, :])
    top, ti = _topk_rows(jnp.concatenate(blocks, axis=0), PEER_TOPK)
    mid = ti - PEER_TOPK
    e1 = pick(i1, jnp.where(ti < 16, 0, jnp.where(ti < 72, (mid >> 3) + 1, ti - 64)))
    e2 = pick(i2, jnp.where(ti < 16, ti, jnp.where(ti < 72, mid & 7, 0)))
    ex = jnp.exp(top - jnp.max(top, axis=0, keepdims=True))
    return e1, e2, ex / jnp.sum(ex, axis=0, keepdims=True)


PEER_ROWS = 16
PEER_EC = PEER_ROWS * PEER_NKEYS
assert PEER_NEXP // PEER_EC == PEER_HEADS


def _gelu(a):
    return 0.5 * a * (1.0 + lax.erf(a * (2.0 ** -0.5)))


def _route_act_kernel(hn_ref, hp_ref, wq_ref, sk_ref, ut_ref, e1_ref, e2_ref, c_ref,
                      q_scr, r1_scr, r2_scr, rg_scr, t1_scr, t2_scr, tg_scr, acc_scr):
    i = pl.program_id(0)
    c = pl.program_id(1)
    slot_new = i % 2
    slot_old = 1 - slot_new

    @pl.when((i == 0) & (c == 0))
    def _():
        t1_scr[1] = jnp.zeros_like(t1_scr[1])
        t2_scr[1] = jnp.zeros_like(t2_scr[1])
        tg_scr[1] = jnp.zeros_like(tg_scr[1])

    @pl.when(c == 0)
    def _():
        acc_scr[...] = jnp.zeros_like(acc_scr)
        q = _dot(hn_ref[...], wq_ref[...])
        for hc in range(2 * PEER_HEADS):
            q_scr[hc] = q[:, hc * LANES:(hc + 1) * LANES].astype(BF16)

    v1, i1 = _topk_rows(_dot_nt(sk_ref[2 * c], q_scr[2 * c]), PEER_TOPK)
    v2, i2 = _topk_rows(_dot_nt(sk_ref[2 * c + 1], q_scr[2 * c + 1]), PEER_TOPK)
    e1h, e2h, gh = _pair_topk(v1, i1, v2, i2)
    rows = pl.ds(pl.multiple_of(c * PEER_TOPK, PEER_TOPK), PEER_TOPK)
    r1_scr[rows, :] = e1h
    r2_scr[rows, :] = e2h
    rg_scr[rows, :] = gh

    a_all = _dot(hp_ref[...], ut_ref[...])
    e1 = t1_scr[slot_old]
    e2 = t2_scr[slot_old]
    acc = acc_scr[...]
    for r in range(PEER_ROWS):
        picked = jnp.take_along_axis(a_all[:, r * LANES:(r + 1) * LANES], e2, axis=1)
        acc = acc + jnp.where(e1 == c * PEER_ROWS + r, picked, 0.0)
    acc_scr[...] = acc

    @pl.when(c == pl.num_programs(1) - 1)
    def _():
        c_ref[...] = tg_scr[slot_old] * _gelu(acc)
        t1_scr[slot_new] = r1_scr[...].T
        t2_scr[slot_new] = r2_scr[...].T
        tg_scr[slot_new] = rg_scr[...].T
        e1_ref[...] = t1_scr[slot_new]
        e2_ref[...] = t2_scr[slot_new]


def _route_act(h2, wq, sk, ut):
    n = h2.shape[0]
    tn = _tile(n, 512)
    tiles = n // tn
    new = lambda i, c: (jnp.minimum(i, tiles - 1), 0)
    old = lambda i, c: (jnp.maximum(i - 1, 0), 0)
    return pl.pallas_call(
        _route_act_kernel,
        grid=(tiles + 1, PEER_HEADS),
        in_specs=[pl.BlockSpec((tn, D_MODEL), new), pl.BlockSpec((tn, D_MODEL), old),
                  _const_spec(wq.shape), _const_spec(sk.shape),
                  pl.BlockSpec((D_MODEL, PEER_EC), lambda i, c: (0, c))],
        out_specs=[pl.BlockSpec((tn, PEER_J), new), pl.BlockSpec((tn, PEER_J), new),
                   pl.BlockSpec((tn, PEER_J), old)],
        out_shape=[jax.ShapeDtypeStruct((n, PEER_J), I32), jax.ShapeDtypeStruct((n, PEER_J), I32),
                   jax.ShapeDtypeStruct((n, PEER_J), F32)],
        scratch_shapes=[pltpu.VMEM((2 * PEER_HEADS, tn, LANES), BF16),
                        pltpu.VMEM((PEER_J, tn), I32), pltpu.VMEM((PEER_J, tn), I32),
                        pltpu.VMEM((PEER_J, tn), F32),
                        pltpu.VMEM((2, tn, PEER_J), I32), pltpu.VMEM((2, tn, PEER_J), I32),
                        pltpu.VMEM((2, tn, PEER_J), F32), pltpu.VMEM((tn, PEER_J), F32)],
        compiler_params=_params(("arbitrary", "arbitrary")),
    )(h2, h2, wq, sk, ut)


PEER_HALF = PEER_ROWS // 2


def _peer_out_kernel(e1_ref, e2_ref, c_ref, v_ref, x_ref, o_ref, y_scr, acc_scr):
    tn = x_ref.shape[0]
    c = pl.program_id(1)
    nchunks = PEER_NEXP // PEER_EC

    @pl.when(c == 0)
    def _():
        acc_scr[...] = jnp.zeros_like(acc_scr)
        key = lax.broadcasted_iota(I32, (PEER_NKEYS, PEER_J), 0)

        def token(t, carry):
            row = pl.ds(t, 1)
            d = jnp.where(key == e1_ref[row, :], c_ref[row, :], 0.0).astype(BF16)
            w = jnp.where(key == e2_ref[row, :], 1.0, 0.0).astype(BF16)
            y = _dot_nt(d, w).astype(BF16).astype(F32)
            bits = lax.bitcast_convert_type(y, I32)
            base = pl.multiple_of(t * PEER_HALF, PEER_HALF)
            for cc in range(nchunks):
                hi = bits[cc * PEER_ROWS:cc * PEER_ROWS + PEER_HALF, :]
                lo = bits[cc * PEER_ROWS + PEER_HALF:(cc + 1) * PEER_ROWS, :]
                y_scr[cc, pl.ds(base, PEER_HALF), :] = hi | lax.shift_right_logical(lo, 16)
            return carry

        lax.fori_loop(0, tn, token, 0, unroll=32)

    his, los = [], []
    for i in range(PEER_HALF):
        words = y_scr[c, pl.ds(i, tn, stride=PEER_HALF), :]
        his.append(lax.bitcast_convert_type(words & -65536, F32).astype(BF16))
        los.append(lax.bitcast_convert_type(words << 16, F32).astype(BF16))
    acc_scr[...] += _dot(jnp.concatenate(his + los, axis=1), v_ref[...])

    @pl.when(c == nchunks - 1)
    def _():
        o_ref[...] = x_ref[...] + acc_scr[...]


def _peer_out(e1, e2, cj, vt, x):
    n = x.shape[0]
    tn = _tile(n, 512)
    row = lambda i, c: (i, 0)
    nchunks = PEER_NEXP // PEER_EC
    return pl.pallas_call(
        _peer_out_kernel,
        grid=(n // tn, nchunks),
        in_specs=[pl.BlockSpec((tn, PEER_J), row), pl.BlockSpec((tn, PEER_J), row),
                  pl.BlockSpec((tn, PEER_J), row), pl.BlockSpec((PEER_EC, D_MODEL), lambda i, c: (c, 0)),
                  pl.BlockSpec((tn, D_MODEL), row)],
        out_specs=pl.BlockSpec((tn, D_MODEL), row),
        out_shape=jax.ShapeDtypeStruct((n, D_MODEL), F32),
        scratch_shapes=[pltpu.VMEM((nchunks, tn * PEER_HALF, LANES), I32),
                        pltpu.VMEM((tn, D_MODEL), F32)],
        compiler_params=_params(("parallel", "arbitrary")),
    )(e1, e2, cj, vt, x)


def _pair_major(lf, batch, length):
    return lf.reshape(batch, length, FOX_HEADS // 2, 2).transpose(0, 2, 3, 1)


def _prep_weights(g_norm1, w_in, w_gla_a2, b_gla_a, g_gla_onorm, g_fox_qnorm, g_fox_knorm, b_fox_f, w_out,
                  g_norm2, w_peer_q, peer_subkeys, peer_u, peer_v):
    bounds = [0]
    for size in IN_SIZES:
        bounds.append(bounds[-1] + size)
    gq, gk, gv, gg, glr, fq, fk, fv, ff = [w_in[:, bounds[i]:bounds[i + 1]] for i in range(len(IN_SIZES))]
    pad = jnp.zeros((D_MODEL, Z_SMALL - FOX_HEADS - GLA_LOWRANK), w_in.dtype)
    w_cat = jnp.concatenate([gq, gk, gv, gg, fq, fk, fv, ff, glr, pad], axis=1).astype(BF16)
    wa2p = jnp.zeros((Z_SMALL, 256), F32).at[FOX_HEADS:FOX_HEADS + GLA_LOWRANK].set(w_gla_a2).astype(BF16)
    bsm = jnp.zeros((1, Z_SMALL), F32).at[0, :FOX_HEADS].set(b_fox_f)
    head_of_col = jnp.arange(FOX_HEADS * FOX_HD) // FOX_HD
    ind = (head_of_col[:, None] == jnp.arange(LANES)[None, :]).astype(BF16)
    return dict(
        inproj=(g_norm1[None], w_cat, wa2p, b_gla_a[None], bsm, jnp.tile(g_fox_qnorm, FOX_HEADS)[None],
                jnp.tile(g_fox_knorm, FOX_HEADS)[None], ind, ind.T),
        gon=g_gla_onorm[None], w_out=w_out.astype(BF16), g2=g_norm2[None], wq=w_peer_q.astype(BF16),
        sk=peer_subkeys.reshape(2 * PEER_HEADS, PEER_NKEYS, LANES).astype(BF16),
        ut=peer_u.T.astype(BF16), v=peer_v.astype(BF16))


def _path_layer(x, wts, layer, depth, stacked, batch, seq, s0, k_past=None, v_past=None, lf_past=None):
    z, la, k_st, v_st, lf_st = _inproj(x, wts["inproj"], layer, depth, stacked)
    og, state = _gla(z, la, s0, wts["gon"], batch=batch, seq=seq, row0=0)
    past_args = {}
    if k_past is not None:
        past = k_past.shape[1]
        past_args = dict(k_past=k_past.reshape(batch, past, FOX_HEADS * FOX_HD),
                         v_past=v_past.reshape(batch, past, FOX_HEADS * FOX_HD),
                         lf_past_t=lf_past.reshape(batch, past, FOX_HEADS // 2, 2).transpose(0, 2, 3, 1))
    of = _fox(z, _pair_major(lf_st[layer], batch, seq), batch=batch, seq=seq, row0=0, **past_args)
    x, h2 = _outproj(og, of, x, wts["w_out"], wts["g2"])
    e1, e2, cj = _route_act(h2, wts["wq"], wts["sk"], wts["ut"])
    x = _peer_out(e1, e2, cj, wts["v"], x)
    return x, (k_st, v_st, lf_st), state


def kernel(x_prompt, x_sample, cache_fox_k, cache_fox_v, cache_fox_logf, state_gla, g_norm1, w_in, w_gla_a2,
           b_gla_a, g_gla_onorm, g_fox_qnorm, g_fox_knorm, b_fox_f, w_out, g_norm2, w_peer_q, peer_subkeys,
           peer_u, peer_v):
    bp, tp, _ = x_prompt.shape
    bs, ts, _ = x_sample.shape
    depth = w_in.shape[0]
    xp = x_prompt.reshape(bp * tp, D_MODEL)
    xs = x_sample.reshape(bs * ts, D_MODEL)
    stk_p, stk_s, st_p, st_s = (), (), [], []
    zero_state = jnp.zeros((bp, GLA_HEADS, GLA_DK, GLA_DV), F32)
    for l in range(depth):
        wts = _prep_weights(g_norm1[l], w_in[l], w_gla_a2[l], b_gla_a[l], g_gla_onorm[l], g_fox_qnorm[l],
                            g_fox_knorm[l], b_fox_f[l], w_out[l], g_norm2[l], w_peer_q[l], peer_subkeys[l],
                            peer_u[l], peer_v[l])
        xp, stk_p, state = _path_layer(xp, wts, l, depth, stk_p, bp, tp, zero_state)
        st_p.append(state)
        xs, stk_s, state = _path_layer(xs, wts, l, depth, stk_s, bs, ts, state_gla[l],
                                       cache_fox_k[l], cache_fox_v[l], cache_fox_logf[l])
        st_s.append(state)

    def unstack(stk, b, t):
        k, v, lf = stk
        return (k.reshape(depth, b, t, FOX_HEADS, FOX_HD), v.reshape(depth, b, t, FOX_HEADS, FOX_HD),
                lf.reshape(depth, b, t, FOX_HEADS))

    return (xp.reshape(bp, tp, D_MODEL), xs.reshape(bs, ts, D_MODEL),
            *unstack(stk_p, bp, tp), jnp.stack(st_p), *unstack(stk_s, bs, ts), jnp.stack(st_s))
```

```python
import functools

import jax
import jax.numpy as jnp
from jax import lax
from jax.experimental import pallas as pl
from jax.experimental.pallas import tpu as pltpu

F32 = jnp.float32
BF16 = jnp.bfloat16
I32 = jnp.int32

D_MODEL = 1024
EPS = 1e-6
GLA_CHUNK = 64
GLA_HEADS = 4
GLA_DK = 64
GLA_DV = 128
GLA_LOWRANK = 16
GLA_GATE_NORM = 16.0
GLA_SUB = 16
FOX_HEADS = 8
FOX_HD = 64
FOX_SCALE = FOX_HD ** -0.5
LOG2E = 1.4426950408889634
FOX_BLOCK = 512
IN_SIZES = (256, 256, 512, 512, GLA_LOWRANK, 512, 512, 512, FOX_HEADS)
PEER_HEADS = 8
PEER_NKEYS = 128
PEER_TOPK = 16
PEER_NEXP = PEER_NKEYS * PEER_NKEYS
PEER_J = PEER_HEADS * PEER_TOPK

LANES = 128
Z_MAIN = 3072
Z_SMALL = 128
VMEM_LIMIT = 56 * 1024 * 1024

ZC_GQ, ZC_GK, ZC_GV, ZC_GG, ZC_FQ, ZC_FK, ZC_FV = 0, 2, 4, 8, 12, 16, 20


def _params(sem):
    return pltpu.CompilerParams(dimension_semantics=sem, vmem_limit_bytes=VMEM_LIMIT)


def _tile(n, cap):
    t = cap
    while n % t:
        t //= 2
    return t


def _dot(a, b):
    return jnp.dot(a, b, preferred_element_type=F32)


def _dot_nt(a, b):
    return lax.dot_general(a, b, (((1,), (1,)), ((), ())), preferred_element_type=F32)


def _dot_tn(a, b):
    return lax.dot_general(a, b, (((0,), (0,)), ((), ())), preferred_element_type=F32)


def _split(x):
    hi = x.astype(BF16)
    lo = (x - hi.astype(F32)).astype(BF16)
    return hi, lo


def _dot_exact_rhs(x, m):
    hi, lo = _split(x)
    return _dot(hi, m) + _dot(lo, m)


def _dot_exact_lhs(m, x):
    hi, lo = _split(x)
    return _dot(m, hi) + _dot(m, lo)


def _log_sigmoid(y):
    return jnp.minimum(y, 0.0) - jnp.log(1.0 + jnp.exp(-jnp.abs(y)))


def _const_spec(shape):
    nd = len(shape)
    return pl.BlockSpec(shape, lambda *_: (0,) * nd)


def _inproj_kernel(*refs, n_alias):
    (x_ref, g1_ref, w_ref, wa2_ref, ba_ref, bsm_ref, gq_ref, gk_ref, ind_ref, indt_ref) = refs[:10]
    z_ref, la_ref, k32_ref, v32_ref, lf_ref = refs[10 + n_alias:]
    x = x_ref[...]
    ms = jnp.mean(x * x, axis=-1, keepdims=True)
    h = (x * lax.rsqrt(ms + EPS) * g1_ref[...]).astype(BF16)
    z = _dot(h, w_ref[...])

    def headnorm(t, gain):
        ss = _dot_exact_rhs(t * t, ind_ref[...])
        r = lax.rsqrt(ss * (1.0 / FOX_HD) + EPS)
        return t * _dot_exact_rhs(r, indt_ref[...]) * gain

    fq = headnorm(z[:, 1536:2048], gq_ref[...])
    fk = headnorm(z[:, 2048:2560], gk_ref[...])
    fv = z[:, 2560:3072]
    z_ref[:, 0:1536] = z[:, 0:1536].astype(BF16)
    z_ref[:, 1536:2048] = fq.astype(BF16)
    z_ref[:, 2048:2560] = fk.astype(BF16)
    z_ref[:, 2560:3072] = fv.astype(BF16)
    k32_ref[0] = fk
    v32_ref[0] = fv
    small = z[:, Z_MAIN:Z_MAIN + Z_SMALL]
    lf_ref[0] = _log_sigmoid(small + bsm_ref[...])[:, 0:FOX_HEADS]
    y = _dot(small.astype(BF16), wa2_ref[...]) + ba_ref[...]
    la_ref[...] = _log_sigmoid(y) * (1.0 / GLA_GATE_NORM)


def _inproj(x, consts, layer, depth, stacked):
    n = x.shape[0]
    tm = _tile(n, 512)
    row = lambda i: (i, 0)
    slab = lambda i: (layer, i, 0)
    n_alias = len(stacked)
    widths = (512, 512, FOX_HEADS)
    return pl.pallas_call(
        functools.partial(_inproj_kernel, n_alias=n_alias),
        grid=(n // tm,),
        in_specs=[pl.BlockSpec((tm, D_MODEL), row)] + [_const_spec(c.shape) for c in consts]
        + [pl.BlockSpec(memory_space=pl.ANY)] * n_alias,
        out_specs=[pl.BlockSpec((tm, Z_MAIN), row), pl.BlockSpec((tm, 256), row)]
        + [pl.BlockSpec((1, tm, w), slab) for w in widths],
        out_shape=[jax.ShapeDtypeStruct((n, Z_MAIN), BF16), jax.ShapeDtypeStruct((n, 256), F32)]
        + [jax.ShapeDtypeStruct((depth, n, w), F32) for w in widths],
        input_output_aliases={1 + len(consts) + i: 2 + i for i in range(n_alias)},
        compiler_params=_params(("parallel",)),
    )(x, *consts, *stacked)


def _gla_kernel(q_ref, k_ref, v_ref, gg_ref, la_ref, s0_ref, gon_ref, tri_ref, o_ref, sfin_ref, st_scr,
                *, chunk, nchunk):
    t = pl.program_id(1)
    nsub = chunk // GLA_SUB

    @pl.when(t == 0)
    def _():
        for p in range(2):
            st_scr[p] = s0_ref[0, p].T

    rowi = lax.broadcasted_iota(I32, (chunk, LANES), 0)
    lane = lax.broadcasted_iota(I32, (chunk, LANES), 1)
    lane_st = lax.broadcasted_iota(I32, (chunk, nsub * LANES), 1)
    lane_sq = lax.broadcasted_iota(I32, (LANES, LANES), 1)
    arow = lax.broadcasted_iota(I32, (chunk, chunk), 0)
    acol = lax.broadcasted_iota(I32, (chunk, chunk), 1)

    per_trip = next(n for n in (4, 2, 1) if nchunk % n == 0)

    def body(trip, carry):
        rows_u, prep = [], []
        for u in range(per_trip):
            rows = pl.ds(pl.multiple_of((trip * per_trip + u) * chunk, chunk), chunk)
            rows_u.append(rows)
            bcum = _dot_exact_lhs(tri_ref[...], la_ref[rows, :])
            q = q_ref[rows, :].astype(F32) * (GLA_DK ** -0.5)
            k = k_ref[rows, :].astype(F32)
            for p in range(2):
                cols = slice(p * LANES, (p + 1) * LANES)
                bp, qp, kp = bcum[:, cols], q[:, cols], k[:, cols]
                blast = bp[chunk - 1:chunk, :]
                qs, ks = [], []
                for sb in range(nsub):
                    beta = bp[sb * GLA_SUB - 1:sb * GLA_SUB, :] if sb else jnp.zeros((1, LANES), F32)
                    inblk = (rowi >= sb * GLA_SUB) & (rowi < (sb + 1) * GLA_SUB)
                    qs.append(jnp.where(inblk, qp * jnp.exp(jnp.where(inblk, bp - beta, 0.0)), 0.0))
                    valid = rowi < (sb + 1) * GLA_SUB
                    ks.append(jnp.where(valid, kp * jnp.exp(jnp.where(valid, beta - bp, 0.0)), 0.0))
                prep.append(dict(qst=jnp.concatenate(qs, axis=1), kst=jnp.concatenate(ks, axis=1).astype(BF16),
                                 qinter=qp * jnp.exp(bp), kdec=(kp * jnp.exp(blast - bp)).astype(BF16),
                                 decay=jnp.exp(blast)))
        heads = [(u, p, i) for u in range(per_trip) for p in range(2) for i in range(2)]
        hcols = [slice((2 * p + i) * GLA_DV, (2 * p + i + 1) * GLA_DV) for _, p, i in heads]
        vhs = [v_ref[rows_u[u], hc] for (u, _, _), hc in zip(heads, hcols)]
        a_s = [_dot_nt(jnp.where((lane_st & GLA_DK) == i * GLA_DK, prep[2 * u + p]["qst"], 0.0).astype(BF16),
                       prep[2 * u + p]["kst"]) for u, p, i in heads]
        upd = [_dot_tn(vh, prep[2 * u + p]["kdec"]) for (u, p, i), vh in zip(heads, vhs)]
        a_s = [jnp.where(acol <= arow, a, 0.0).astype(BF16) for a in a_s]
        intra = [_dot(a, vh) for a, vh in zip(a_s, vhs)]
        states = [st_scr[p] for p in range(2)]
        for u in range(per_trip):
            for p in range(2):
                new = states[2 * u + p] * prep[2 * u + p]["decay"] + jnp.where(
                    lane_sq < GLA_DK, upd[4 * u + 2 * p], upd[4 * u + 2 * p + 1])
                states.append(new)
        for p in range(2):
            st_scr[p] = states[2 * per_trip + p]
        stb = [st.astype(BF16) for st in states[:2 * per_trip]]
        o_s = [o_in + _dot_nt(jnp.where((lane & GLA_DK) == i * GLA_DK, prep[2 * u + p]["qinter"], 0.0)
                              .astype(BF16), stb[2 * u + p]) for (u, p, i), o_in in zip(heads, intra)]
        for (u, _, _), o, hc in zip(heads, o_s, hcols):
            on = o * lax.rsqrt(jnp.mean(o * o, axis=-1, keepdims=True) + EPS) * gon_ref[...]
            gate = gg_ref[rows_u[u], hc].astype(F32)
            o_ref[rows_u[u], hc] = (on * gate * (1.0 / (1.0 + jnp.exp(-gate)))).astype(BF16)
        return carry

    lax.fori_loop(0, nchunk // per_trip, body, 0)

    @pl.when(t == pl.num_programs(1) - 1)
    def _():
        for p in range(2):
            sfin_ref[0, p] = st_scr[p].T


def _gla(z, la, s0, gon, *, batch, seq, row0):
    chunk = min(seq, GLA_CHUNK)
    tb = min(seq, 512)
    nt = seq // tb
    blk0 = row0 // tb
    tri = jnp.tril(jnp.ones((chunk, chunk), F32)).astype(BF16)
    rows = lambda col: (lambda b, t: (blk0 + b * nt + t, col))
    state_spec = pl.BlockSpec((1, 2, LANES, LANES), lambda b, t: (b, 0, 0, 0))
    o, sfin = pl.pallas_call(
        functools.partial(_gla_kernel, chunk=chunk, nchunk=tb // chunk),
        grid=(batch, nt),
        in_specs=[pl.BlockSpec((tb, 256), rows(0)), pl.BlockSpec((tb, 256), rows(1)),
                  pl.BlockSpec((tb, 512), rows(1)), pl.BlockSpec((tb, 512), rows(2)),
                  pl.BlockSpec((tb, 256), rows(0)), state_spec,
                  _const_spec((1, GLA_DV)), _const_spec((chunk, chunk))],
        out_specs=[pl.BlockSpec((tb, 512), lambda b, t: (b * nt + t, 0)), state_spec],
        out_shape=[jax.ShapeDtypeStruct((batch * seq, 512), BF16),
                   jax.ShapeDtypeStruct((batch, 2, LANES, LANES), F32)],
        scratch_shapes=[pltpu.VMEM((2, LANES, LANES), F32)],
        compiler_params=_params(("parallel", "arbitrary")),
    )(z, z, z, z, la, s0.reshape(batch, 2, LANES, LANES), gon, tri)
    return o, sfin.reshape(batch, GLA_HEADS, GLA_DK, GLA_DV)


def _fox_kernel(*refs, seq, past, tq, has_past):
    if has_past:
        (q_ref, kc_ref, vc_ref, lfc_ref, triu_ref, kp_ref, vp_ref, lfp_ref, triup_ref,
         o_ref, cc_scr, cp_scr) = refs
    else:
        q_ref, kc_ref, vc_ref, lfc_ref, triu_ref, o_ref, cc_scr = refs
    i = pl.program_id(2)
    nq = seq // tq
    pblk = FOX_BLOCK

    @pl.when(i == 0)
    def _():
        carry = jnp.zeros((2, 1), F32)
        if has_past:
            for jb in range(past // pblk):
                x = lfp_ref[0, 0, :, jb * pblk:(jb + 1) * pblk]
                cp_scr[:, jb * pblk:(jb + 1) * pblk] = _dot_exact_rhs(x, triup_ref[...]) + carry
                carry = carry + jnp.sum(x, axis=1, keepdims=True)
        for jb in range(nq):
            x = lfc_ref[0, 0, :, jb * tq:(jb + 1) * tq]
            cc_scr[:, jb * tq:(jb + 1) * tq] = _dot_exact_rhs(x, triu_ref[...]) + carry
            carry = carry + jnp.sum(x, axis=1, keepdims=True)

    q = q_ref[...]
    lane = lax.broadcasted_iota(I32, (tq, LANES), 1)
    qh = [jnp.where(lane < FOX_HD, q, jnp.zeros_like(q)), jnp.where(lane >= FOX_HD, q, jnp.zeros_like(q))]

    def kv_step(state, kb, vb, crows, mask):
        hs = range(2)
        s = [_dot_nt(qh[h], kb) * (FOX_SCALE * LOG2E) - crows[h] * LOG2E for h in hs]
        if mask is not None:
            s = [jnp.where(mask, s[h], -jnp.inf) for h in hs]
        m_new = [jnp.maximum(state[h][0], jnp.max(s[h], axis=1, keepdims=True)) for h in hs]
        pr = [jnp.exp2(s[h] - m_new[h]) for h in hs]
        alpha = [jnp.exp2(state[h][0] - m_new[h]) for h in hs]
        l = [alpha[h] * state[h][1] + jnp.sum(pr[h], axis=1, keepdims=True) for h in hs]
        pv = [_dot(pr[h].astype(BF16), vb) for h in hs]
        return tuple((m_new[h], l[h], alpha[h] * state[h][2] + pv[h]) for h in hs)

    init = tuple((jnp.full((tq, 1), -jnp.inf, F32), jnp.zeros((tq, 1), F32), jnp.zeros((tq, LANES), F32))
                 for _ in range(2))
    state = init
    if has_past:
        def past_body(j, st):
            off = pl.multiple_of(j * pblk, pblk)
            kb = kp_ref[0, pl.ds(off, pblk), :].astype(BF16)
            vb = vp_ref[0, pl.ds(off, pblk), :].astype(BF16)
            crows = [cp_scr[h:h + 1, pl.ds(off, pblk)] for h in range(2)]
            return kv_step(st, kb, vb, crows, None)
        state = lax.fori_loop(0, past // pblk, past_body, state)
    if nq > 1:
        def cur_body(j, st):
            off = pl.multiple_of(j * tq, tq)
            crows = [cc_scr[h:h + 1, pl.ds(off, tq)] for h in range(2)]
            return kv_step(st, kc_ref[pl.ds(off, tq), :], vc_ref[pl.ds(off, tq), :], crows, None)
        state = lax.fori_loop(0, i, cur_body, state)
        off = pl.multiple_of(i * tq, tq)
    else:
        off = 0
    causal = lax.broadcasted_iota(I32, (tq, tq), 1) <= lax.broadcasted_iota(I32, (tq, tq), 0)
    crows = [cc_scr[h:h + 1, pl.ds(off, tq)] for h in range(2)]
    state = kv_step(state, kc_ref[pl.ds(off, tq), :], vc_ref[pl.ds(off, tq), :], crows, causal)
    o0 = state[0][2] / state[0][1]
    o1 = state[1][2] / state[1][1]
    o_ref[...] = jnp.where(lane < FOX_HD, o0, o1).astype(BF16)


def _fox(z, lf_cur_t, *, batch, seq, row0, k_past=None, v_past=None, lf_past_t=None):
    has_past = k_past is not None
    past = k_past.shape[1] if has_past else 0
    tq = min(seq, FOX_BLOCK)
    nq = seq // tq
    qblk0 = row0 // tq
    sblk0 = row0 // seq
    triu = jnp.triu(jnp.ones((tq, tq), F32)).astype(BF16)
    in_specs = [pl.BlockSpec((tq, LANES), lambda b, p, i: (qblk0 + b * nq + i, ZC_FQ + p)),
                pl.BlockSpec((seq, LANES), lambda b, p, i: (sblk0 + b, ZC_FK + p)),
                pl.BlockSpec((seq, LANES), lambda b, p, i: (sblk0 + b, ZC_FV + p)),
                pl.BlockSpec((1, 1, 2, seq), lambda b, p, i: (b, p, 0, 0)),
                _const_spec((tq, tq))]
    args = [z, z, z, lf_cur_t, triu]
    scratch = [pltpu.VMEM((2, seq), F32)]
    if has_past:
        triup = jnp.triu(jnp.ones((FOX_BLOCK, FOX_BLOCK), F32)).astype(BF16)
        in_specs += [pl.BlockSpec((1, past, LANES), lambda b, p, i: (b, 0, p)),
                     pl.BlockSpec((1, past, LANES), lambda b, p, i: (b, 0, p)),
                     pl.BlockSpec((1, 1, 2, past), lambda b, p, i: (b, p, 0, 0)),
                     _const_spec((FOX_BLOCK, FOX_BLOCK))]
        args += [k_past, v_past, lf_past_t, triup]
        scratch.append(pltpu.VMEM((2, past), F32))
    return pl.pallas_call(
        functools.partial(_fox_kernel, seq=seq, past=past, tq=tq, has_past=has_past),
        grid=(batch, FOX_HEADS // 2, nq),
        in_specs=in_specs,
        out_specs=pl.BlockSpec((tq, LANES), lambda b, p, i: (b * nq + i, p)),
        out_shape=jax.ShapeDtypeStruct((batch * seq, 512), BF16),
        scratch_shapes=scratch,
        compiler_params=_params(("parallel", "parallel", "arbitrary")),
    )(*args)


def _outproj_kernel(og_ref, of_ref, x_ref, w_ref, g2_ref, xo_ref, h2_ref):
    mixed = _dot(og_ref[...], w_ref[0:512, :]) + _dot(of_ref[...], w_ref[512:1024, :])
    x = x_ref[...] + mixed
    xo_ref[...] = x
    ms = jnp.mean(x * x, axis=-1, keepdims=True)
    h2_ref[...] = (x * lax.rsqrt(ms + EPS) * g2_ref[...]).astype(BF16)


def _outproj(og, of, x, w, g2):
    n = x.shape[0]
    tm = _tile(n, 512)
    row = lambda i: (i, 0)
    return pl.pallas_call(
        _outproj_kernel,
        grid=(n // tm,),
        in_specs=[pl.BlockSpec((tm, 512), row), pl.BlockSpec((tm, 512), row),
                  pl.BlockSpec((tm, D_MODEL), row), _const_spec(w.shape), _const_spec(g2.shape)],
        out_specs=[pl.BlockSpec((tm, D_MODEL), row), pl.BlockSpec((tm, D_MODEL), row)],
        out_shape=[jax.ShapeDtypeStruct((n, D_MODEL), F32), jax.ShapeDtypeStruct((n, D_MODEL), BF16)],
        compiler_params=_params(("parallel",)),
    )(og, of, x, w, g2)


def _extract_max(s, iota):
    m = jnp.max(s, axis=0, keepdims=True)
    idx = jnp.min(jnp.where(s == m, iota, s.shape[0]), axis=0, keepdims=True)
    return m, idx, jnp.where(iota == idx, -jnp.inf, s)


def _topk_rows(s, k):
    iota = lax.broadcasted_iota(I32, s.shape, 0)
    vals, idxs = [], []
    for _ in range(k):
        m, idx, s = _extract_max(s, iota)
        vals.append(m)
        idxs.append(idx)
    return jnp.concatenate(vals, axis=0), jnp.concatenate(idxs, axis=0)


def _pair_topk(v1, i1, v2, i2):
    tn = v1.shape[1]
    iota_k = lax.broadcasted_iota(I32, (PEER_TOPK, tn), 0)
    iota_8 = lax.broadcasted_iota(I32, (8, tn), 0)

    def pick(table, sel):
        rows = []
        for r in range(PEER_TOPK):
            rows.append(jnp.sum(jnp.where(iota_k == sel[r:r + 1, :], table, 0), axis=0, keepdims=True))
        return jnp.concatenate(rows, axis=0)

    blocks = [v1[0:1, :] + v2]
    for a in range(1, 8):
        blocks.append(jnp.where(iota_8 < PEER_TOPK // (a + 1), v1[a:a + 1, :] + v2[0:8, :], -jnp.inf))
    blocks.append(v1[8:16, :] + v2[0:1, :])
    top, ti = _topk_rows(jnp.concatenate(blocks, axis=0), PEER_TOPK)
    mid = ti - PEER_TOPK
    e1 = pick(i1, jnp.where(ti < 16, 0, jnp.where(ti < 72, (mid >> 3) + 1, ti - 64)))
    e2 = pick(i2, jnp.where(ti < 16, ti, jnp.where(ti < 72, mid & 7, 0)))
    ex = jnp.exp(top - jnp.max(top, axis=0, keepdims=True))
    return e1, e2, ex / jnp.sum(ex, axis=0, keepdims=True)


PEER_ROWS = 16
PEER_EC = PEER_ROWS * PEER_NKEYS
assert PEER_NEXP // PEER_EC == PEER_HEADS


def _gelu(a):
    return 0.5 * a * (1.0 + lax.erf(a * (2.0 ** -0.5)))


def _route_act_kernel(hn_ref, hp_ref, wq_ref, sk_ref, ut_ref, e1_ref, e2_ref, c_ref,
                      q_scr, r1_scr, r2_scr, rg_scr, t1_scr, t2_scr, tg_scr, acc_scr):
    i = pl.program_id(0)
    c = pl.program_id(1)
    slot_new = i % 2
    slot_old = 1 - slot_new

    @pl.when((i == 0) & (c == 0))
    def _():
        t1_scr[1] = jnp.zeros_like(t1_scr[1])
        t2_scr[1] = jnp.zeros_like(t2_scr[1])
        tg_scr[1] = jnp.zeros_like(tg_scr[1])

    @pl.when(c == 0)
    def _():
        acc_scr[...] = jnp.zeros_like(acc_scr)
        q = _dot(hn_ref[...], wq_ref[...])
        for hc in range(2 * PEER_HEADS):
            q_scr[hc] = q[:, hc * LANES:(hc + 1) * LANES].astype(BF16)

    v1, i1 = _topk_rows(_dot_nt(sk_ref[2 * c], q_scr[2 * c]), PEER_TOPK)
    v2, i2 = _topk_rows(_dot_nt(sk_ref[2 * c + 1], q_scr[2 * c + 1]), PEER_TOPK)
    e1h, e2h, gh = _pair_topk(v1, i1, v2, i2)
    rows = pl.ds(pl.multiple_of(c * PEER_TOPK, PEER_TOPK), PEER_TOPK)
    r1_scr[rows, :] = e1h
    r2_scr[rows, :] = e2h
    rg_scr[rows, :] = gh

    a_all = _dot(hp_ref[...], ut_ref[...])
    e1 = t1_scr[slot_old]
    e2 = t2_scr[slot_old]
    acc = acc_scr[...]
    for r in range(PEER_ROWS):
        picked = jnp.take_along_axis(a_all[:, r * LANES:(r + 1) * LANES], e2, axis=1,
                                     mode="promise_in_bounds")
        acc = jnp.where(e1 == c * PEER_ROWS + r, picked, acc)
    acc_scr[...] = acc

    @pl.when(c == pl.num_programs(1) - 1)
    def _():
        c_ref[...] = tg_scr[slot_old] * _gelu(acc)
        t1_scr[slot_new] = r1_scr[...].T
        t2_scr[slot_new] = r2_scr[...].T
        tg_scr[slot_new] = rg_scr[...].T
        e1_ref[...] = t1_scr[slot_new]
        e2_ref[...] = t2_scr[slot_new]


def _route_act(h2, wq, sk, ut):
    n = h2.shape[0]
    tn = _tile(n, 512)
    tiles = n // tn
    new = lambda i, c: (jnp.minimum(i, tiles - 1), 0)
    old = lambda i, c: (jnp.maximum(i - 1, 0), 0)
    return pl.pallas_call(
        _route_act_kernel,
        grid=(tiles + 1, PEER_HEADS),
        in_specs=[pl.BlockSpec((tn, D_MODEL), new), pl.BlockSpec((tn, D_MODEL), old),
                  _const_spec(wq.shape), _const_spec(sk.shape),
                  pl.BlockSpec((D_MODEL, PEER_EC), lambda i, c: (0, c))],
        out_specs=[pl.BlockSpec((tn, PEER_J), new), pl.BlockSpec((tn, PEER_J), new),
                   pl.BlockSpec((tn, PEER_J), old)],
        out_shape=[jax.ShapeDtypeStruct((n, PEER_J), I32), jax.ShapeDtypeStruct((n, PEER_J), I32),
                   jax.ShapeDtypeStruct((n, PEER_J), F32)],
        scratch_shapes=[pltpu.VMEM((2 * PEER_HEADS, tn, LANES), BF16),
                        pltpu.VMEM((PEER_J, tn), I32), pltpu.VMEM((PEER_J, tn), I32),
                        pltpu.VMEM((PEER_J, tn), F32),
                        pltpu.VMEM((2, tn, PEER_J), I32), pltpu.VMEM((2, tn, PEER_J), I32),
                        pltpu.VMEM((2, tn, PEER_J), F32), pltpu.VMEM((tn, PEER_J), F32)],
        compiler_params=_params(("arbitrary", "arbitrary")),
    )(h2, h2, wq, sk, ut)


PEER_HALF = PEER_ROWS // 2


def _peer_out_kernel(e1_ref, e2_ref, c_ref, v_ref, x_ref, o_ref, y_scr, acc_scr):
    tn = x_ref.shape[0]
    c = pl.program_id(1)
    nchunks = PEER_NEXP // PEER_EC

    @pl.when(c == 0)
    def _():
        acc_scr[...] = jnp.zeros_like(acc_scr)
        key = lax.broadcasted_iota(I32, (PEER_NKEYS, PEER_J), 0)

        def token(t, carry):
            row = pl.ds(t, 1)
            d = jnp.where(key == e1_ref[row, :], c_ref[row, :], 0.0).astype(BF16)
            w = jnp.where(key == e2_ref[row, :], 1.0, 0.0).astype(BF16)
            y = _dot_nt(d, w).astype(BF16).astype(F32)
            bits = lax.bitcast_convert_type(y, I32)
            base = pl.multiple_of(t * PEER_HALF, PEER_HALF)
            for cc in range(nchunks):
                hi = bits[cc * PEER_ROWS:cc * PEER_ROWS + PEER_HALF, :]
                lo = bits[cc * PEER_ROWS + PEER_HALF:(cc + 1) * PEER_ROWS, :]
                y_scr[cc, pl.ds(base, PEER_HALF), :] = hi | lax.shift_right_logical(lo, 16)
            return carry

        lax.fori_loop(0, tn, token, 0, unroll=32)

    his, los = [], []
    for i in range(PEER_HALF):
        words = y_scr[c, pl.ds(i, tn, stride=PEER_HALF), :]
        his.append(lax.bitcast_convert_type(words & -65536, F32).astype(BF16))
        los.append(lax.bitcast_convert_type(words << 16, F32).astype(BF16))
    acc_scr[...] += _dot(jnp.concatenate(his + los, axis=1), v_ref[...])

    @pl.when(c == nchunks - 1)
    def _():
        o_ref[...] = x_ref[...] + acc_scr[...]


def _peer_out(e1, e2, cj, vt, x):
    n = x.shape[0]
    tn = _tile(n, 512)
    row = lambda i, c: (i, 0)
    nchunks = PEER_NEXP // PEER_EC
    return pl.pallas_call(
        _peer_out_kernel,
        grid=(n // tn, nchunks),
        in_specs=[pl.BlockSpec((tn, PEER_J), row), pl.BlockSpec((tn, PEER_J), row),
                  pl.BlockSpec((tn, PEER_J), row), pl.BlockSpec((PEER_EC, D_MODEL), lambda i, c: (c, 0)),
                  pl.BlockSpec((tn, D_MODEL), row)],
        out_specs=pl.BlockSpec((tn, D_MODEL), row),
        out_shape=jax.ShapeDtypeStruct((n, D_MODEL), F32),
        scratch_shapes=[pltpu.VMEM((nchunks, tn * PEER_HALF, LANES), I32),
                        pltpu.VMEM((tn, D_MODEL), F32)],
        compiler_params=_params(("parallel", "arbitrary")),
    )(e1, e2, cj, vt, x)


def _pair_major(lf, batch, length):
    return lf.reshape(batch, length, FOX_HEADS // 2, 2).transpose(0, 2, 3, 1)


def _prep_weights(g_norm1, w_in, w_gla_a2, b_gla_a, g_gla_onorm, g_fox_qnorm, g_fox_knorm, b_fox_f, w_out,
                  g_norm2, w_peer_q, peer_subkeys, peer_u, peer_v):
    bounds = [0]
    for size in IN_SIZES:
        bounds.append(bounds[-1] + size)
    gq, gk, gv, gg, glr, fq, fk, fv, ff = [w_in[:, bounds[i]:bounds[i + 1]] for i in range(len(IN_SIZES))]
    pad = jnp.zeros((D_MODEL, Z_SMALL - FOX_HEADS - GLA_LOWRANK), w_in.dtype)
    w_cat = jnp.concatenate([gq, gk, gv, gg, fq, fk, fv, ff, glr, pad], axis=1).astype(BF16)
    wa2p = jnp.zeros((Z_SMALL, 256), F32).at[FOX_HEADS:FOX_HEADS + GLA_LOWRANK].set(w_gla_a2).astype(BF16)
    bsm = jnp.zeros((1, Z_SMALL), F32).at[0, :FOX_HEADS].set(b_fox_f)
    head_of_col = jnp.arange(FOX_HEADS * FOX_HD) // FOX_HD
    ind = (head_of_col[:, None] == jnp.arange(LANES)[None, :]).astype(BF16)
    return dict(
        inproj=(g_norm1[None], w_cat, wa2p, b_gla_a[None], bsm, jnp.tile(g_fox_qnorm, FOX_HEADS)[None],
                jnp.tile(g_fox_knorm, FOX_HEADS)[None], ind, ind.T),
        gon=g_gla_onorm[None], w_out=w_out.astype(BF16), g2=g_norm2[None], wq=w_peer_q.astype(BF16),
        sk=peer_subkeys.reshape(2 * PEER_HEADS, PEER_NKEYS, LANES).astype(BF16),
        ut=peer_u.T.astype(BF16), v=peer_v.astype(BF16))


def _path_layer(x, wts, layer, depth, stacked, batch, seq, s0, k_past=None, v_past=None, lf_past=None):
    z, la, k_st, v_st, lf_st = _inproj(x, wts["inproj"], layer, depth, stacked)
    og, state = _gla(z, la, s0, wts["gon"], batch=batch, seq=seq, row0=0)
    past_args = {}
    if k_past is not None:
        past = k_past.shape[1]
        past_args = dict(k_past=k_past.reshape(batch, past, FOX_HEADS * FOX_HD),
                         v_past=v_past.reshape(batch, past, FOX_HEADS * FOX_HD),
                         lf_past_t=lf_past.reshape(batch, past, FOX_HEADS // 2, 2).transpose(0, 2, 3, 1))
    of = _fox(z, _pair_major(lf_st[layer], batch, seq), batch=batch, seq=seq, row0=0, **past_args)
    x, h2 = _outproj(og, of, x, wts["w_out"], wts["g2"])
    e1, e2, cj = _route_act(h2, wts["wq"], wts["sk"], wts["ut"])
    x = _peer_out(e1, e2, cj, wts["v"], x)
    return x, (k_st, v_st, lf_st), state


def kernel(x_prompt, x_sample, cache_fox_k, cache_fox_v, cache_fox_logf, state_gla, g_norm1, w_in, w_gla_a2,
           b_gla_a, g_gla_onorm, g_fox_qnorm, g_fox_knorm, b_fox_f, w_out, g_norm2, w_peer_q, peer_subkeys,
           peer_u, peer_v):
    bp, tp, _ = x_prompt.shape
    bs, ts, _ = x_sample.shape
    depth = w_in.shape[0]
    xp = x_prompt.reshape(bp * tp, D_MODEL)
    xs = x_sample.reshape(bs * ts, D_MODEL)
    stacks = lambda n: tuple(jnp.zeros((depth, n, w), F32) for w in (512, 512, FOX_HEADS))
    stk_p, stk_s, st_p, st_s = stacks(bp * tp), stacks(bs * ts), [], []
    zero_state = jnp.zeros((bp, GLA_HEADS, GLA_DK, GLA_DV), F32)
    for l in range(depth):
        wts = _prep_weights(g_norm1[l], w_in[l], w_gla_a2[l], b_gla_a[l], g_gla_onorm[l], g_fox_qnorm[l],
                            g_fox_knorm[l], b_fox_f[l], w_out[l], g_norm2[l], w_peer_q[l], peer_subkeys[l],
                            peer_u[l], peer_v[l])
        xp, stk_p, state = _path_layer(xp, wts, l, depth, stk_p, bp, tp, zero_state)
        st_p.append(state)
        xs, stk_s, state = _path_layer(xs, wts, l, depth, stk_s, bs, ts, state_gla[l],
                                       cache_fox_k[l], cache_fox_v[l], cache_fox_logf[l])
        st_s.append(state)

    def unstack(stk, b, t):
        k, v, lf = stk
        return (k.reshape(depth, b, t, FOX_HEADS, FOX_HD), v.reshape(depth, b, t, FOX_HEADS, FOX_HD),
                lf.reshape(depth, b, t, FOX_HEADS))

    return (xp.reshape(bp, tp, D_MODEL), xs.reshape(bs, ts, D_MODEL),
            *unstack(stk_p, bp, tp), jnp.stack(st_p), *unstack(stk_s, bs, ts), jnp.stack(st_s))
```

```python
import functools

import jax
import jax.numpy as jnp
from jax import lax
from jax.experimental import pallas as pl
from jax.experimental.pallas import tpu as pltpu

F32 = jnp.float32
BF16 = jnp.bfloat16
I32 = jnp.int32

D_MODEL = 1024
EPS = 1e-6
GLA_CHUNK = 64
GLA_HEADS = 4
GLA_DK = 64
GLA_DV = 128
GLA_LOWRANK = 16
GLA_GATE_NORM = 16.0
GLA_SUB = 16
FOX_HEADS = 8
FOX_HD = 64
FOX_SCALE = FOX_HD ** -0.5
LOG2E = 1.4426950408889634
FOX_BLOCK = 512
IN_SIZES = (256, 256, 512, 512, GLA_LOWRANK, 512, 512, 512, FOX_HEADS)
PEER_HEADS = 8
PEER_NKEYS = 128
PEER_TOPK = 16
PEER_NEXP = PEER_NKEYS * PEER_NKEYS
PEER_J = PEER_HEADS * PEER_TOPK

LANES = 128
SUBLANES = 8
Z_MAIN = 3072
Z_SMALL = 128
VMEM_LIMIT = 56 * 1024 * 1024

ZC_GQ, ZC_GK, ZC_GV, ZC_GG, ZC_FQ, ZC_FK, ZC_FV = 0, 2, 4, 8, 12, 16, 20


def _params(sem):
    return pltpu.CompilerParams(dimension_semantics=sem, vmem_limit_bytes=VMEM_LIMIT)


def _tile(n, cap):
    t = cap
    while n % t:
        t //= 2
    return t


def _dot(a, b):
    return jnp.dot(a, b, preferred_element_type=F32)


def _dot_nt(a, b):
    return lax.dot_general(a, b, (((1,), (1,)), ((), ())), preferred_element_type=F32)


def _dot_tn(a, b):
    return lax.dot_general(a, b, (((0,), (0,)), ((), ())), preferred_element_type=F32)


def _split(x):
    hi = x.astype(BF16)
    lo = (x - hi.astype(F32)).astype(BF16)
    return hi, lo


def _dot_exact_rhs(x, m):
    hi, lo = _split(x)
    return _dot(hi, m) + _dot(lo, m)


def _dot_exact_lhs(m, x):
    hi, lo = _split(x)
    return _dot(m, hi) + _dot(m, lo)


def _log_sigmoid(y):
    return jnp.minimum(y, 0.0) - jnp.log(1.0 + jnp.exp(-jnp.abs(y)))


def _const_spec(shape):
    nd = len(shape)
    return pl.BlockSpec(shape, lambda *_: (0,) * nd)


def _inproj_kernel(*refs, n_alias):
    (x_ref, g1_ref, w_ref, wa2_ref, ba_ref, bsm_ref, gq_ref, gk_ref, ind_ref, indt_ref) = refs[:10]
    z_ref, la_ref, k32_ref, v32_ref, lf_ref = refs[10 + n_alias:]
    x = x_ref[...]
    ms = jnp.mean(x * x, axis=-1, keepdims=True)
    h = (x * lax.rsqrt(ms + EPS) * g1_ref[...]).astype(BF16)
    z = _dot(h, w_ref[...])

    def headnorm(t, gain):
        ss = _dot_exact_rhs(t * t, ind_ref[...])
        r = lax.rsqrt(ss * (1.0 / FOX_HD) + EPS)
        return t * _dot_exact_rhs(r, indt_ref[...]) * gain

    fq = headnorm(z[:, 1536:2048], gq_ref[...])
    fk = headnorm(z[:, 2048:2560], gk_ref[...])
    fv = z[:, 2560:3072]
    z_ref[:, 0:1536] = z[:, 0:1536].astype(BF16)
    z_ref[:, 1536:2048] = fq.astype(BF16)
    z_ref[:, 2048:2560] = fk.astype(BF16)
    z_ref[:, 2560:3072] = fv.astype(BF16)
    k32_ref[0] = fk
    v32_ref[0] = fv
    small = z[:, Z_MAIN:Z_MAIN + Z_SMALL]
    lf_ref[0] = _log_sigmoid(small + bsm_ref[...])[:, 0:FOX_HEADS]
    y = _dot(small.astype(BF16), wa2_ref[...]) + ba_ref[...]
    la_ref[...] = _log_sigmoid(y) * (1.0 / GLA_GATE_NORM)


def _inproj(x, consts, layer, depth, stacked):
    n = x.shape[0]
    tm = _tile(n, 512)
    row = lambda i: (i, 0)
    slab = lambda i: (layer, i, 0)
    n_alias = len(stacked)
    widths = (512, 512, FOX_HEADS)
    return pl.pallas_call(
        functools.partial(_inproj_kernel, n_alias=n_alias),
        grid=(n // tm,),
        in_specs=[pl.BlockSpec((tm, D_MODEL), row)] + [_const_spec(c.shape) for c in consts]
        + [pl.BlockSpec(memory_space=pl.ANY)] * n_alias,
        out_specs=[pl.BlockSpec((tm, Z_MAIN), row), pl.BlockSpec((tm, 256), row)]
        + [pl.BlockSpec((1, tm, w), slab) for w in widths],
        out_shape=[jax.ShapeDtypeStruct((n, Z_MAIN), BF16), jax.ShapeDtypeStruct((n, 256), F32)]
        + [jax.ShapeDtypeStruct((depth, n, w), F32) for w in widths],
        input_output_aliases={1 + len(consts) + i: 2 + i for i in range(n_alias)},
        compiler_params=_params(("parallel",)),
    )(x, *consts, *stacked)


def _gla_kernel(q_ref, k_ref, v_ref, gg_ref, la_ref, s0_ref, gon_ref, tri_ref, o_ref, sfin_ref, st_scr,
                *, chunk, nchunk):
    t = pl.program_id(1)
    nsub = chunk // GLA_SUB

    @pl.when(t == 0)
    def _():
        for p in range(2):
            st_scr[p] = s0_ref[0, p].T

    rowi = lax.broadcasted_iota(I32, (chunk, LANES), 0)
    lane = lax.broadcasted_iota(I32, (chunk, LANES), 1)
    lane_st = lax.broadcasted_iota(I32, (chunk, nsub * LANES), 1)
    lane_sq = lax.broadcasted_iota(I32, (LANES, LANES), 1)
    arow = lax.broadcasted_iota(I32, (chunk, chunk), 0)
    acol = lax.broadcasted_iota(I32, (chunk, chunk), 1)

    per_trip = next(n for n in (4, 2, 1) if nchunk % n == 0)

    def body(trip, carry):
        rows_u, prep = [], []
        for u in range(per_trip):
            rows = pl.ds(pl.multiple_of((trip * per_trip + u) * chunk, chunk), chunk)
            rows_u.append(rows)
            bcum = _dot_exact_lhs(tri_ref[...], la_ref[rows, :])
            q = q_ref[rows, :].astype(F32) * (GLA_DK ** -0.5)
            k = k_ref[rows, :].astype(F32)
            for p in range(2):
                cols = slice(p * LANES, (p + 1) * LANES)
                bp, qp, kp = bcum[:, cols], q[:, cols], k[:, cols]
                blast = bp[chunk - 1:chunk, :]
                qs, ks = [], []
                for sb in range(nsub):
                    beta = bp[sb * GLA_SUB - 1:sb * GLA_SUB, :] if sb else jnp.zeros((1, LANES), F32)
                    inblk = (rowi >= sb * GLA_SUB) & (rowi < (sb + 1) * GLA_SUB)
                    qs.append(jnp.where(inblk, qp * jnp.exp(jnp.where(inblk, bp - beta, 0.0)), 0.0))
                    valid = rowi < (sb + 1) * GLA_SUB
                    ks.append(jnp.where(valid, kp * jnp.exp(jnp.where(valid, beta - bp, 0.0)), 0.0))
                prep.append(dict(qst=jnp.concatenate(qs, axis=1), kst=jnp.concatenate(ks, axis=1).astype(BF16),
                                 qinter=qp * jnp.exp(bp), kdec=(kp * jnp.exp(blast - bp)).astype(BF16),
                                 decay=jnp.exp(blast)))
        heads = [(u, p, i) for u in range(per_trip) for p in range(2) for i in range(2)]
        hcols = [slice((2 * p + i) * GLA_DV, (2 * p + i + 1) * GLA_DV) for _, p, i in heads]
        vhs = [v_ref[rows_u[u], hc] for (u, _, _), hc in zip(heads, hcols)]
        a_s = [_dot_nt(jnp.where((lane_st & GLA_DK) == i * GLA_DK, prep[2 * u + p]["qst"], 0.0).astype(BF16),
                       prep[2 * u + p]["kst"]) for u, p, i in heads]
        upd = [_dot_tn(vh, prep[2 * u + p]["kdec"]) for (u, p, i), vh in zip(heads, vhs)]
        a_s = [jnp.where(acol <= arow, a, 0.0).astype(BF16) for a in a_s]
        intra = [_dot(a, vh) for a, vh in zip(a_s, vhs)]
        states = [st_scr[p] for p in range(2)]
        for u in range(per_trip):
            for p in range(2):
                new = states[2 * u + p] * prep[2 * u + p]["decay"] + jnp.where(
                    lane_sq < GLA_DK, upd[4 * u + 2 * p], upd[4 * u + 2 * p + 1])
                states.append(new)
        for p in range(2):
            st_scr[p] = states[2 * per_trip + p]
        stb = [st.astype(BF16) for st in states[:2 * per_trip]]
        o_s = [o_in + _dot_nt(jnp.where((lane & GLA_DK) == i * GLA_DK, prep[2 * u + p]["qinter"], 0.0)
                              .astype(BF16), stb[2 * u + p]) for (u, p, i), o_in in zip(heads, intra)]
        for (u, _, _), o, hc in zip(heads, o_s, hcols):
            on = o * lax.rsqrt(jnp.mean(o * o, axis=-1, keepdims=True) + EPS) * gon_ref[...]
            gate = gg_ref[rows_u[u], hc].astype(F32)
            o_ref[rows_u[u], hc] = (on * gate * (1.0 / (1.0 + jnp.exp(-gate)))).astype(BF16)
        return carry

    lax.fori_loop(0, nchunk // per_trip, body, 0)

    @pl.when(t == pl.num_programs(1) - 1)
    def _():
        for p in range(2):
            sfin_ref[0, p] = st_scr[p].T


def _gla(z, la, s0, gon, *, batch, seq, row0):
    chunk = min(seq, GLA_CHUNK)
    tb = min(seq, 512)
    nt = seq // tb
    blk0 = row0 // tb
    tri = jnp.tril(jnp.ones((chunk, chunk), F32)).astype(BF16)
    rows = lambda col: (lambda b, t: (blk0 + b * nt + t, col))
    state_spec = pl.BlockSpec((1, 2, LANES, LANES), lambda b, t: (b, 0, 0, 0))
    o, sfin = pl.pallas_call(
        functools.partial(_gla_kernel, chunk=chunk, nchunk=tb // chunk),
        grid=(batch, nt),
        in_specs=[pl.BlockSpec((tb, 256), rows(0)), pl.BlockSpec((tb, 256), rows(1)),
                  pl.BlockSpec((tb, 512), rows(1)), pl.BlockSpec((tb, 512), rows(2)),
                  pl.BlockSpec((tb, 256), rows(0)), state_spec,
                  _const_spec((1, GLA_DV)), _const_spec((chunk, chunk))],
        out_specs=[pl.BlockSpec((tb, 512), lambda b, t: (b * nt + t, 0)), state_spec],
        out_shape=[jax.ShapeDtypeStruct((batch * seq, 512), BF16),
                   jax.ShapeDtypeStruct((batch, 2, LANES, LANES), F32)],
        scratch_shapes=[pltpu.VMEM((2, LANES, LANES), F32)],
        compiler_params=_params(("parallel", "arbitrary")),
    )(z, z, z, z, la, s0.reshape(batch, 2, LANES, LANES), gon, tri)
    return o, sfin.reshape(batch, GLA_HEADS, GLA_DK, GLA_DV)


def _fox_kernel(*refs, seq, past, tq, has_past):
    if has_past:
        (q_ref, kc_ref, vc_ref, lfc_ref, triu_ref, kp_ref, vp_ref, lfp_ref, triup_ref,
         o_ref, cc_scr, cp_scr) = refs
    else:
        q_ref, kc_ref, vc_ref, lfc_ref, triu_ref, o_ref, cc_scr = refs
    i = pl.program_id(2)
    nq = seq // tq
    pblk = FOX_BLOCK

    @pl.when(i == 0)
    def _():
        carry = jnp.zeros((2, 1), F32)
        if has_past:
            for jb in range(past // pblk):
                x = lfp_ref[0, 0, :, jb * pblk:(jb + 1) * pblk]
                cp_scr[:, jb * pblk:(jb + 1) * pblk] = _dot_exact_rhs(x, triup_ref[...]) + carry
                carry = carry + jnp.sum(x, axis=1, keepdims=True)
        for jb in range(nq):
            x = lfc_ref[0, 0, :, jb * tq:(jb + 1) * tq]
            cc_scr[:, jb * tq:(jb + 1) * tq] = _dot_exact_rhs(x, triu_ref[...]) + carry
            carry = carry + jnp.sum(x, axis=1, keepdims=True)

    q = q_ref[...]
    lane = lax.broadcasted_iota(I32, (tq, LANES), 1)
    qh = [jnp.where(lane < FOX_HD, q, jnp.zeros_like(q)), jnp.where(lane >= FOX_HD, q, jnp.zeros_like(q))]

    def kv_step(state, kb, vb, crows, mask):
        hs = range(2)
        s = [_dot_nt(qh[h], kb) * (FOX_SCALE * LOG2E) - crows[h] * LOG2E for h in hs]
        if mask is not None:
            s = [jnp.where(mask, s[h], -jnp.inf) for h in hs]
        m_new = [jnp.maximum(state[h][0], jnp.max(s[h], axis=1, keepdims=True)) for h in hs]
        pr = [jnp.exp2(s[h] - m_new[h]) for h in hs]
        alpha = [jnp.exp2(state[h][0] - m_new[h]) for h in hs]
        l = [alpha[h] * state[h][1] + jnp.sum(pr[h], axis=1, keepdims=True) for h in hs]
        pv = [_dot(pr[h].astype(BF16), vb) for h in hs]
        return tuple((m_new[h], l[h], alpha[h] * state[h][2] + pv[h]) for h in hs)

    init = tuple((jnp.full((tq, 1), -jnp.inf, F32), jnp.zeros((tq, 1), F32), jnp.zeros((tq, LANES), F32))
                 for _ in range(2))
    state = init
    if has_past:
        def past_body(j, st):
            off = pl.multiple_of(j * pblk, pblk)
            kb = kp_ref[0, pl.ds(off, pblk), :].astype(BF16)
            vb = vp_ref[0, pl.ds(off, pblk), :].astype(BF16)
            crows = [cp_scr[h:h + 1, pl.ds(off, pblk)] for h in range(2)]
            return kv_step(st, kb, vb, crows, None)
        state = lax.fori_loop(0, past // pblk, past_body, state)
    if nq > 1:
        def cur_body(j, st):
            off = pl.multiple_of(j * tq, tq)
            crows = [cc_scr[h:h + 1, pl.ds(off, tq)] for h in range(2)]
            return kv_step(st, kc_ref[pl.ds(off, tq), :], vc_ref[pl.ds(off, tq), :], crows, None)
        state = lax.fori_loop(0, i, cur_body, state)
        off = pl.multiple_of(i * tq, tq)
    else:
        off = 0
    causal = lax.broadcasted_iota(I32, (tq, tq), 1) <= lax.broadcasted_iota(I32, (tq, tq), 0)
    crows = [cc_scr[h:h + 1, pl.ds(off, tq)] for h in range(2)]
    state = kv_step(state, kc_ref[pl.ds(off, tq), :], vc_ref[pl.ds(off, tq), :], crows, causal)
    o0 = state[0][2] / state[0][1]
    o1 = state[1][2] / state[1][1]
    o_ref[...] = jnp.where(lane < FOX_HD, o0, o1).astype(BF16)


def _fox(z, lf_cur_t, *, batch, seq, row0, k_past=None, v_past=None, lf_past_t=None):
    has_past = k_past is not None
    past = k_past.shape[1] if has_past else 0
    tq = min(seq, FOX_BLOCK)
    nq = seq // tq
    qblk0 = row0 // tq
    sblk0 = row0 // seq
    triu = jnp.triu(jnp.ones((tq, tq), F32)).astype(BF16)
    in_specs = [pl.BlockSpec((tq, LANES), lambda b, p, i: (qblk0 + b * nq + i, ZC_FQ + p)),
                pl.BlockSpec((seq, LANES), lambda b, p, i: (sblk0 + b, ZC_FK + p)),
                pl.BlockSpec((seq, LANES), lambda b, p, i: (sblk0 + b, ZC_FV + p)),
                pl.BlockSpec((1, 1, 2, seq), lambda b, p, i: (b, p, 0, 0)),
                _const_spec((tq, tq))]
    args = [z, z, z, lf_cur_t, triu]
    scratch = [pltpu.VMEM((2, seq), F32)]
    if has_past:
        triup = jnp.triu(jnp.ones((FOX_BLOCK, FOX_BLOCK), F32)).astype(BF16)
        in_specs += [pl.BlockSpec((1, past, LANES), lambda b, p, i: (b, 0, p)),
                     pl.BlockSpec((1, past, LANES), lambda b, p, i: (b, 0, p)),
                     pl.BlockSpec((1, 1, 2, past), lambda b, p, i: (b, p, 0, 0)),
                     _const_spec((FOX_BLOCK, FOX_BLOCK))]
        args += [k_past, v_past, lf_past_t, triup]
        scratch.append(pltpu.VMEM((2, past), F32))
    return pl.pallas_call(
        functools.partial(_fox_kernel, seq=seq, past=past, tq=tq, has_past=has_past),
        grid=(batch, FOX_HEADS // 2, nq),
        in_specs=in_specs,
        out_specs=pl.BlockSpec((tq, LANES), lambda b, p, i: (b * nq + i, p)),
        out_shape=jax.ShapeDtypeStruct((batch * seq, 512), BF16),
        scratch_shapes=scratch,
        compiler_params=_params(("parallel", "parallel", "arbitrary")),
    )(*args)


def _outproj_kernel(og_ref, of_ref, x_ref, w_ref, g2_ref, xo_ref, h2_ref):
    mixed = _dot(og_ref[...], w_ref[0:512, :]) + _dot(of_ref[...], w_ref[512:1024, :])
    x = x_ref[...] + mixed
    xo_ref[...] = x
    ms = jnp.mean(x * x, axis=-1, keepdims=True)
    h2_ref[...] = (x * lax.rsqrt(ms + EPS) * g2_ref[...]).astype(BF16)


def _outproj(og, of, x, w, g2):
    n = x.shape[0]
    tm = _tile(n, 512)
    row = lambda i: (i, 0)
    return pl.pallas_call(
        _outproj_kernel,
        grid=(n // tm,),
        in_specs=[pl.BlockSpec((tm, 512), row), pl.BlockSpec((tm, 512), row),
                  pl.BlockSpec((tm, D_MODEL), row), _const_spec(w.shape), _const_spec(g2.shape)],
        out_specs=[pl.BlockSpec((tm, D_MODEL), row), pl.BlockSpec((tm, D_MODEL), row)],
        out_shape=[jax.ShapeDtypeStruct((n, D_MODEL), F32), jax.ShapeDtypeStruct((n, D_MODEL), BF16)],
        compiler_params=_params(("parallel",)),
    )(og, of, x, w, g2)


def _extract_max(s, iota):
    m = jnp.max(s, axis=0, keepdims=True)
    idx = jnp.min(jnp.where(s == m, iota, s.shape[0]), axis=0, keepdims=True)
    return m, idx, jnp.where(iota == idx, -jnp.inf, s)


def _topk_rows(s, k):
    iota = lax.broadcasted_iota(I32, s.shape, 0)
    vals, idxs = [], []
    for _ in range(k):
        m, idx, s = _extract_max(s, iota)
        vals.append(m)
        idxs.append(idx)
    return jnp.concatenate(vals, axis=0), jnp.concatenate(idxs, axis=0)


def _merge_exchange_pairs(n):
    pairs = []
    p = 1
    while p < n:
        k = p
        while k >= 1:
            for j in range(k % p, n - k, 2 * k):
                for i in range(min(k, n - j - k)):
                    if (i + j) // (2 * p) == (i + j + k) // (2 * p):
                        pairs.append((i + j, i + j + k))
            k //= 2
        p *= 2
    return tuple(pairs)


def _topk_keys(s, k):
    nrows, tn = s.shape
    groups = nrows // SUBLANES
    sub = lax.broadcasted_iota(I32, (SUBLANES, tn), 0)
    vals = [s[SUBLANES * r:SUBLANES * (r + 1), :] for r in range(groups)]
    idxs = [sub + SUBLANES * r for r in range(groups)]
    for a, b in _merge_exchange_pairs(groups):
        va, vb, ia, ib = vals[a], vals[b], idxs[a], idxs[b]
        a_first = (va > vb) | ((va == vb) & (ia < ib))
        vals[a], vals[b] = jnp.maximum(va, vb), jnp.minimum(va, vb)
        idxs[a], idxs[b] = jnp.where(a_first, ia, ib), jnp.where(a_first, ib, ia)
    out_v, out_i = [], []
    for t in range(k):
        head_v, head_i = vals[0], idxs[0]
        m = jnp.max(head_v, axis=0, keepdims=True)
        idx = jnp.min(jnp.where(head_v == m, head_i, nrows), axis=0, keepdims=True)
        out_v.append(m)
        out_i.append(idx)
        won = head_i == idx
        for r in range(k - 1 - t):
            vals[r] = jnp.where(won, vals[r + 1], vals[r])
            idxs[r] = jnp.where(won, idxs[r + 1], idxs[r])
    return jnp.concatenate(out_v, axis=0), jnp.concatenate(out_i, axis=0)


def _pair_topk(v1, i1, v2, i2):
    tn = v1.shape[1]
    iota_k = lax.broadcasted_iota(I32, (PEER_TOPK, tn), 0)
    iota_8 = lax.broadcasted_iota(I32, (8, tn), 0)

    def pick(table, sel):
        rows = []
        for r in range(PEER_TOPK):
            rows.append(jnp.sum(jnp.where(iota_k == sel[r:r + 1, :], table, 0), axis=0, keepdims=True))
        return jnp.concatenate(rows, axis=0)

    blocks = [v1[0:1, :] + v2]
    for a in range(1, 8):
        blocks.append(jnp.where(iota_8 < PEER_TOPK // (a + 1), v1[a:a + 1, :] + v2[0:8, :], -jnp.inf))
    blocks.append(v1[8:16, :] + v2[0:1, :])
    top, ti = _topk_rows(jnp.concatenate(blocks, axis=0), PEER_TOPK)
    mid = ti - PEER_TOPK
    e1 = pick(i1, jnp.where(ti < 16, 0, jnp.where(ti < 72, (mid >> 3) + 1, ti - 64)))
    e2 = pick(i2, jnp.where(ti < 16, ti, jnp.where(ti < 72, mid & 7, 0)))
    ex = jnp.exp(top - jnp.max(top, axis=0, keepdims=True))
    return e1, e2, ex / jnp.sum(ex, axis=0, keepdims=True)


PEER_ROWS = 16
PEER_EC = PEER_ROWS * PEER_NKEYS
assert PEER_NEXP // PEER_EC == PEER_HEADS


def _gelu(a):
    return 0.5 * a * (1.0 + lax.erf(a * (2.0 ** -0.5)))


def _route_act_kernel(hn_ref, hp_ref, wq_ref, sk_ref, ut_ref, e1_ref, e2_ref, c_ref,
                      q_scr, r1_scr, r2_scr, rg_scr, t1_scr, t2_scr, tg_scr, acc_scr):
    i = pl.program_id(0)
    c = pl.program_id(1)
    slot_new = i % 2
    slot_old = 1 - slot_new

    @pl.when((i == 0) & (c == 0))
    def _():
        t1_scr[1] = jnp.zeros_like(t1_scr[1])
        t2_scr[1] = jnp.zeros_like(t2_scr[1])
        tg_scr[1] = jnp.zeros_like(tg_scr[1])

    @pl.when(c == 0)
    def _():
        acc_scr[...] = jnp.zeros_like(acc_scr)
        q = _dot(hn_ref[...], wq_ref[...])
        for hc in range(2 * PEER_HEADS):
            q_scr[hc] = q[:, hc * LANES:(hc + 1) * LANES].astype(BF16)

    v1, i1 = _topk_keys(_dot_nt(sk_ref[2 * c], q_scr[2 * c]), PEER_TOPK)
    v2, i2 = _topk_keys(_dot_nt(sk_ref[2 * c + 1], q_scr[2 * c + 1]), PEER_TOPK)
    e1h, e2h, gh = _pair_topk(v1, i1, v2, i2)
    rows = pl.ds(pl.multiple_of(c * PEER_TOPK, PEER_TOPK), PEER_TOPK)
    r1_scr[rows, :] = e1h
    r2_scr[rows, :] = e2h
    rg_scr[rows, :] = gh

    a_all = _dot(hp_ref[...], ut_ref[...])
    e1 = t1_scr[slot_old]
    e2 = t2_scr[slot_old]
    acc = acc_scr[...]
    for r in range(PEER_ROWS):
        picked = jnp.take_along_axis(a_all[:, r * LANES:(r + 1) * LANES], e2, axis=1,
                                     mode="promise_in_bounds")
        acc = jnp.where(e1 == c * PEER_ROWS + r, picked, acc)
    acc_scr[...] = acc

    @pl.when(c == pl.num_programs(1) - 1)
    def _():
        c_ref[...] = tg_scr[slot_old] * _gelu(acc)
        t1_scr[slot_new] = r1_scr[...].T
        t2_scr[slot_new] = r2_scr[...].T
        tg_scr[slot_new] = rg_scr[...].T
        e1_ref[...] = t1_scr[slot_new]
        e2_ref[...] = t2_scr[slot_new]


def _route_act(h2, wq, sk, ut):
    n = h2.shape[0]
    tn = _tile(n, 512)
    tiles = n // tn
    new = lambda i, c: (jnp.minimum(i, tiles - 1), 0)
    old = lambda i, c: (jnp.maximum(i - 1, 0), 0)
    return pl.pallas_call(
        _route_act_kernel,
        grid=(tiles + 1, PEER_HEADS),
        in_specs=[pl.BlockSpec((tn, D_MODEL), new), pl.BlockSpec((tn, D_MODEL), old),
                  _const_spec(wq.shape), _const_spec(sk.shape),
                  pl.BlockSpec((D_MODEL, PEER_EC), lambda i, c: (0, c))],
        out_specs=[pl.BlockSpec((tn, PEER_J), new), pl.BlockSpec((tn, PEER_J), new),
                   pl.BlockSpec((tn, PEER_J), old)],
        out_shape=[jax.ShapeDtypeStruct((n, PEER_J), I32), jax.ShapeDtypeStruct((n, PEER_J), I32),
                   jax.ShapeDtypeStruct((n, PEER_J), F32)],
        scratch_shapes=[pltpu.VMEM((2 * PEER_HEADS, tn, LANES), BF16),
                        pltpu.VMEM((PEER_J, tn), I32), pltpu.VMEM((PEER_J, tn), I32),
                        pltpu.VMEM((PEER_J, tn), F32),
                        pltpu.VMEM((2, tn, PEER_J), I32), pltpu.VMEM((2, tn, PEER_J), I32),
                        pltpu.VMEM((2, tn, PEER_J), F32), pltpu.VMEM((tn, PEER_J), F32)],
        compiler_params=_params(("arbitrary", "arbitrary")),
    )(h2, h2, wq, sk, ut)


PEER_HALF = PEER_ROWS // 2


def _peer_out_kernel(e1_ref, e2_ref, c_ref, v_ref, x_ref, o_ref, y_scr, acc_scr):
    tn = x_ref.shape[0]
    c = pl.program_id(1)
    nchunks = PEER_NEXP // PEER_EC

    @pl.when(c == 0)
    def _():
        acc_scr[...] = jnp.zeros_like(acc_scr)
        key = lax.broadcasted_iota(I32, (PEER_NKEYS, PEER_J), 0)

        def token(t, carry):
            row = pl.ds(t, 1)
            d = jnp.where(key == e1_ref[row, :], c_ref[row, :], 0.0).astype(BF16)
            w = jnp.where(key == e2_ref[row, :], 1.0, 0.0).astype(BF16)
            y = _dot_nt(d, w).astype(BF16).astype(F32)
            bits = lax.bitcast_convert_type(y, I32)
            base = pl.multiple_of(t * PEER_HALF, PEER_HALF)
            for cc in range(nchunks):
                hi = bits[cc * PEER_ROWS:cc * PEER_ROWS + PEER_HALF, :]
                lo = bits[cc * PEER_ROWS + PEER_HALF:(cc + 1) * PEER_ROWS, :]
                y_scr[cc, pl.ds(base, PEER_HALF), :] = hi | lax.shift_right_logical(lo, 16)
            return carry

        lax.fori_loop(0, tn, token, 0, unroll=32)

    his, los = [], []
    for i in range(PEER_HALF):
        words = y_scr[c, pl.ds(i, tn, stride=PEER_HALF), :]
        his.append(lax.bitcast_convert_type(words & -65536, F32).astype(BF16))
        los.append(lax.bitcast_convert_type(words << 16, F32).astype(BF16))
    acc_scr[...] += _dot(jnp.concatenate(his + los, axis=1), v_ref[...])

    @pl.when(c == nchunks - 1)
    def _():
        o_ref[...] = x_ref[...] + acc_scr[...]


def _peer_out(e1, e2, cj, vt, x):
    n = x.shape[0]
    tn = _tile(n, 512)
    row = lambda i, c: (i, 0)
    nchunks = PEER_NEXP // PEER_EC
    return pl.pallas_call(
        _peer_out_kernel,
        grid=(n // tn, nchunks),
        in_specs=[pl.BlockSpec((tn, PEER_J), row), pl.BlockSpec((tn, PEER_J), row),
                  pl.BlockSpec((tn, PEER_J), row), pl.BlockSpec((PEER_EC, D_MODEL), lambda i, c: (c, 0)),
                  pl.BlockSpec((tn, D_MODEL), row)],
        out_specs=pl.BlockSpec((tn, D_MODEL), row),
        out_shape=jax.ShapeDtypeStruct((n, D_MODEL), F32),
        scratch_shapes=[pltpu.VMEM((nchunks, tn * PEER_HALF, LANES), I32),
                        pltpu.VMEM((tn, D_MODEL), F32)],
        compiler_params=_params(("parallel", "arbitrary")),
    )(e1, e2, cj, vt, x)


def _pair_major(lf, batch, length):
    return lf.reshape(batch, length, FOX_HEADS // 2, 2).transpose(0, 2, 3, 1)


def _prep_weights(g_norm1, w_in, w_gla_a2, b_gla_a, g_gla_onorm, g_fox_qnorm, g_fox_knorm, b_fox_f, w_out,
                  g_norm2, w_peer_q, peer_subkeys, peer_u, peer_v):
    bounds = [0]
    for size in IN_SIZES:
        bounds.append(bounds[-1] + size)
    gq, gk, gv, gg, glr, fq, fk, fv, ff = [w_in[:, bounds[i]:bounds[i + 1]] for i in range(len(IN_SIZES))]
    pad = jnp.zeros((D_MODEL, Z_SMALL - FOX_HEADS - GLA_LOWRANK), w_in.dtype)
    w_cat = jnp.concatenate([gq, gk, gv, gg, fq, fk, fv, ff, glr, pad], axis=1).astype(BF16)
    wa2p = jnp.zeros((Z_SMALL, 256), F32).at[FOX_HEADS:FOX_HEADS + GLA_LOWRANK].set(w_gla_a2).astype(BF16)
    bsm = jnp.zeros((1, Z_SMALL), F32).at[0, :FOX_HEADS].set(b_fox_f)
    head_of_col = jnp.arange(FOX_HEADS * FOX_HD) // FOX_HD
    ind = (head_of_col[:, None] == jnp.arange(LANES)[None, :]).astype(BF16)
    return dict(
        inproj=(g_norm1[None], w_cat, wa2p, b_gla_a[None], bsm, jnp.tile(g_fox_qnorm, FOX_HEADS)[None],
                jnp.tile(g_fox_knorm, FOX_HEADS)[None], ind, ind.T),
        gon=g_gla_onorm[None], w_out=w_out.astype(BF16), g2=g_norm2[None], wq=w_peer_q.astype(BF16),
        sk=peer_subkeys.reshape(2 * PEER_HEADS, PEER_NKEYS, LANES).astype(BF16),
        ut=peer_u.T.astype(BF16), v=peer_v.astype(BF16))


def _path_layer(x, wts, layer, depth, stacked, batch, seq, s0, k_past=None, v_past=None, lf_past=None):
    z, la, k_st, v_st, lf_st = _inproj(x, wts["inproj"], layer, depth, stacked)
    og, state = _gla(z, la, s0, wts["gon"], batch=batch, seq=seq, row0=0)
    past_args = {}
    if k_past is not None:
        past = k_past.shape[1]
        past_args = dict(k_past=k_past.reshape(batch, past, FOX_HEADS * FOX_HD),
                         v_past=v_past.reshape(batch, past, FOX_HEADS * FOX_HD),
                         lf_past_t=lf_past.reshape(batch, past, FOX_HEADS // 2, 2).transpose(0, 2, 3, 1))
    of = _fox(z, _pair_major(lf_st[layer], batch, seq), batch=batch, seq=seq, row0=0, **past_args)
    x, h2 = _outproj(og, of, x, wts["w_out"], wts["g2"])
    e1, e2, cj = _route_act(h2, wts["wq"], wts["sk"], wts["ut"])
    x = _peer_out(e1, e2, cj, wts["v"], x)
    return x, (k_st, v_st, lf_st), state


def kernel(x_prompt, x_sample, cache_fox_k, cache_fox_v, cache_fox_logf, state_gla, g_norm1, w_in, w_gla_a2,
           b_gla_a, g_gla_onorm, g_fox_qnorm, g_fox_knorm, b_fox_f, w_out, g_norm2, w_peer_q, peer_subkeys,
           peer_u, peer_v):
    bp, tp, _ = x_prompt.shape
    bs, ts, _ = x_sample.shape
    depth = w_in.shape[0]
    xp = x_prompt.reshape(bp * tp, D_MODEL)
    xs = x_sample.reshape(bs * ts, D_MODEL)
    stacks = lambda n: tuple(jnp.zeros((depth, n, w), F32) for w in (512, 512, FOX_HEADS))
    stk_p, stk_s, st_p, st_s = stacks(bp * tp), stacks(bs * ts), [], []
    zero_state = jnp.zeros((bp, GLA_HEADS, GLA_DK, GLA_DV), F32)
    for l in range(depth):
        wts = _prep_weights(g_norm1[l], w_in[l], w_gla_a2[l], b_gla_a[l], g_gla_onorm[l], g_fox_qnorm[l],
                            g_fox_knorm[l], b_fox_f[l], w_out[l], g_norm2[l], w_peer_q[l], peer_subkeys[l],
                            peer_u[l], peer_v[l])
        xp, stk_p, state = _path_layer(xp, wts, l, depth, stk_p, bp, tp, zero_state)
        st_p.append(state)
        xs, stk_s, state = _path_layer(xs, wts, l, depth, stk_s, bs, ts, state_gla[l],
                                       cache_fox_k[l], cache_fox_v[l], cache_fox_logf[l])
        st_s.append(state)

    def unstack(stk, b, t):
        k, v, lf = stk
        return (k.reshape(depth, b, t, FOX_HEADS, FOX_HD), v.reshape(depth, b, t, FOX_HEADS, FOX_HD),
                lf.reshape(depth, b, t, FOX_HEADS))

    return (xp.reshape(bp, tp, D_MODEL), xs.reshape(bs, ts, D_MODEL),
            *unstack(stk_p, bp, tp), jnp.stack(st_p), *unstack(stk_s, bs, ts), jnp.stack(st_s))
```

```python
import functools

import jax
import jax.numpy as jnp
from jax import lax
from jax.experimental import pallas as pl
from jax.experimental.pallas import tpu as pltpu

F32 = jnp.float32
BF16 = jnp.bfloat16
I32 = jnp.int32

D_MODEL = 1024
EPS = 1e-6
GLA_CHUNK = 64
GLA_HEADS = 4
GLA_DK = 64
GLA_DV = 128
GLA_LOWRANK = 16
GLA_GATE_NORM = 16.0
GLA_SUB = 16
FOX_HEADS = 8
FOX_HD = 64
FOX_SCALE = FOX_HD ** -0.5
LOG2E = 1.4426950408889634
FOX_BLOCK = 512
IN_SIZES = (256, 256, 512, 512, GLA_LOWRANK, 512, 512, 512, FOX_HEADS)
PEER_HEADS = 8
PEER_NKEYS = 128
PEER_TOPK = 16
PEER_NEXP = PEER_NKEYS * PEER_NKEYS
PEER_J = PEER_HEADS * PEER_TOPK

LANES = 128
SUBLANES = 8
Z_MAIN = 3072
Z_SMALL = 128
VMEM_LIMIT = 56 * 1024 * 1024

ZC_GQ, ZC_GK, ZC_GV, ZC_GG, ZC_FQ, ZC_FK, ZC_FV = 0, 2, 4, 8, 12, 16, 20


def _params(sem):
    return pltpu.CompilerParams(dimension_semantics=sem, vmem_limit_bytes=VMEM_LIMIT)


def _tile(n, cap):
    t = cap
    while n % t:
        t //= 2
    return t


def _dot(a, b):
    return jnp.dot(a, b, preferred_element_type=F32)


def _dot_nt(a, b):
    return lax.dot_general(a, b, (((1,), (1,)), ((), ())), preferred_element_type=F32)


def _dot_tn(a, b):
    return lax.dot_general(a, b, (((0,), (0,)), ((), ())), preferred_element_type=F32)


def _split(x):
    hi = x.astype(BF16)
    lo = (x - hi.astype(F32)).astype(BF16)
    return hi, lo


def _dot_exact_rhs(x, m):
    hi, lo = _split(x)
    return _dot(hi, m) + _dot(lo, m)


def _dot_exact_lhs(m, x):
    hi, lo = _split(x)
    return _dot(m, hi) + _dot(m, lo)


def _log_sigmoid(y):
    return jnp.minimum(y, 0.0) - jnp.log(1.0 + jnp.exp(-jnp.abs(y)))


def _const_spec(shape):
    nd = len(shape)
    return pl.BlockSpec(shape, lambda *_: (0,) * nd)


def _inproj_kernel(*refs, n_alias):
    (x_ref, g1_ref, w_ref, wa2_ref, ba_ref, bsm_ref, gq_ref, gk_ref, ind_ref, indt_ref) = refs[:10]
    z_ref, la_ref, k32_ref, v32_ref, lf_ref = refs[10 + n_alias:]
    x = x_ref[...]
    ms = jnp.mean(x * x, axis=-1, keepdims=True)
    h = (x * lax.rsqrt(ms + EPS) * g1_ref[...]).astype(BF16)
    z = _dot(h, w_ref[...])

    def headnorm(t, gain):
        ss = _dot_exact_rhs(t * t, ind_ref[...])
        r = lax.rsqrt(ss * (1.0 / FOX_HD) + EPS)
        return t * _dot_exact_rhs(r, indt_ref[...]) * gain

    fq = headnorm(z[:, 1536:2048], gq_ref[...])
    fk = headnorm(z[:, 2048:2560], gk_ref[...])
    fv = z[:, 2560:3072]
    z_ref[:, 0:1536] = z[:, 0:1536].astype(BF16)
    z_ref[:, 1536:2048] = fq.astype(BF16)
    z_ref[:, 2048:2560] = fk.astype(BF16)
    z_ref[:, 2560:3072] = fv.astype(BF16)
    tm = x_ref.shape[0]
    for hd in range(FOX_HEADS):
        token_rows = pl.ds(hd, tm, stride=FOX_HEADS)
        k32_ref[0, token_rows, :] = fk[:, hd * FOX_HD:(hd + 1) * FOX_HD]
        v32_ref[0, token_rows, :] = fv[:, hd * FOX_HD:(hd + 1) * FOX_HD]
    small = z[:, Z_MAIN:Z_MAIN + Z_SMALL]
    lf_ref[0] = _log_sigmoid(small + bsm_ref[...])[:, 0:FOX_HEADS]
    y = _dot(small.astype(BF16), wa2_ref[...]) + ba_ref[...]
    la_ref[...] = _log_sigmoid(y) * (1.0 / GLA_GATE_NORM)


def _inproj(x, consts, layer, depth, stacked):
    n = x.shape[0]
    tm = _tile(n, 512)
    row = lambda i: (i, 0)
    slab = lambda i: (layer, i, 0)
    n_alias = len(stacked)
    kv_block = (1, tm * FOX_HEADS, FOX_HD)
    return pl.pallas_call(
        functools.partial(_inproj_kernel, n_alias=n_alias),
        grid=(n // tm,),
        in_specs=[pl.BlockSpec((tm, D_MODEL), row)] + [_const_spec(c.shape) for c in consts]
        + [pl.BlockSpec(memory_space=pl.ANY)] * n_alias,
        out_specs=[pl.BlockSpec((tm, Z_MAIN), row), pl.BlockSpec((tm, 256), row)]
        + [pl.BlockSpec(kv_block, slab), pl.BlockSpec(kv_block, slab), pl.BlockSpec((1, tm, FOX_HEADS), slab)],
        out_shape=[jax.ShapeDtypeStruct((n, Z_MAIN), BF16), jax.ShapeDtypeStruct((n, 256), F32)]
        + [jax.ShapeDtypeStruct(a.shape, F32) for a in stacked],
        input_output_aliases={1 + len(consts) + i: 2 + i for i in range(n_alias)},
        compiler_params=_params(("parallel",)),
    )(x, *consts, *stacked)


def _gla_kernel(q_ref, k_ref, v_ref, gg_ref, la_ref, s0_ref, gon_ref, tri_ref, o_ref, sfin_ref, st_scr,
                *, chunk, nchunk):
    t = pl.program_id(1)
    nsub = chunk // GLA_SUB

    @pl.when(t == 0)
    def _():
        for p in range(2):
            st_scr[p] = s0_ref[0, p].T

    rowi = lax.broadcasted_iota(I32, (chunk, LANES), 0)
    lane = lax.broadcasted_iota(I32, (chunk, LANES), 1)
    lane_st = lax.broadcasted_iota(I32, (chunk, nsub * LANES), 1)
    lane_sq = lax.broadcasted_iota(I32, (LANES, LANES), 1)
    arow = lax.broadcasted_iota(I32, (chunk, chunk), 0)
    acol = lax.broadcasted_iota(I32, (chunk, chunk), 1)

    per_trip = next(n for n in (4, 2, 1) if nchunk % n == 0)

    def body(trip, carry):
        rows_u, prep = [], []
        for u in range(per_trip):
            rows = pl.ds(pl.multiple_of((trip * per_trip + u) * chunk, chunk), chunk)
            rows_u.append(rows)
            bcum = _dot_exact_lhs(tri_ref[...], la_ref[rows, :])
            q = q_ref[rows, :].astype(F32) * (GLA_DK ** -0.5)
            k = k_ref[rows, :].astype(F32)
            for p in range(2):
                cols = slice(p * LANES, (p + 1) * LANES)
                bp, qp, kp = bcum[:, cols], q[:, cols], k[:, cols]
                blast = bp[chunk - 1:chunk, :]
                qs, ks = [], []
                for sb in range(nsub):
                    beta = bp[sb * GLA_SUB - 1:sb * GLA_SUB, :] if sb else jnp.zeros((1, LANES), F32)
                    inblk = (rowi >= sb * GLA_SUB) & (rowi < (sb + 1) * GLA_SUB)
                    qs.append(jnp.where(inblk, qp * jnp.exp(jnp.where(inblk, bp - beta, 0.0)), 0.0))
                    valid = rowi < (sb + 1) * GLA_SUB
                    ks.append(jnp.where(valid, kp * jnp.exp(jnp.where(valid, beta - bp, 0.0)), 0.0))
                prep.append(dict(qst=jnp.concatenate(qs, axis=1), kst=jnp.concatenate(ks, axis=1).astype(BF16),
                                 qinter=qp * jnp.exp(bp), kdec=(kp * jnp.exp(blast - bp)).astype(BF16),
                                 decay=jnp.exp(blast)))
        heads = [(u, p, i) for u in range(per_trip) for p in range(2) for i in range(2)]
        hcols = [slice((2 * p + i) * GLA_DV, (2 * p + i + 1) * GLA_DV) for _, p, i in heads]
        vhs = [v_ref[rows_u[u], hc] for (u, _, _), hc in zip(heads, hcols)]
        a_s = [_dot_nt(jnp.where((lane_st & GLA_DK) == i * GLA_DK, prep[2 * u + p]["qst"], 0.0).astype(BF16),
                       prep[2 * u + p]["kst"]) for u, p, i in heads]
        upd = [_dot_tn(vh, prep[2 * u + p]["kdec"]) for (u, p, i), vh in zip(heads, vhs)]
        a_s = [jnp.where(acol <= arow, a, 0.0).astype(BF16) for a in a_s]
        intra = [_dot(a, vh) for a, vh in zip(a_s, vhs)]
        states = [st_scr[p] for p in range(2)]
        for u in range(per_trip):
            for p in range(2):
                new = states[2 * u + p] * prep[2 * u + p]["decay"] + jnp.where(
                    lane_sq < GLA_DK, upd[4 * u + 2 * p], upd[4 * u + 2 * p + 1])
                states.append(new)
        for p in range(2):
            st_scr[p] = states[2 * per_trip + p]
        stb = [st.astype(BF16) for st in states[:2 * per_trip]]
        o_s = [o_in + _dot_nt(jnp.where((lane & GLA_DK) == i * GLA_DK, prep[2 * u + p]["qinter"], 0.0)
                              .astype(BF16), stb[2 * u + p]) for (u, p, i), o_in in zip(heads, intra)]
        for (u, _, _), o, hc in zip(heads, o_s, hcols):
            on = o * lax.rsqrt(jnp.mean(o * o, axis=-1, keepdims=True) + EPS) * gon_ref[...]
            gate = gg_ref[rows_u[u], hc].astype(F32)
            o_ref[rows_u[u], hc] = (on * gate * (1.0 / (1.0 + jnp.exp(-gate)))).astype(BF16)
        return carry

    lax.fori_loop(0, nchunk // per_trip, body, 0)

    @pl.when(t == pl.num_programs(1) - 1)
    def _():
        for p in range(2):
            sfin_ref[0, p] = st_scr[p].T


def _gla(z, la, s0, gon, *, batch, seq, row0):
    chunk = min(seq, GLA_CHUNK)
    tb = min(seq, 512)
    nt = seq // tb
    blk0 = row0 // tb
    tri = jnp.tril(jnp.ones((chunk, chunk), F32)).astype(BF16)
    rows = lambda col: (lambda b, t: (blk0 + b * nt + t, col))
    state_spec = pl.BlockSpec((1, 2, LANES, LANES), lambda b, t: (b, 0, 0, 0))
    o, sfin = pl.pallas_call(
        functools.partial(_gla_kernel, chunk=chunk, nchunk=tb // chunk),
        grid=(batch, nt),
        in_specs=[pl.BlockSpec((tb, 256), rows(0)), pl.BlockSpec((tb, 256), rows(1)),
                  pl.BlockSpec((tb, 512), rows(1)), pl.BlockSpec((tb, 512), rows(2)),
                  pl.BlockSpec((tb, 256), rows(0)), state_spec,
                  _const_spec((1, GLA_DV)), _const_spec((chunk, chunk))],
        out_specs=[pl.BlockSpec((tb, 512), lambda b, t: (b * nt + t, 0)), state_spec],
        out_shape=[jax.ShapeDtypeStruct((batch * seq, 512), BF16),
                   jax.ShapeDtypeStruct((batch, 2, LANES, LANES), F32)],
        scratch_shapes=[pltpu.VMEM((2, LANES, LANES), F32)],
        compiler_params=_params(("parallel", "arbitrary")),
    )(z, z, z, z, la, s0.reshape(batch, 2, LANES, LANES), gon, tri)
    return o, sfin.reshape(batch, GLA_HEADS, GLA_DK, GLA_DV)


def _fox_kernel(*refs, seq, past, tq, has_past):
    if has_past:
        (q_ref, kc_ref, vc_ref, lfc_ref, triu_ref, kp_ref, vp_ref, lfp_ref, triup_ref,
         o_ref, cc_scr, cp_scr) = refs
    else:
        q_ref, kc_ref, vc_ref, lfc_ref, triu_ref, o_ref, cc_scr = refs
    i = pl.program_id(2)
    nq = seq // tq
    pblk = FOX_BLOCK

    @pl.when(i == 0)
    def _():
        carry = jnp.zeros((2, 1), F32)
        if has_past:
            for jb in range(past // pblk):
                x = lfp_ref[0, 0, :, jb * pblk:(jb + 1) * pblk]
                cp_scr[:, jb * pblk:(jb + 1) * pblk] = _dot_exact_rhs(x, triup_ref[...]) + carry
                carry = carry + jnp.sum(x, axis=1, keepdims=True)
        for jb in range(nq):
            x = lfc_ref[0, 0, :, jb * tq:(jb + 1) * tq]
            cc_scr[:, jb * tq:(jb + 1) * tq] = _dot_exact_rhs(x, triu_ref[...]) + carry
            carry = carry + jnp.sum(x, axis=1, keepdims=True)

    q = q_ref[...]
    lane = lax.broadcasted_iota(I32, (tq, LANES), 1)
    qh = [jnp.where(lane < FOX_HD, q, jnp.zeros_like(q)), jnp.where(lane >= FOX_HD, q, jnp.zeros_like(q))]

    def kv_step(state, kb, vb, crows, mask):
        hs = range(2)
        lane_v = lax.broadcasted_iota(I32, vb.shape, 1)
        ones = jnp.ones_like(vb)
        vh = [jnp.where(lane_v < FOX_HD, vb, ones), jnp.where(lane_v >= FOX_HD, vb, ones)]
        s = [_dot_nt(qh[h], kb) * (FOX_SCALE * LOG2E) - crows[h] * LOG2E for h in hs]
        if mask is not None:
            s = [jnp.where(mask, s[h], -jnp.inf) for h in hs]
        m_new = [jnp.maximum(state[h][0], jnp.max(s[h], axis=1, keepdims=True)) for h in hs]
        pv = [_dot(jnp.exp2(s[h] - m_new[h]).astype(BF16), vh[h]) for h in hs]
        alpha = [jnp.exp2(state[h][0] - m_new[h]) for h in hs]
        return tuple((m_new[h], alpha[h] * state[h][1] + pv[h]) for h in hs)

    init = tuple((jnp.full((tq, 1), -jnp.inf, F32), jnp.zeros((tq, LANES), F32)) for _ in range(2))
    state = init
    if has_past:
        def past_body(j, st):
            off = pl.multiple_of(j * pblk, pblk)
            kb = kp_ref[0, pl.ds(off, pblk), :].astype(BF16)
            vb = vp_ref[0, pl.ds(off, pblk), :].astype(BF16)
            crows = [cp_scr[h:h + 1, pl.ds(off, pblk)] for h in range(2)]
            return kv_step(st, kb, vb, crows, None)
        state = lax.fori_loop(0, past // pblk, past_body, state)
    if nq > 1:
        def cur_body(j, st):
            off = pl.multiple_of(j * tq, tq)
            crows = [cc_scr[h:h + 1, pl.ds(off, tq)] for h in range(2)]
            return kv_step(st, kc_ref[pl.ds(off, tq), :], vc_ref[pl.ds(off, tq), :], crows, None)
        state = lax.fori_loop(0, i, cur_body, state)
        off = pl.multiple_of(i * tq, tq)
    else:
        off = 0
    causal = lax.broadcasted_iota(I32, (tq, tq), 1) <= lax.broadcasted_iota(I32, (tq, tq), 0)
    crows = [cc_scr[h:h + 1, pl.ds(off, tq)] for h in range(2)]
    state = kv_step(state, kc_ref[pl.ds(off, tq), :], vc_ref[pl.ds(off, tq), :], crows, causal)
    acc = jnp.where(lane < FOX_HD, state[0][1], state[1][1])
    den = jnp.where(lane < FOX_HD, pltpu.roll(state[0][1], FOX_HD, 1), pltpu.roll(state[1][1], FOX_HD, 1))
    o_ref[...] = (acc / den).astype(BF16)


def _fox(z, lf_cur_t, *, batch, seq, row0, k_past=None, v_past=None, lf_past_t=None):
    has_past = k_past is not None
    past = k_past.shape[1] if has_past else 0
    tq = min(seq, FOX_BLOCK)
    nq = seq // tq
    qblk0 = row0 // tq
    sblk0 = row0 // seq
    triu = jnp.triu(jnp.ones((tq, tq), F32)).astype(BF16)
    in_specs = [pl.BlockSpec((tq, LANES), lambda b, p, i: (qblk0 + b * nq + i, ZC_FQ + p)),
                pl.BlockSpec((seq, LANES), lambda b, p, i: (sblk0 + b, ZC_FK + p)),
                pl.BlockSpec((seq, LANES), lambda b, p, i: (sblk0 + b, ZC_FV + p)),
                pl.BlockSpec((1, 1, 2, seq), lambda b, p, i: (b, p, 0, 0)),
                _const_spec((tq, tq))]
    args = [z, z, z, lf_cur_t, triu]
    scratch = [pltpu.VMEM((2, seq), F32)]
    if has_past:
        triup = jnp.triu(jnp.ones((FOX_BLOCK, FOX_BLOCK), F32)).astype(BF16)
        in_specs += [pl.BlockSpec((1, past, LANES), lambda b, p, i: (b, 0, p)),
                     pl.BlockSpec((1, past, LANES), lambda b, p, i: (b, 0, p)),
                     pl.BlockSpec((1, 1, 2, past), lambda b, p, i: (b, p, 0, 0)),
                     _const_spec((FOX_BLOCK, FOX_BLOCK))]
        args += [k_past, v_past, lf_past_t, triup]
        scratch.append(pltpu.VMEM((2, past), F32))
    return pl.pallas_call(
        functools.partial(_fox_kernel, seq=seq, past=past, tq=tq, has_past=has_past),
        grid=(batch, FOX_HEADS // 2, nq),
        in_specs=in_specs,
        out_specs=pl.BlockSpec((tq, LANES), lambda b, p, i: (b * nq + i, p)),
        out_shape=jax.ShapeDtypeStruct((batch * seq, 512), BF16),
        scratch_shapes=scratch,
        compiler_params=_params(("parallel", "parallel", "arbitrary")),
    )(*args)


def _outproj_kernel(og_ref, of_ref, x_ref, w_ref, g2_ref, xo_ref, h2_ref):
    mixed = _dot(og_ref[...], w_ref[0:512, :]) + _dot(of_ref[...], w_ref[512:1024, :])
    x = x_ref[...] + mixed
    xo_ref[...] = x
    ms = jnp.mean(x * x, axis=-1, keepdims=True)
    h2_ref[...] = (x * lax.rsqrt(ms + EPS) * g2_ref[...]).astype(BF16)


def _outproj(og, of, x, w, g2):
    n = x.shape[0]
    tm = _tile(n, 512)
    row = lambda i: (i, 0)
    return pl.pallas_call(
        _outproj_kernel,
        grid=(n // tm,),
        in_specs=[pl.BlockSpec((tm, 512), row), pl.BlockSpec((tm, 512), row),
                  pl.BlockSpec((tm, D_MODEL), row), _const_spec(w.shape), _const_spec(g2.shape)],
        out_specs=[pl.BlockSpec((tm, D_MODEL), row), pl.BlockSpec((tm, D_MODEL), row)],
        out_shape=[jax.ShapeDtypeStruct((n, D_MODEL), F32), jax.ShapeDtypeStruct((n, D_MODEL), BF16)],
        compiler_params=_params(("parallel",)),
    )(og, of, x, w, g2)


def _extract_max(s, iota):
    m = jnp.max(s, axis=0, keepdims=True)
    idx = jnp.min(jnp.where(s == m, iota, s.shape[0]), axis=0, keepdims=True)
    return m, idx, jnp.where(iota == idx, -jnp.inf, s)


def _topk_rows(s, k):
    iota = lax.broadcasted_iota(I32, s.shape, 0)
    vals, idxs = [], []
    for _ in range(k):
        m, idx, s = _extract_max(s, iota)
        vals.append(m)
        idxs.append(idx)
    return jnp.concatenate(vals, axis=0), jnp.concatenate(idxs, axis=0)


def _merge_exchange_pairs(n):
    pairs = []
    p = 1
    while p < n:
        k = p
        while k >= 1:
            for j in range(k % p, n - k, 2 * k):
                for i in range(min(k, n - j - k)):
                    if (i + j) // (2 * p) == (i + j + k) // (2 * p):
                        pairs.append((i + j, i + j + k))
            k //= 2
        p *= 2
    return tuple(pairs)


def _topk_keys(s, k):
    nrows, tn = s.shape
    groups = nrows // SUBLANES
    sub = lax.broadcasted_iota(I32, (SUBLANES, tn), 0)
    vals = [s[SUBLANES * r:SUBLANES * (r + 1), :] for r in range(groups)]
    idxs = [sub + SUBLANES * r for r in range(groups)]
    for a, b in _merge_exchange_pairs(groups):
        va, vb, ia, ib = vals[a], vals[b], idxs[a], idxs[b]
        a_first = (va > vb) | ((va == vb) & (ia < ib))
        vals[a], vals[b] = jnp.maximum(va, vb), jnp.minimum(va, vb)
        idxs[a], idxs[b] = jnp.where(a_first, ia, ib), jnp.where(a_first, ib, ia)
    out_v, out_i = [], []
    for t in range(k):
        head_v, head_i = vals[0], idxs[0]
        m = jnp.max(head_v, axis=0, keepdims=True)
        idx = jnp.min(jnp.where(head_v == m, head_i, nrows), axis=0, keepdims=True)
        out_v.append(m)
        out_i.append(idx)
        won = head_i == idx
        for r in range(k - 1 - t):
            vals[r] = jnp.where(won, vals[r + 1], vals[r])
            idxs[r] = jnp.where(won, idxs[r + 1], idxs[r])
    return jnp.concatenate(out_v, axis=0), jnp.concatenate(out_i, axis=0)


def _pair_topk(v1, i1, v2, i2):
    tn = v1.shape[1]
    iota_k = lax.broadcasted_iota(I32, (PEER_TOPK, tn), 0)
    iota_8 = lax.broadcasted_iota(I32, (8, tn), 0)

    def pick(table, sel):
        rows = []
        for r in range(PEER_TOPK):
            rows.append(jnp.sum(jnp.where(iota_k == sel[r:r + 1, :], table, 0), axis=0, keepdims=True))
        return jnp.concatenate(rows, axis=0)

    blocks = [v1[0:1, :] + v2]
    for a in range(1, 8):
        blocks.append(jnp.where(iota_8 < PEER_TOPK // (a + 1), v1[a:a + 1, :] + v2[0:8, :], -jnp.inf))
    blocks.append(v1[8:16, :] + v2[0:1, :])
    top, ti = _topk_rows(jnp.concatenate(blocks, axis=0), PEER_TOPK)
    mid = ti - PEER_TOPK
    e1 = pick(i1, jnp.where(ti < 16, 0, jnp.where(ti < 72, (mid >> 3) + 1, ti - 64)))
    e2 = pick(i2, jnp.where(ti < 16, ti, jnp.where(ti < 72, mid & 7, 0)))
    ex = jnp.exp(top - jnp.max(top, axis=0, keepdims=True))
    return e1, e2, ex / jnp.sum(ex, axis=0, keepdims=True)


PEER_ROWS = 16
PEER_EC = PEER_ROWS * PEER_NKEYS
assert PEER_NEXP // PEER_EC == PEER_HEADS


def _gelu(a):
    return 0.5 * a * (1.0 + lax.erf(a * (2.0 ** -0.5)))


def _route_act_kernel(hn_ref, hp_ref, wq_ref, sk_ref, ut_ref, e1_ref, e2_ref, c_ref,
                      q_scr, r1_scr, r2_scr, rg_scr, t1_scr, t2_scr, tg_scr, acc_scr):
    i = pl.program_id(0)
    c = pl.program_id(1)
    slot_new = i % 2
    slot_old = 1 - slot_new

    @pl.when((i == 0) & (c == 0))
    def _():
        t1_scr[1] = jnp.zeros_like(t1_scr[1])
        t2_scr[1] = jnp.zeros_like(t2_scr[1])
        tg_scr[1] = jnp.zeros_like(tg_scr[1])

    @pl.when(c == 0)
    def _():
        acc_scr[...] = jnp.zeros_like(acc_scr)
        q = _dot(hn_ref[...], wq_ref[...])
        for hc in range(2 * PEER_HEADS):
            q_scr[hc] = q[:, hc * LANES:(hc + 1) * LANES].astype(BF16)

    v1, i1 = _topk_keys(_dot_nt(sk_ref[2 * c], q_scr[2 * c]), PEER_TOPK)
    v2, i2 = _topk_keys(_dot_nt(sk_ref[2 * c + 1], q_scr[2 * c + 1]), PEER_TOPK)
    e1h, e2h, gh = _pair_topk(v1, i1, v2, i2)
    rows = pl.ds(pl.multiple_of(c * PEER_TOPK, PEER_TOPK), PEER_TOPK)
    r1_scr[rows, :] = e1h
    r2_scr[rows, :] = e2h
    rg_scr[rows, :] = gh

    a_all = _dot(hp_ref[...], ut_ref[...])
    e1 = t1_scr[slot_old]
    e2 = t2_scr[slot_old]
    acc = acc_scr[...]
    for r in range(PEER_ROWS):
        picked = jnp.take_along_axis(a_all[:, r * LANES:(r + 1) * LANES], e2, axis=1,
                                     mode="promise_in_bounds")
        acc = jnp.where(e1 == c * PEER_ROWS + r, picked, acc)
    acc_scr[...] = acc

    @pl.when(c == pl.num_programs(1) - 1)
    def _():
        c_ref[...] = tg_scr[slot_old] * _gelu(acc)
        t1_scr[slot_new] = r1_scr[...].T
        t2_scr[slot_new] = r2_scr[...].T
        tg_scr[slot_new] = rg_scr[...].T
        e1_ref[...] = t1_scr[slot_new]
        e2_ref[...] = t2_scr[slot_new]


def _route_act(h2, wq, sk, ut):
    n = h2.shape[0]
    tn = _tile(n, 512)
    tiles = n // tn
    new = lambda i, c: (jnp.minimum(i, tiles - 1), 0)
    old = lambda i, c: (jnp.maximum(i - 1, 0), 0)
    return pl.pallas_call(
        _route_act_kernel,
        grid=(tiles + 1, PEER_HEADS),
        in_specs=[pl.BlockSpec((tn, D_MODEL), new), pl.BlockSpec((tn, D_MODEL), old),
                  _const_spec(wq.shape), _const_spec(sk.shape),
                  pl.BlockSpec((D_MODEL, PEER_EC), lambda i, c: (0, c))],
        out_specs=[pl.BlockSpec((tn, PEER_J), new), pl.BlockSpec((tn, PEER_J), new),
                   pl.BlockSpec((tn, PEER_J), old)],
        out_shape=[jax.ShapeDtypeStruct((n, PEER_J), I32), jax.ShapeDtypeStruct((n, PEER_J), I32),
                   jax.ShapeDtypeStruct((n, PEER_J), F32)],
        scratch_shapes=[pltpu.VMEM((2 * PEER_HEADS, tn, LANES), BF16),
                        pltpu.VMEM((PEER_J, tn), I32), pltpu.VMEM((PEER_J, tn), I32),
                        pltpu.VMEM((PEER_J, tn), F32),
                        pltpu.VMEM((2, tn, PEER_J), I32), pltpu.VMEM((2, tn, PEER_J), I32),
                        pltpu.VMEM((2, tn, PEER_J), F32), pltpu.VMEM((tn, PEER_J), F32)],
        compiler_params=_params(("arbitrary", "arbitrary")),
    )(h2, h2, wq, sk, ut)


PEER_HALF = PEER_ROWS // 2


def _peer_out_kernel(e1_ref, e2_ref, c_ref, v_ref, x_ref, o_ref, y_scr, acc_scr):
    tn = x_ref.shape[0]
    c = pl.program_id(1)
    nchunks = PEER_NEXP // PEER_EC

    @pl.when(c == 0)
    def _():
        acc_scr[...] = jnp.zeros_like(acc_scr)
        key = lax.broadcasted_iota(I32, (PEER_NKEYS, PEER_J), 0)

        def token(t, carry):
            row = pl.ds(t, 1)
            d = jnp.where(key == e1_ref[row, :], c_ref[row, :], 0.0).astype(BF16)
            w = jnp.where(key == e2_ref[row, :], 1.0, 0.0).astype(BF16)
            y = _dot_nt(d, w).astype(BF16).astype(F32)
            bits = lax.bitcast_convert_type(y, I32)
            base = pl.multiple_of(t * PEER_HALF, PEER_HALF)
            for cc in range(nchunks):
                hi = bits[cc * PEER_ROWS:cc * PEER_ROWS + PEER_HALF, :]
                lo = bits[cc * PEER_ROWS + PEER_HALF:(cc + 1) * PEER_ROWS, :]
                y_scr[cc, pl.ds(base, PEER_HALF), :] = hi | lax.shift_right_logical(lo, 16)
            return carry

        lax.fori_loop(0, tn, token, 0, unroll=32)

    his, los = [], []
    for i in range(PEER_HALF):
        words = y_scr[c, pl.ds(i, tn, stride=PEER_HALF), :]
        his.append(lax.bitcast_convert_type(words & -65536, F32).astype(BF16))
        los.append(lax.bitcast_convert_type(words << 16, F32).astype(BF16))
    acc_scr[...] += _dot(jnp.concatenate(his + los, axis=1), v_ref[...])

    @pl.when(c == nchunks - 1)
    def _():
        o_ref[...] = x_ref[...] + acc_scr[...]


def _peer_out(e1, e2, cj, vt, x):
    n = x.shape[0]
    tn = _tile(n, 512)
    row = lambda i, c: (i, 0)
    nchunks = PEER_NEXP // PEER_EC
    return pl.pallas_call(
        _peer_out_kernel,
        grid=(n // tn, nchunks),
        in_specs=[pl.BlockSpec((tn, PEER_J), row), pl.BlockSpec((tn, PEER_J), row),
                  pl.BlockSpec((tn, PEER_J), row), pl.BlockSpec((PEER_EC, D_MODEL), lambda i, c: (c, 0)),
                  pl.BlockSpec((tn, D_MODEL), row)],
        out_specs=pl.BlockSpec((tn, D_MODEL), row),
        out_shape=jax.ShapeDtypeStruct((n, D_MODEL), F32),
        scratch_shapes=[pltpu.VMEM((nchunks, tn * PEER_HALF, LANES), I32),
                        pltpu.VMEM((tn, D_MODEL), F32)],
        compiler_params=_params(("parallel", "arbitrary")),
    )(e1, e2, cj, vt, x)


def _pair_major(lf, batch, length):
    return lf.reshape(batch, length, FOX_HEADS // 2, 2).transpose(0, 2, 3, 1)


def _prep_weights(g_norm1, w_in, w_gla_a2, b_gla_a, g_gla_onorm, g_fox_qnorm, g_fox_knorm, b_fox_f, w_out,
                  g_norm2, w_peer_q, peer_subkeys, peer_u, peer_v):
    bounds = [0]
    for size in IN_SIZES:
        bounds.append(bounds[-1] + size)
    gq, gk, gv, gg, glr, fq, fk, fv, ff = [w_in[:, bounds[i]:bounds[i + 1]] for i in range(len(IN_SIZES))]
    pad = jnp.zeros((D_MODEL, Z_SMALL - FOX_HEADS - GLA_LOWRANK), w_in.dtype)
    w_cat = jnp.concatenate([gq, gk, gv, gg, fq, fk, fv, ff, glr, pad], axis=1).astype(BF16)
    wa2p = jnp.zeros((Z_SMALL, 256), F32).at[FOX_HEADS:FOX_HEADS + GLA_LOWRANK].set(w_gla_a2).astype(BF16)
    bsm = jnp.zeros((1, Z_SMALL), F32).at[0, :FOX_HEADS].set(b_fox_f)
    head_of_col = jnp.arange(FOX_HEADS * FOX_HD) // FOX_HD
    ind = (head_of_col[:, None] == jnp.arange(LANES)[None, :]).astype(BF16)
    return dict(
        inproj=(g_norm1[None], w_cat, wa2p, b_gla_a[None], bsm, jnp.tile(g_fox_qnorm, FOX_HEADS)[None],
                jnp.tile(g_fox_knorm, FOX_HEADS)[None], ind, ind.T),
        gon=g_gla_onorm[None], w_out=w_out.astype(BF16), g2=g_norm2[None], wq=w_peer_q.astype(BF16),
        sk=peer_subkeys.reshape(2 * PEER_HEADS, PEER_NKEYS, LANES).astype(BF16),
        ut=peer_u.T.astype(BF16), v=peer_v.astype(BF16))


def _path_layer(x, wts, layer, depth, stacked, batch, seq, s0, k_past=None, v_past=None, lf_past=None):
    z, la, k_st, v_st, lf_st = _inproj(x, wts["inproj"], layer, depth, stacked)
    og, state = _gla(z, la, s0, wts["gon"], batch=batch, seq=seq, row0=0)
    past_args = {}
    if k_past is not None:
        past = k_past.shape[1]
        past_args = dict(k_past=k_past.reshape(batch, past, FOX_HEADS * FOX_HD),
                         v_past=v_past.reshape(batch, past, FOX_HEADS * FOX_HD),
                         lf_past_t=lf_past.reshape(batch, past, FOX_HEADS // 2, 2).transpose(0, 2, 3, 1))
    of = _fox(z, _pair_major(lf_st[layer], batch, seq), batch=batch, seq=seq, row0=0, **past_args)
    x, h2 = _outproj(og, of, x, wts["w_out"], wts["g2"])
    e1, e2, cj = _route_act(h2, wts["wq"], wts["sk"], wts["ut"])
    x = _peer_out(e1, e2, cj, wts["v"], x)
    return x, (k_st, v_st, lf_st), state


def kernel(x_prompt, x_sample, cache_fox_k, cache_fox_v, cache_fox_logf, state_gla, g_norm1, w_in, w_gla_a2,
           b_gla_a, g_gla_onorm, g_fox_qnorm, g_fox_knorm, b_fox_f, w_out, g_norm2, w_peer_q, peer_subkeys,
           peer_u, peer_v):
    bp, tp, _ = x_prompt.shape
    bs, ts, _ = x_sample.shape
    depth = w_in.shape[0]
    xp = x_prompt.reshape(bp * tp, D_MODEL)
    xs = x_sample.reshape(bs * ts, D_MODEL)
    stacks = lambda n: (jnp.zeros((depth, n * FOX_HEADS, FOX_HD), F32), jnp.zeros((depth, n * FOX_HEADS, FOX_HD), F32),
                        jnp.zeros((depth, n, FOX_HEADS), F32))
    stk_p, stk_s, st_p, st_s = stacks(bp * tp), stacks(bs * ts), [], []
    zero_state = jnp.zeros((bp, GLA_HEADS, GLA_DK, GLA_DV), F32)
    for l in range(depth):
        wts = _prep_weights(g_norm1[l], w_in[l], w_gla_a2[l], b_gla_a[l], g_gla_onorm[l], g_fox_qnorm[l],
                            g_fox_knorm[l], b_fox_f[l], w_out[l], g_norm2[l], w_peer_q[l], peer_subkeys[l],
                            peer_u[l], peer_v[l])
        xp, stk_p, state = _path_layer(xp, wts, l, depth, stk_p, bp, tp, zero_state)
        st_p.append(state)
        xs, stk_s, state = _path_layer(xs, wts, l, depth, stk_s, bs, ts, state_gla[l],
                                       cache_fox_k[l], cache_fox_v[l], cache_fox_logf[l])
        st_s.append(state)

    def unstack(stk, b, t):
        k, v, lf = stk
        return (k.reshape(depth, b, t, FOX_HEADS, FOX_HD), v.reshape(depth, b, t, FOX_HEADS, FOX_HD),
                lf.reshape(depth, b, t, FOX_HEADS))

    return (xp.reshape(bp, tp, D_MODEL), xs.reshape(bs, ts, D_MODEL),
            *unstack(stk_p, bp, tp), jnp.stack(st_p), *unstack(stk_s, bs, ts), jnp.stack(st_s))
```

```python
import functools

import jax
import jax.numpy as jnp
from jax import lax
from jax.experimental import pallas as pl
from jax.experimental.pallas import tpu as pltpu

F32 = jnp.float32
BF16 = jnp.bfloat16
I32 = jnp.int32

D_MODEL = 1024
EPS = 1e-6
GLA_CHUNK = 64
GLA_HEADS = 4
GLA_DK = 64
GLA_DV = 128
GLA_LOWRANK = 16
GLA_GATE_NORM = 16.0
GLA_SUB = 16
FOX_HEADS = 8
FOX_HD = 64
FOX_SCALE = FOX_HD ** -0.5
LOG2E = 1.4426950408889634
FOX_BLOCK = 512
IN_SIZES = (256, 256, 512, 512, GLA_LOWRANK, 512, 512, 512, FOX_HEADS)
PEER_HEADS = 8
PEER_NKEYS = 128
PEER_TOPK = 16
PEER_NEXP = PEER_NKEYS * PEER_NKEYS
PEER_J = PEER_HEADS * PEER_TOPK

LANES = 128
SUBLANES = 8
Z_MAIN = 3072
Z_SMALL = 128
VMEM_LIMIT = 56 * 1024 * 1024

ZC_GQ, ZC_GK, ZC_GV, ZC_GG, ZC_FQ, ZC_FK, ZC_FV = 0, 2, 4, 8, 12, 16, 20


def _params(sem):
    return pltpu.CompilerParams(dimension_semantics=sem, vmem_limit_bytes=VMEM_LIMIT)


def _tile(n, cap):
    t = cap
    while n % t:
        t //= 2
    return t


def _dot(a, b):
    return jnp.dot(a, b, preferred_element_type=F32)


def _dot_nt(a, b):
    return lax.dot_general(a, b, (((1,), (1,)), ((), ())), preferred_element_type=F32)


def _dot_tn(a, b):
    return lax.dot_general(a, b, (((0,), (0,)), ((), ())), preferred_element_type=F32)


def _split(x):
    hi = x.astype(BF16)
    lo = (x - hi.astype(F32)).astype(BF16)
    return hi, lo


def _dot_exact_rhs(x, m):
    hi, lo = _split(x)
    return _dot(hi, m) + _dot(lo, m)


def _dot_exact_lhs(m, x):
    hi, lo = _split(x)
    return _dot(m, hi) + _dot(m, lo)


def _log_sigmoid(y):
    return jnp.minimum(y, 0.0) - jnp.log(1.0 + jnp.exp(-jnp.abs(y)))


def _const_spec(shape):
    nd = len(shape)
    return pl.BlockSpec(shape, lambda *_: (0,) * nd)


def _inproj_kernel(*refs, n_alias):
    (x_ref, g1_ref, w_ref, wa2_ref, ba_ref, bsm_ref, gq_ref, gk_ref, ind_ref, indt_ref) = refs[:10]
    z_ref, la_ref, k32_ref, v32_ref, lf_ref = refs[10 + n_alias:]
    x = x_ref[...]
    ms = jnp.mean(x * x, axis=-1, keepdims=True)
    h = (x * lax.rsqrt(ms + EPS) * g1_ref[...]).astype(BF16)
    z = _dot(h, w_ref[...])

    def headnorm(t, gain):
        ss = _dot_exact_rhs(t * t, ind_ref[...])
        r = lax.rsqrt(ss * (1.0 / FOX_HD) + EPS)
        return t * _dot_exact_rhs(r, indt_ref[...]) * gain

    fq = headnorm(z[:, 1536:2048], gq_ref[...])
    fk = headnorm(z[:, 2048:2560], gk_ref[...])
    fv = z[:, 2560:3072]
    z_ref[:, 0:1536] = z[:, 0:1536].astype(BF16)
    z_ref[:, 1536:2048] = fq.astype(BF16)
    z_ref[:, 2048:2560] = fk.astype(BF16)
    z_ref[:, 2560:3072] = fv.astype(BF16)
    tm = x_ref.shape[0]
    for hd in range(FOX_HEADS):
        token_rows = pl.ds(hd, tm, stride=FOX_HEADS)
        k32_ref[0, token_rows, :] = fk[:, hd * FOX_HD:(hd + 1) * FOX_HD]
        v32_ref[0, token_rows, :] = fv[:, hd * FOX_HD:(hd + 1) * FOX_HD]
    small = z[:, Z_MAIN:Z_MAIN + Z_SMALL]
    lf_ref[0] = _log_sigmoid(small + bsm_ref[...])[:, 0:FOX_HEADS]
    y = _dot(small.astype(BF16), wa2_ref[...]) + ba_ref[...]
    la_ref[...] = _log_sigmoid(y) * (1.0 / GLA_GATE_NORM)


def _inproj(x, consts, layer, depth, stacked):
    n = x.shape[0]
    tm = _tile(n, 512)
    row = lambda i: (i, 0)
    slab = lambda i: (layer, i, 0)
    n_alias = len(stacked)
    kv_block = (1, tm * FOX_HEADS, FOX_HD)
    return pl.pallas_call(
        functools.partial(_inproj_kernel, n_alias=n_alias),
        grid=(n // tm,),
        in_specs=[pl.BlockSpec((tm, D_MODEL), row)] + [_const_spec(c.shape) for c in consts]
        + [pl.BlockSpec(memory_space=pl.ANY)] * n_alias,
        out_specs=[pl.BlockSpec((tm, Z_MAIN), row), pl.BlockSpec((tm, 256), row)]
        + [pl.BlockSpec(kv_block, slab), pl.BlockSpec(kv_block, slab), pl.BlockSpec((1, tm, FOX_HEADS), slab)],
        out_shape=[jax.ShapeDtypeStruct((n, Z_MAIN), BF16), jax.ShapeDtypeStruct((n, 256), F32)]
        + [jax.ShapeDtypeStruct(a.shape, F32) for a in stacked],
        input_output_aliases={1 + len(consts) + i: 2 + i for i in range(n_alias)},
        compiler_params=_params(("parallel",)),
    )(x, *consts, *stacked)


def _gla_kernel(q_ref, k_ref, v_ref, gg_ref, la_ref, s0_ref, gon_ref, tri_ref, o_ref, sfin_ref, st_scr,
                *, chunk, nchunk):
    t = pl.program_id(1)
    nsub = chunk // GLA_SUB

    @pl.when(t == 0)
    def _():
        for p in range(2):
            st_scr[p] = s0_ref[0, p].T

    rowi = lax.broadcasted_iota(I32, (chunk, LANES), 0)
    lane = lax.broadcasted_iota(I32, (chunk, LANES), 1)
    lane_st = lax.broadcasted_iota(I32, (chunk, nsub * LANES), 1)
    lane_sq = lax.broadcasted_iota(I32, (LANES, LANES), 1)
    arow = lax.broadcasted_iota(I32, (chunk, chunk), 0)
    acol = lax.broadcasted_iota(I32, (chunk, chunk), 1)

    per_trip = next(n for n in (4, 2, 1) if nchunk % n == 0)

    def body(trip, carry):
        rows_u, prep = [], []
        for u in range(per_trip):
            rows = pl.ds(pl.multiple_of((trip * per_trip + u) * chunk, chunk), chunk)
            rows_u.append(rows)
            bcum = _dot_exact_lhs(tri_ref[...], la_ref[rows, :])
            q = q_ref[rows, :].astype(F32) * (GLA_DK ** -0.5)
            k = k_ref[rows, :].astype(F32)
            for p in range(2):
                cols = slice(p * LANES, (p + 1) * LANES)
                bp, qp, kp = bcum[:, cols], q[:, cols], k[:, cols]
                blast = bp[chunk - 1:chunk, :]
                qs, ks = [], []
                for sb in range(nsub):
                    beta = bp[sb * GLA_SUB - 1:sb * GLA_SUB, :] if sb else jnp.zeros((1, LANES), F32)
                    inblk = (rowi >= sb * GLA_SUB) & (rowi < (sb + 1) * GLA_SUB)
                    qs.append(jnp.where(inblk, qp * jnp.exp(jnp.where(inblk, bp - beta, 0.0)), 0.0))
                    valid = rowi < (sb + 1) * GLA_SUB
                    ks.append(jnp.where(valid, kp * jnp.exp(jnp.where(valid, beta - bp, 0.0)), 0.0))
                prep.append(dict(qst=jnp.concatenate(qs, axis=1), kst=jnp.concatenate(ks, axis=1).astype(BF16),
                                 qinter=qp * jnp.exp(bp), kdec=(kp * jnp.exp(blast - bp)).astype(BF16),
                                 decay=jnp.exp(blast)))
        heads = [(u, p, i) for u in range(per_trip) for p in range(2) for i in range(2)]
        hcols = [slice((2 * p + i) * GLA_DV, (2 * p + i + 1) * GLA_DV) for _, p, i in heads]
        vhs = [v_ref[rows_u[u], hc] for (u, _, _), hc in zip(heads, hcols)]
        a_s = [_dot_nt(jnp.where((lane_st & GLA_DK) == i * GLA_DK, prep[2 * u + p]["qst"], 0.0).astype(BF16),
                       prep[2 * u + p]["kst"]) for u, p, i in heads]
        upd = [_dot_tn(vh, prep[2 * u + p]["kdec"]) for (u, p, i), vh in zip(heads, vhs)]
        a_s = [jnp.where(acol <= arow, a, 0.0).astype(BF16) for a in a_s]
        intra = [_dot(a, vh) for a, vh in zip(a_s, vhs)]
        states = [st_scr[p] for p in range(2)]
        for u in range(per_trip):
            for p in range(2):
                new = states[2 * u + p] * prep[2 * u + p]["decay"] + jnp.where(
                    lane_sq < GLA_DK, upd[4 * u + 2 * p], upd[4 * u + 2 * p + 1])
                states.append(new)
        for p in range(2):
            st_scr[p] = states[2 * per_trip + p]
        stb = [st.astype(BF16) for st in states[:2 * per_trip]]
        o_s = [o_in + _dot_nt(jnp.where((lane & GLA_DK) == i * GLA_DK, prep[2 * u + p]["qinter"], 0.0)
                              .astype(BF16), stb[2 * u + p]) for (u, p, i), o_in in zip(heads, intra)]
        for (u, _, _), o, hc in zip(heads, o_s, hcols):
            on = o * lax.rsqrt(jnp.mean(o * o, axis=-1, keepdims=True) + EPS) * gon_ref[...]
            gate = gg_ref[rows_u[u], hc].astype(F32)
            o_ref[rows_u[u], hc] = (on * gate * (1.0 / (1.0 + jnp.exp(-gate)))).astype(BF16)
        return carry

    lax.fori_loop(0, nchunk // per_trip, body, 0)

    @pl.when(t == pl.num_programs(1) - 1)
    def _():
        for p in range(2):
            sfin_ref[0, p] = st_scr[p].T


def _gla(z, la, s0, gon, *, batch, seq, row0):
    chunk = min(seq, GLA_CHUNK)
    tb = min(seq, 512)
    nt = seq // tb
    blk0 = row0 // tb
    tri = jnp.tril(jnp.ones((chunk, chunk), F32)).astype(BF16)
    rows = lambda col: (lambda b, t: (blk0 + b * nt + t, col))
    state_spec = pl.BlockSpec((1, 2, LANES, LANES), lambda b, t: (b, 0, 0, 0))
    o, sfin = pl.pallas_call(
        functools.partial(_gla_kernel, chunk=chunk, nchunk=tb // chunk),
        grid=(batch, nt),
        in_specs=[pl.BlockSpec((tb, 256), rows(0)), pl.BlockSpec((tb, 256), rows(1)),
                  pl.BlockSpec((tb, 512), rows(1)), pl.BlockSpec((tb, 512), rows(2)),
                  pl.BlockSpec((tb, 256), rows(0)), state_spec,
                  _const_spec((1, GLA_DV)), _const_spec((chunk, chunk))],
        out_specs=[pl.BlockSpec((tb, 512), lambda b, t: (b * nt + t, 0)), state_spec],
        out_shape=[jax.ShapeDtypeStruct((batch * seq, 512), BF16),
                   jax.ShapeDtypeStruct((batch, 2, LANES, LANES), F32)],
        scratch_shapes=[pltpu.VMEM((2, LANES, LANES), F32)],
        compiler_params=_params(("parallel", "arbitrary")),
    )(z, z, z, z, la, s0.reshape(batch, 2, LANES, LANES), gon, tri)
    return o, sfin.reshape(batch, GLA_HEADS, GLA_DK, GLA_DV)


def _fox_kernel(*refs, seq, past, tq, has_past):
    if has_past:
        (q_ref, kc_ref, vc_ref, lfc_ref, triu_ref, kp_ref, vp_ref, lfp_ref, triup_ref,
         o_ref, cc_scr, cp_scr) = refs
    else:
        q_ref, kc_ref, vc_ref, lfc_ref, triu_ref, o_ref, cc_scr = refs
    i = pl.program_id(2)
    nq = seq // tq
    pblk = FOX_BLOCK
    past_blk = min(past, FOX_BLOCK * FOX_BLOCK // tq)

    @pl.when(i == 0)
    def _():
        carry = jnp.zeros((2, 1), F32)
        if has_past:
            for jb in range(past // pblk):
                x = lfp_ref[0, 0, :, jb * pblk:(jb + 1) * pblk]
                cp_scr[:, jb * pblk:(jb + 1) * pblk] = _dot_exact_rhs(x, triup_ref[...]) + carry
                carry = carry + jnp.sum(x, axis=1, keepdims=True)
        for jb in range(nq):
            x = lfc_ref[0, 0, :, jb * tq:(jb + 1) * tq]
            cc_scr[:, jb * tq:(jb + 1) * tq] = _dot_exact_rhs(x, triu_ref[...]) + carry
            carry = carry + jnp.sum(x, axis=1, keepdims=True)

    q = q_ref[...]
    lane = lax.broadcasted_iota(I32, (tq, LANES), 1)
    qh = [jnp.where(lane < FOX_HD, q, jnp.zeros_like(q)), jnp.where(lane >= FOX_HD, q, jnp.zeros_like(q))]

    def kv_step(state, kb, vb, crows, mask):
        hs = range(2)
        lane_v = lax.broadcasted_iota(I32, vb.shape, 1)
        ones = jnp.ones_like(vb)
        vh = [jnp.where(lane_v < FOX_HD, vb, ones), jnp.where(lane_v >= FOX_HD, vb, ones)]
        s = [_dot_nt(qh[h], kb) * (FOX_SCALE * LOG2E) - crows[h] * LOG2E for h in hs]
        if mask is not None:
            s = [jnp.where(mask, s[h], -jnp.inf) for h in hs]
        m_new = [jnp.maximum(state[h][0], jnp.max(s[h], axis=1, keepdims=True)) for h in hs]
        pv = [_dot(jnp.exp2(s[h] - m_new[h]).astype(BF16), vh[h]) for h in hs]
        alpha = [jnp.exp2(state[h][0] - m_new[h]) for h in hs]
        return tuple((m_new[h], alpha[h] * state[h][1] + pv[h]) for h in hs)

    init = tuple((jnp.full((tq, 1), -jnp.inf, F32), jnp.zeros((tq, LANES), F32)) for _ in range(2))
    state = init
    if has_past:
        def past_body(j, st):
            off = pl.multiple_of(j * past_blk, past_blk)
            kb = kp_ref[0, pl.ds(off, past_blk), :].astype(BF16)
            vb = vp_ref[0, pl.ds(off, past_blk), :].astype(BF16)
            crows = [cp_scr[h:h + 1, pl.ds(off, past_blk)] for h in range(2)]
            return kv_step(st, kb, vb, crows, None)
        state = lax.fori_loop(0, past // past_blk, past_body, state)
    if nq > 1:
        def cur_body(j, st):
            off = pl.multiple_of(j * tq, tq)
            crows = [cc_scr[h:h + 1, pl.ds(off, tq)] for h in range(2)]
            return kv_step(st, kc_ref[pl.ds(off, tq), :], vc_ref[pl.ds(off, tq), :], crows, None)
        state = lax.fori_loop(0, i, cur_body, state)
        off = pl.multiple_of(i * tq, tq)
    else:
        off = 0
    causal = lax.broadcasted_iota(I32, (tq, tq), 1) <= lax.broadcasted_iota(I32, (tq, tq), 0)
    crows = [cc_scr[h:h + 1, pl.ds(off, tq)] for h in range(2)]
    state = kv_step(state, kc_ref[pl.ds(off, tq), :], vc_ref[pl.ds(off, tq), :], crows, causal)
    acc = jnp.where(lane < FOX_HD, state[0][1], state[1][1])
    den = jnp.where(lane < FOX_HD, pltpu.roll(state[0][1], FOX_HD, 1), pltpu.roll(state[1][1], FOX_HD, 1))
    o_ref[...] = (acc / den).astype(BF16)


def _fox(z, lf_cur_t, *, batch, seq, row0, k_past=None, v_past=None, lf_past_t=None):
    has_past = k_past is not None
    past = k_past.shape[1] if has_past else 0
    tq = min(seq, FOX_BLOCK)
    nq = seq // tq
    qblk0 = row0 // tq
    sblk0 = row0 // seq
    triu = jnp.triu(jnp.ones((tq, tq), F32)).astype(BF16)
    in_specs = [pl.BlockSpec((tq, LANES), lambda b, p, i: (qblk0 + b * nq + i, ZC_FQ + p)),
                pl.BlockSpec((seq, LANES), lambda b, p, i: (sblk0 + b, ZC_FK + p)),
                pl.BlockSpec((seq, LANES), lambda b, p, i: (sblk0 + b, ZC_FV + p)),
                pl.BlockSpec((1, 1, 2, seq), lambda b, p, i: (b, p, 0, 0)),
                _const_spec((tq, tq))]
    args = [z, z, z, lf_cur_t, triu]
    scratch = [pltpu.VMEM((2, seq), F32)]
    if has_past:
        triup = jnp.triu(jnp.ones((FOX_BLOCK, FOX_BLOCK), F32)).astype(BF16)
        in_specs += [pl.BlockSpec((1, past, LANES), lambda b, p, i: (b, 0, p)),
                     pl.BlockSpec((1, past, LANES), lambda b, p, i: (b, 0, p)),
                     pl.BlockSpec((1, 1, 2, past), lambda b, p, i: (b, p, 0, 0)),
                     _const_spec((FOX_BLOCK, FOX_BLOCK))]
        args += [k_past, v_past, lf_past_t, triup]
        scratch.append(pltpu.VMEM((2, past), F32))
    return pl.pallas_call(
        functools.partial(_fox_kernel, seq=seq, past=past, tq=tq, has_past=has_past),
        grid=(batch, FOX_HEADS // 2, nq),
        in_specs=in_specs,
        out_specs=pl.BlockSpec((tq, LANES), lambda b, p, i: (b * nq + i, p)),
        out_shape=jax.ShapeDtypeStruct((batch * seq, 512), BF16),
        scratch_shapes=scratch,
        compiler_params=_params(("parallel", "parallel", "arbitrary")),
    )(*args)


def _outproj_kernel(og_ref, of_ref, x_ref, w_ref, g2_ref, xo_ref, h2_ref):
    mixed = _dot(og_ref[...], w_ref[0:512, :]) + _dot(of_ref[...], w_ref[512:1024, :])
    x = x_ref[...] + mixed
    xo_ref[...] = x
    ms = jnp.mean(x * x, axis=-1, keepdims=True)
    h2_ref[...] = (x * lax.rsqrt(ms + EPS) * g2_ref[...]).astype(BF16)


def _outproj(og, of, x, w, g2):
    n = x.shape[0]
    tm = _tile(n, 512)
    row = lambda i: (i, 0)
    return pl.pallas_call(
        _outproj_kernel,
        grid=(n // tm,),
        in_specs=[pl.BlockSpec((tm, 512), row), pl.BlockSpec((tm, 512), row),
                  pl.BlockSpec((tm, D_MODEL), row), _const_spec(w.shape), _const_spec(g2.shape)],
        out_specs=[pl.BlockSpec((tm, D_MODEL), row), pl.BlockSpec((tm, D_MODEL), row)],
        out_shape=[jax.ShapeDtypeStruct((n, D_MODEL), F32), jax.ShapeDtypeStruct((n, D_MODEL), BF16)],
        compiler_params=_params(("parallel",)),
    )(og, of, x, w, g2)


def _merge_exchange_pairs(n):
    full = 1
    while full < n:
        full *= 2
    pairs = []
    p = 1
    while p < full:
        k = p
        while k >= 1:
            for j in range(k % p, full - k, 2 * k):
                for i in range(min(k, full - j - k)):
                    if (i + j) // (2 * p) == (i + j + k) // (2 * p):
                        pairs.append((i + j, i + j + k))
            k //= 2
        p *= 2
    return tuple((a, b) for a, b in pairs if b < n)


def _topk_keys(s, k):
    nrows, tn = s.shape
    groups = nrows // SUBLANES
    sub = lax.broadcasted_iota(I32, (SUBLANES, tn), 0)
    vals = [s[SUBLANES * r:SUBLANES * (r + 1), :] for r in range(groups)]
    idxs = [sub + SUBLANES * r for r in range(groups)]
    for a, b in _merge_exchange_pairs(groups):
        va, vb, ia, ib = vals[a], vals[b], idxs[a], idxs[b]
        a_first = (va > vb) | ((va == vb) & (ia < ib))
        vals[a], vals[b] = jnp.maximum(va, vb), jnp.minimum(va, vb)
        idxs[a], idxs[b] = jnp.where(a_first, ia, ib), jnp.where(a_first, ib, ia)
    out_v, out_i = [], []
    for t in range(k):
        head_v, head_i = vals[0], idxs[0]
        m = jnp.max(head_v, axis=0, keepdims=True)
        idx = jnp.min(jnp.where(head_v == m, head_i, nrows), axis=0, keepdims=True)
        out_v.append(m)
        out_i.append(idx)
        won = head_i == idx
        for r in range(min(k - 1 - t, groups)):
            last = r + 1 == groups
            vals[r] = jnp.where(won, -jnp.inf if last else vals[r + 1], vals[r])
            idxs[r] = jnp.where(won, nrows if last else idxs[r + 1], idxs[r])
    return jnp.concatenate(out_v, axis=0), jnp.concatenate(out_i, axis=0)


def _pair_topk(v1, i1, v2, i2):
    tn = v1.shape[1]
    iota_k = lax.broadcasted_iota(I32, (PEER_TOPK, tn), 0)
    iota_8 = lax.broadcasted_iota(I32, (8, tn), 0)

    def pick(table, sel):
        rows = []
        for r in range(PEER_TOPK):
            rows.append(jnp.sum(jnp.where(iota_k == sel[r:r + 1, :], table, 0), axis=0, keepdims=True))
        return jnp.concatenate(rows, axis=0)

    blocks = [v1[0:1, :] + v2]
    for a in range(1, 8):
        blocks.append(jnp.where(iota_8 < PEER_TOPK // (a + 1), v1[a:a + 1, :] + v2[0:8, :], -jnp.inf))
    blocks.append(v1[8:16, :] + v2[0:1, :])
    top, ti = _topk_keys(jnp.concatenate(blocks, axis=0), PEER_TOPK)
    mid = ti - PEER_TOPK
    e1 = pick(i1, jnp.where(ti < 16, 0, jnp.where(ti < 72, (mid >> 3) + 1, ti - 64)))
    e2 = pick(i2, jnp.where(ti < 16, ti, jnp.where(ti < 72, mid & 7, 0)))
    ex = jnp.exp(top - jnp.max(top, axis=0, keepdims=True))
    return e1, e2, ex / jnp.sum(ex, axis=0, keepdims=True)


PEER_ROWS = 16
PEER_EC = PEER_ROWS * PEER_NKEYS
assert PEER_NEXP // PEER_EC == PEER_HEADS


def _gelu(a):
    return 0.5 * a * (1.0 + lax.erf(a * (2.0 ** -0.5)))


def _route_act_kernel(hn_ref, hp_ref, wq_ref, sk_ref, ut_ref, e1_ref, e2_ref, c_ref,
                      q_scr, r1_scr, r2_scr, rg_scr, t1_scr, t2_scr, tg_scr, acc_scr):
    i = pl.program_id(0)
    c = pl.program_id(1)
    slot_new = i % 2
    slot_old = 1 - slot_new

    @pl.when((i == 0) & (c == 0))
    def _():
        t1_scr[1] = jnp.zeros_like(t1_scr[1])
        t2_scr[1] = jnp.zeros_like(t2_scr[1])
        tg_scr[1] = jnp.zeros_like(tg_scr[1])

    @pl.when(c == 0)
    def _():
        acc_scr[...] = jnp.zeros_like(acc_scr)
        q = _dot(hn_ref[...], wq_ref[...])
        for hc in range(2 * PEER_HEADS):
            q_scr[hc] = q[:, hc * LANES:(hc + 1) * LANES].astype(BF16)

    v1, i1 = _topk_keys(_dot_nt(sk_ref[2 * c], q_scr[2 * c]), PEER_TOPK)
    v2, i2 = _topk_keys(_dot_nt(sk_ref[2 * c + 1], q_scr[2 * c + 1]), PEER_TOPK)
    e1h, e2h, gh = _pair_topk(v1, i1, v2, i2)
    rows = pl.ds(pl.multiple_of(c * PEER_TOPK, PEER_TOPK), PEER_TOPK)
    r1_scr[rows, :] = e1h
    r2_scr[rows, :] = e2h
    rg_scr[rows, :] = gh

    a_all = _dot(hp_ref[...], ut_ref[...])
    e1 = t1_scr[slot_old]
    e2 = t2_scr[slot_old]
    acc = acc_scr[...]
    for r in range(PEER_ROWS):
        picked = jnp.take_along_axis(a_all[:, r * LANES:(r + 1) * LANES], e2, axis=1,
                                     mode="promise_in_bounds")
        acc = jnp.where(e1 == c * PEER_ROWS + r, picked, acc)
    acc_scr[...] = acc

    @pl.when(c == pl.num_programs(1) - 1)
    def _():
        c_ref[...] = tg_scr[slot_old] * _gelu(acc)
        t1_scr[slot_new] = r1_scr[...].T
        t2_scr[slot_new] = r2_scr[...].T
        tg_scr[slot_new] = rg_scr[...].T
        e1_ref[...] = t1_scr[slot_new]
        e2_ref[...] = t2_scr[slot_new]


def _route_act(h2, wq, sk, ut):
    n = h2.shape[0]
    tn = _tile(n, 512)
    tiles = n // tn
    new = lambda i, c: (jnp.minimum(i, tiles - 1), 0)
    old = lambda i, c: (jnp.maximum(i - 1, 0), 0)
    return pl.pallas_call(
        _route_act_kernel,
        grid=(tiles + 1, PEER_HEADS),
        in_specs=[pl.BlockSpec((tn, D_MODEL), new), pl.BlockSpec((tn, D_MODEL), old),
                  _const_spec(wq.shape), _const_spec(sk.shape),
                  pl.BlockSpec((D_MODEL, PEER_EC), lambda i, c: (0, c))],
        out_specs=[pl.BlockSpec((tn, PEER_J), new), pl.BlockSpec((tn, PEER_J), new),
                   pl.BlockSpec((tn, PEER_J), old)],
        out_shape=[jax.ShapeDtypeStruct((n, PEER_J), I32), jax.ShapeDtypeStruct((n, PEER_J), I32),
                   jax.ShapeDtypeStruct((n, PEER_J), F32)],
        scratch_shapes=[pltpu.VMEM((2 * PEER_HEADS, tn, LANES), BF16),
                        pltpu.VMEM((PEER_J, tn), I32), pltpu.VMEM((PEER_J, tn), I32),
                        pltpu.VMEM((PEER_J, tn), F32),
                        pltpu.VMEM((2, tn, PEER_J), I32), pltpu.VMEM((2, tn, PEER_J), I32),
                        pltpu.VMEM((2, tn, PEER_J), F32), pltpu.VMEM((tn, PEER_J), F32)],
        compiler_params=_params(("arbitrary", "arbitrary")),
    )(h2, h2, wq, sk, ut)


PEER_HALF = PEER_ROWS // 2


def _peer_out_kernel(e1_ref, e2_ref, c_ref, v_ref, x_ref, o_ref, y_scr, acc_scr):
    tn = x_ref.shape[0]
    c = pl.program_id(1)
    nchunks = PEER_NEXP // PEER_EC

    @pl.when(c == 0)
    def _():
        acc_scr[...] = jnp.zeros_like(acc_scr)
        key = lax.broadcasted_iota(I32, (PEER_NKEYS, PEER_J), 0)

        def token(t, carry):
            row = pl.ds(t, 1)
            d = jnp.where(key == e1_ref[row, :], c_ref[row, :], 0.0).astype(BF16)
            w = jnp.where(key == e2_ref[row, :], 1.0, 0.0).astype(BF16)
            y = _dot_nt(d, w).astype(BF16).astype(F32)
            bits = lax.bitcast_convert_type(y, I32)
            base = pl.multiple_of(t * PEER_HALF, PEER_HALF)
            for cc in range(nchunks):
                hi = bits[cc * PEER_ROWS:cc * PEER_ROWS + PEER_HALF, :]
                lo = bits[cc * PEER_ROWS + PEER_HALF:(cc + 1) * PEER_ROWS, :]
                y_scr[cc, pl.ds(base, PEER_HALF), :] = hi | lax.shift_right_logical(lo, 16)
            return carry

        lax.fori_loop(0, tn, token, 0, unroll=32)

    his, los = [], []
    for i in range(PEER_HALF):
        words = y_scr[c, pl.ds(i, tn, stride=PEER_HALF), :]
        his.append(lax.bitcast_convert_type(words & -65536, F32).astype(BF16))
        los.append(lax.bitcast_convert_type(words << 16, F32).astype(BF16))
    acc_scr[...] += _dot(jnp.concatenate(his + los, axis=1), v_ref[...])

    @pl.when(c == nchunks - 1)
    def _():
        o_ref[...] = x_ref[...] + acc_scr[...]


def _peer_out(e1, e2, cj, vt, x):
    n = x.shape[0]
    tn = _tile(n, 512)
    row = lambda i, c: (i, 0)
    nchunks = PEER_NEXP // PEER_EC
    return pl.pallas_call(
        _peer_out_kernel,
        grid=(n // tn, nchunks),
        in_specs=[pl.BlockSpec((tn, PEER_J), row), pl.BlockSpec((tn, PEER_J), row),
                  pl.BlockSpec((tn, PEER_J), row), pl.BlockSpec((PEER_EC, D_MODEL), lambda i, c: (c, 0)),
                  pl.BlockSpec((tn, D_MODEL), row)],
        out_specs=pl.BlockSpec((tn, D_MODEL), row),
        out_shape=jax.ShapeDtypeStruct((n, D_MODEL), F32),
        scratch_shapes=[pltpu.VMEM((nchunks, tn * PEER_HALF, LANES), I32),
                        pltpu.VMEM((tn, D_MODEL), F32)],
        compiler_params=_params(("parallel", "arbitrary")),
    )(e1, e2, cj, vt, x)


def _pair_major(lf, batch, length):
    return lf.reshape(batch, length, FOX_HEADS // 2, 2).transpose(0, 2, 3, 1)


def _prep_weights(g_norm1, w_in, w_gla_a2, b_gla_a, g_gla_onorm, g_fox_qnorm, g_fox_knorm, b_fox_f, w_out,
                  g_norm2, w_peer_q, peer_subkeys, peer_u, peer_v):
    bounds = [0]
    for size in IN_SIZES:
        bounds.append(bounds[-1] + size)
    gq, gk, gv, gg, glr, fq, fk, fv, ff = [w_in[:, bounds[i]:bounds[i + 1]] for i in range(len(IN_SIZES))]
    pad = jnp.zeros((D_MODEL, Z_SMALL - FOX_HEADS - GLA_LOWRANK), w_in.dtype)
    w_cat = jnp.concatenate([gq, gk, gv, gg, fq, fk, fv, ff, glr, pad], axis=1).astype(BF16)
    wa2p = jnp.zeros((Z_SMALL, 256), F32).at[FOX_HEADS:FOX_HEADS + GLA_LOWRANK].set(w_gla_a2).astype(BF16)
    bsm = jnp.zeros((1, Z_SMALL), F32).at[0, :FOX_HEADS].set(b_fox_f)
    head_of_col = jnp.arange(FOX_HEADS * FOX_HD) // FOX_HD
    ind = (head_of_col[:, None] == jnp.arange(LANES)[None, :]).astype(BF16)
    return dict(
        inproj=(g_norm1[None], w_cat, wa2p, b_gla_a[None], bsm, jnp.tile(g_fox_qnorm, FOX_HEADS)[None],
                jnp.tile(g_fox_knorm, FOX_HEADS)[None], ind, ind.T),
        gon=g_gla_onorm[None], w_out=w_out.astype(BF16), g2=g_norm2[None], wq=w_peer_q.astype(BF16),
        sk=peer_subkeys.reshape(2 * PEER_HEADS, PEER_NKEYS, LANES).astype(BF16),
        ut=peer_u.T.astype(BF16), v=peer_v.astype(BF16))


def _path_layer(x, wts, layer, depth, stacked, batch, seq, s0, k_past=None, v_past=None, lf_past=None):
    z, la, k_st, v_st, lf_st = _inproj(x, wts["inproj"], layer, depth, stacked)
    og, state = _gla(z, la, s0, wts["gon"], batch=batch, seq=seq, row0=0)
    past_args = {}
    if k_past is not None:
        past = k_past.shape[1]
        past_args = dict(k_past=k_past.reshape(batch, past, FOX_HEADS * FOX_HD),
                         v_past=v_past.reshape(batch, past, FOX_HEADS * FOX_HD),
                         lf_past_t=lf_past.reshape(batch, past, FOX_HEADS // 2, 2).transpose(0, 2, 3, 1))
    of = _fox(z, _pair_major(lf_st[layer], batch, seq), batch=batch, seq=seq, row0=0, **past_args)
    x, h2 = _outproj(og, of, x, wts["w_out"], wts["g2"])
    e1, e2, cj = _route_act(h2, wts["wq"], wts["sk"], wts["ut"])
    x = _peer_out(e1, e2, cj, wts["v"], x)
    return x, (k_st, v_st, lf_st), state


def kernel(x_prompt, x_sample, cache_fox_k, cache_fox_v, cache_fox_logf, state_gla, g_norm1, w_in, w_gla_a2,
           b_gla_a, g_gla_onorm, g_fox_qnorm, g_fox_knorm, b_fox_f, w_out, g_norm2, w_peer_q, peer_subkeys,
           peer_u, peer_v):
    bp, tp, _ = x_prompt.shape
    bs, ts, _ = x_sample.shape
    depth = w_in.shape[0]
    xp = x_prompt.reshape(bp * tp, D_MODEL)
    xs = x_sample.reshape(bs * ts, D_MODEL)
    stacks = lambda n: (jnp.zeros((depth, n * FOX_HEADS, FOX_HD), F32), jnp.zeros((depth, n * FOX_HEADS, FOX_HD), F32),
                        jnp.zeros((depth, n, FOX_HEADS), F32))
    stk_p, stk_s, st_p, st_s = stacks(bp * tp), stacks(bs * ts), [], []
    zero_state = jnp.zeros((bp, GLA_HEADS, GLA_DK, GLA_DV), F32)
    for l in range(depth):
        wts = _prep_weights(g_norm1[l], w_in[l], w_gla_a2[l], b_gla_a[l], g_gla_onorm[l], g_fox_qnorm[l],
                            g_fox_knorm[l], b_fox_f[l], w_out[l], g_norm2[l], w_peer_q[l], peer_subkeys[l],
                            peer_u[l], peer_v[l])
        xp, stk_p, state = _path_layer(xp, wts, l, depth, stk_p, bp, tp, zero_state)
        st_p.append(state)
        xs, stk_s, state = _path_layer(xs, wts, l, depth, stk_s, bs, ts, state_gla[l],
                                       cache_fox_k[l], cache_fox_v[l], cache_fox_logf[l])
        st_s.append(state)

    def unstack(stk, b, t):
        k, v, lf = stk
        return (k.reshape(depth, b, t, FOX_HEADS, FOX_HD), v.reshape(depth, b, t, FOX_HEADS, FOX_HD),
                lf.reshape(depth, b, t, FOX_HEADS))

    return (xp.reshape(bp, tp, D_MODEL), xs.reshape(bs, ts, D_MODEL),
            *unstack(stk_p, bp, tp), jnp.stack(st_p), *unstack(stk_s, bs, ts), jnp.stack(st_s))
```

```python
import functools

import jax
import jax.numpy as jnp
from jax import lax
from jax.experimental import pallas as pl
from jax.experimental.pallas import tpu as pltpu

F32 = jnp.float32
BF16 = jnp.bfloat16
I32 = jnp.int32

D_MODEL = 1024
EPS = 1e-6
GLA_CHUNK = 64
GLA_HEADS = 4
GLA_DK = 64
GLA_DV = 128
GLA_LOWRANK = 16
GLA_GATE_NORM = 16.0
GLA_SUB = 16
FOX_HEADS = 8
FOX_HD = 64
FOX_SCALE = FOX_HD ** -0.5
LOG2E = 1.4426950408889634
FOX_BLOCK = 512
GLA_QK = GLA_HEADS * GLA_DK
GLA_V = GLA_HEADS * GLA_DV
FOX_W = FOX_HEADS * FOX_HD
IN_SIZES = (GLA_QK, GLA_QK, GLA_V, GLA_V, GLA_LOWRANK, FOX_W, FOX_W, FOX_W, FOX_HEADS)
PEER_HEADS = 8
PEER_NKEYS = 128
PEER_TOPK = 16
PEER_NEXP = PEER_NKEYS * PEER_NKEYS
PEER_J = PEER_HEADS * PEER_TOPK

LANES = 128
SUBLANES = 8
Z_MAIN = 3072
Z_SMALL = 128
VMEM_LIMIT = 56 * 1024 * 1024

ZC_GQ, ZC_GK, ZC_GV, ZC_GG, ZC_FQ, ZC_FK, ZC_FV = 0, 2, 4, 8, 12, 16, 20


def _params(sem):
    return pltpu.CompilerParams(dimension_semantics=sem, vmem_limit_bytes=VMEM_LIMIT)


def _tile(n, cap):
    t = cap
    while n % t:
        t //= 2
    return t


def _dot(a, b):
    return jnp.dot(a, b, preferred_element_type=F32)


def _dot_nt(a, b):
    return lax.dot_general(a, b, (((1,), (1,)), ((), ())), preferred_element_type=F32)


def _dot_tn(a, b):
    return lax.dot_general(a, b, (((0,), (0,)), ((), ())), preferred_element_type=F32)


def _split(x):
    hi = x.astype(BF16)
    lo = (x - hi.astype(F32)).astype(BF16)
    return hi, lo


def _dot_exact_rhs(x, m):
    hi, lo = _split(x)
    return _dot(hi, m) + _dot(lo, m)


def _dot_exact_lhs(m, x):
    hi, lo = _split(x)
    return _dot(m, hi) + _dot(m, lo)


def _log_sigmoid(y):
    return jnp.minimum(y, 0.0) - jnp.log(1.0 + jnp.exp(-jnp.abs(y)))


def _const_spec(shape):
    nd = len(shape)
    return pl.BlockSpec(shape, lambda *_: (0,) * nd)


def _inproj_kernel(*refs, n_alias, layer):
    (x_ref, g1_ref, w_ref, wa2_ref, ba_ref, bsm_ref, gq_ref, gk_ref, ind_ref, indt_ref) = refs[:10]
    z_ref, la_ref, k32_ref, v32_ref, lf_ref = refs[10 + n_alias:]
    for stack_ref in (k32_ref, v32_ref, lf_ref):
        for other in range(stack_ref.shape[0]):
            if other != layer:
                stack_ref[other] = jnp.zeros(stack_ref.shape[1:], F32)
    x = x_ref[...]
    ms = jnp.mean(x * x, axis=-1, keepdims=True)
    h = (x * lax.rsqrt(ms + EPS) * g1_ref[...]).astype(BF16)
    z = _dot(h, w_ref[...])

    def headnorm(t, gain):
        ss = _dot_exact_rhs(t * t, ind_ref[...])
        r = lax.rsqrt(ss * (1.0 / FOX_HD) + EPS)
        return t * _dot_exact_rhs(r, indt_ref[...]) * gain

    c_fq, c_fk, c_fv = ZC_FQ * LANES, ZC_FK * LANES, ZC_FV * LANES
    fq = headnorm(z[:, c_fq:c_fk], gq_ref[...])
    fk = headnorm(z[:, c_fk:c_fv], gk_ref[...])
    fv = z[:, c_fv:Z_MAIN]
    z_ref[:, 0:c_fq] = z[:, 0:c_fq].astype(BF16)
    z_ref[:, c_fq:c_fk] = fq.astype(BF16)
    z_ref[:, c_fk:c_fv] = fk.astype(BF16)
    z_ref[:, c_fv:Z_MAIN] = fv.astype(BF16)
    tm = x_ref.shape[0]
    for hd in range(FOX_HEADS):
        token_rows = pl.ds(hd, tm, stride=FOX_HEADS)
        k32_ref[layer, token_rows, :] = fk[:, hd * FOX_HD:(hd + 1) * FOX_HD]
        v32_ref[layer, token_rows, :] = fv[:, hd * FOX_HD:(hd + 1) * FOX_HD]
    small = z[:, Z_MAIN:Z_MAIN + Z_SMALL]
    lf_ref[layer] = _log_sigmoid(small + bsm_ref[...])[:, 0:FOX_HEADS]
    y = _dot(small.astype(BF16), wa2_ref[...]) + ba_ref[...]
    la_ref[...] = _log_sigmoid(y) * (1.0 / GLA_GATE_NORM)


def _inproj(x, consts, layer, depth, stacked):
    n = x.shape[0]
    tm = _tile(n, 512)
    row = lambda i: (i, 0)
    n_alias = len(stacked)
    slabs, slab = (1, lambda i: (layer, i, 0)) if n_alias else (depth, lambda i: (0, i, 0))
    kv_block = (slabs, tm * FOX_HEADS, FOX_HD)
    stack_shapes = ((depth, n * FOX_HEADS, FOX_HD), (depth, n * FOX_HEADS, FOX_HD), (depth, n, FOX_HEADS))
    return pl.pallas_call(
        functools.partial(_inproj_kernel, n_alias=n_alias, layer=0 if n_alias else layer),
        grid=(n // tm,),
        in_specs=[pl.BlockSpec((tm, D_MODEL), row)] + [_const_spec(c.shape) for c in consts]
        + [pl.BlockSpec(memory_space=pl.ANY)] * n_alias,
        out_specs=[pl.BlockSpec((tm, Z_MAIN), row), pl.BlockSpec((tm, GLA_QK), row)]
        + [pl.BlockSpec(kv_block, slab), pl.BlockSpec(kv_block, slab),
           pl.BlockSpec((slabs, tm, FOX_HEADS), slab)],
        out_shape=[jax.ShapeDtypeStruct((n, Z_MAIN), BF16), jax.ShapeDtypeStruct((n, GLA_QK), F32)]
        + [jax.ShapeDtypeStruct(shape, F32) for shape in stack_shapes],
        input_output_aliases={1 + len(consts) + i: 2 + i for i in range(n_alias)},
        compiler_params=_params(("parallel",)),
    )(x, *consts, *stacked)


def _gla_kernel(q_ref, k_ref, v_ref, gg_ref, la_ref, s0_ref, gon_ref, tri_ref, o_ref, sfin_ref, st_scr,
                *, chunk, nchunk):
    t = pl.program_id(1)
    nsub = chunk // GLA_SUB

    @pl.when(t == 0)
    def _():
        for p in range(2):
            st_scr[p] = s0_ref[0, p].T

    rowi = lax.broadcasted_iota(I32, (chunk, LANES), 0)
    lane = lax.broadcasted_iota(I32, (chunk, LANES), 1)
    lane_st = lax.broadcasted_iota(I32, (chunk, nsub * LANES), 1)
    lane_sq = lax.broadcasted_iota(I32, (LANES, LANES), 1)
    arow = lax.broadcasted_iota(I32, (chunk, chunk), 0)
    acol = lax.broadcasted_iota(I32, (chunk, chunk), 1)

    per_trip = next(n for n in (4, 2, 1) if nchunk % n == 0)

    def body(trip, carry):
        rows_u, prep = [], []
        for u in range(per_trip):
            rows = pl.ds(pl.multiple_of((trip * per_trip + u) * chunk, chunk), chunk)
            rows_u.append(rows)
            bcum = _dot_exact_lhs(tri_ref[...], la_ref[rows, :])
            q = q_ref[rows, :].astype(F32) * (GLA_DK ** -0.5)
            k = k_ref[rows, :].astype(F32)
            for p in range(2):
                cols = slice(p * LANES, (p + 1) * LANES)
                bp, qp, kp = bcum[:, cols], q[:, cols], k[:, cols]
                blast = bp[chunk - 1:chunk, :]
                qs, ks = [], []
                for sb in range(nsub):
                    beta = bp[sb * GLA_SUB - 1:sb * GLA_SUB, :] if sb else jnp.zeros((1, LANES), F32)
                    inblk = (rowi >= sb * GLA_SUB) & (rowi < (sb + 1) * GLA_SUB)
                    qs.append(jnp.where(inblk, qp * jnp.exp(jnp.where(inblk, bp - beta, 0.0)), 0.0))
                    valid = rowi < (sb + 1) * GLA_SUB
                    ks.append(jnp.where(valid, kp * jnp.exp(jnp.where(valid, beta - bp, 0.0)), 0.0))
                prep.append(dict(qst=jnp.concatenate(qs, axis=1), kst=jnp.concatenate(ks, axis=1).astype(BF16),
                                 qinter=qp * jnp.exp(bp), kdec=(kp * jnp.exp(blast - bp)).astype(BF16),
                                 decay=jnp.exp(blast)))
        heads = [(u, p, i) for u in range(per_trip) for p in range(2) for i in range(2)]
        hcols = [slice((2 * p + i) * GLA_DV, (2 * p + i + 1) * GLA_DV) for _, p, i in heads]
        vhs = [v_ref[rows_u[u], hc] for (u, _, _), hc in zip(heads, hcols)]
        a_s = [_dot_nt(jnp.where((lane_st & GLA_DK) == i * GLA_DK, prep[2 * u + p]["qst"], 0.0).astype(BF16),
                       prep[2 * u + p]["kst"]) for u, p, i in heads]
        upd = [_dot_tn(vh, prep[2 * u + p]["kdec"]) for (u, p, i), vh in zip(heads, vhs)]
        a_s = [jnp.where(acol <= arow, a, 0.0).astype(BF16) for a in a_s]
        intra = [_dot(a, vh) for a, vh in zip(a_s, vhs)]
        states = [st_scr[p] for p in range(2)]
        for u in range(per_trip):
            for p in range(2):
                new = states[2 * u + p] * prep[2 * u + p]["decay"] + jnp.where(
                    lane_sq < GLA_DK, upd[4 * u + 2 * p], upd[4 * u + 2 * p + 1])
                states.append(new)
        for p in range(2):
            st_scr[p] = states[2 * per_trip + p]
        stb = [st.astype(BF16) for st in states[:2 * per_trip]]
        o_s = [o_in + _dot_nt(jnp.where((lane & GLA_DK) == i * GLA_DK, prep[2 * u + p]["qinter"], 0.0)
                              .astype(BF16), stb[2 * u + p]) for (u, p, i), o_in in zip(heads, intra)]
        for (u, _, _), o, hc in zip(heads, o_s, hcols):
            on = o * lax.rsqrt(jnp.mean(o * o, axis=-1, keepdims=True) + EPS) * gon_ref[...]
            gate = gg_ref[rows_u[u], hc].astype(F32)
            o_ref[rows_u[u], hc] = (on * gate * (1.0 / (1.0 + jnp.exp(-gate)))).astype(BF16)
        return carry

    lax.fori_loop(0, nchunk // per_trip, body, 0)

    @pl.when(t == pl.num_programs(1) - 1)
    def _():
        for p in range(2):
            sfin_ref[0, p] = st_scr[p].T


def _gla(z, la, s0, gon, *, batch, seq):
    chunk = min(seq, GLA_CHUNK)
    tb = min(seq, 512)
    nt = seq // tb
    tri = jnp.tril(jnp.ones((chunk, chunk), F32)).astype(BF16)
    qk_w, v_w = GLA_QK, GLA_V
    rows = lambda col: (lambda b, t: (b * nt + t, col))
    state_spec = pl.BlockSpec((1, 2, LANES, LANES), lambda b, t: (b, 0, 0, 0))
    o, sfin = pl.pallas_call(
        functools.partial(_gla_kernel, chunk=chunk, nchunk=tb // chunk),
        grid=(batch, nt),
        in_specs=[pl.BlockSpec((tb, qk_w), rows(ZC_GQ * LANES // qk_w)),
                  pl.BlockSpec((tb, qk_w), rows(ZC_GK * LANES // qk_w)),
                  pl.BlockSpec((tb, v_w), rows(ZC_GV * LANES // v_w)),
                  pl.BlockSpec((tb, v_w), rows(ZC_GG * LANES // v_w)),
                  pl.BlockSpec((tb, qk_w), rows(0)), state_spec,
                  _const_spec((1, GLA_DV)), _const_spec((chunk, chunk))],
        out_specs=[pl.BlockSpec((tb, v_w), lambda b, t: (b * nt + t, 0)), state_spec],
        out_shape=[jax.ShapeDtypeStruct((batch * seq, v_w), BF16),
                   jax.ShapeDtypeStruct((batch, 2, LANES, LANES), F32)],
        scratch_shapes=[pltpu.VMEM((2, LANES, LANES), F32)],
        compiler_params=_params(("parallel", "arbitrary")),
    )(z, z, z, z, la, s0.reshape(batch, 2, LANES, LANES), gon, tri)
    return o, sfin.reshape(batch, GLA_HEADS, GLA_DK, GLA_DV)


def _fox_kernel(*refs, seq, past, tq, has_past):
    if has_past:
        (q_ref, kc_ref, vc_ref, lfc_ref, triu_ref, kp_ref, vp_ref, lfp_ref, triup_ref,
         o_ref, cc_scr, cp_scr) = refs
    else:
        q_ref, kc_ref, vc_ref, lfc_ref, triu_ref, o_ref, cc_scr = refs
    i = pl.program_id(2)
    nq = seq // tq
    pblk = FOX_BLOCK
    past_blk = min(past, FOX_BLOCK * FOX_BLOCK // tq)

    @pl.when(i == 0)
    def _():
        carry = jnp.zeros((2, 1), F32)
        if has_past:
            for jb in range(past // pblk):
                x = lfp_ref[0, 0, :, jb * pblk:(jb + 1) * pblk]
                cp_scr[:, jb * pblk:(jb + 1) * pblk] = _dot_exact_rhs(x, triup_ref[...]) + carry
                carry = carry + jnp.sum(x, axis=1, keepdims=True)
        for jb in range(nq):
            x = lfc_ref[0, 0, :, jb * tq:(jb + 1) * tq]
            cc_scr[:, jb * tq:(jb + 1) * tq] = _dot_exact_rhs(x, triu_ref[...]) + carry
            carry = carry + jnp.sum(x, axis=1, keepdims=True)

    q = q_ref[...]
    lane = lax.broadcasted_iota(I32, (tq, LANES), 1)
    qh = [jnp.where(lane < FOX_HD, q, jnp.zeros_like(q)), jnp.where(lane >= FOX_HD, q, jnp.zeros_like(q))]

    def kv_step(state, kb, vb, crows, mask):
        hs = range(2)
        lane_v = lax.broadcasted_iota(I32, vb.shape, 1)
        ones = jnp.ones_like(vb)
        vh = [jnp.where(lane_v < FOX_HD, vb, ones), jnp.where(lane_v >= FOX_HD, vb, ones)]
        s = [_dot_nt(qh[h], kb) * (FOX_SCALE * LOG2E) - crows[h] * LOG2E for h in hs]
        if mask is not None:
            s = [jnp.where(mask, s[h], -jnp.inf) for h in hs]
        m_new = [jnp.maximum(state[h][0], jnp.max(s[h], axis=1, keepdims=True)) for h in hs]
        pv = [_dot(jnp.exp2(s[h] - m_new[h]).astype(BF16), vh[h]) for h in hs]
        alpha = [jnp.exp2(state[h][0] - m_new[h]) for h in hs]
        return tuple((m_new[h], alpha[h] * state[h][1] + pv[h]) for h in hs)

    init = tuple((jnp.full((tq, 1), -jnp.inf, F32), jnp.zeros((tq, LANES), F32)) for _ in range(2))
    state = init
    if has_past:
        def past_body(j, st):
            off = pl.multiple_of(j * past_blk, past_blk)
            kb = kp_ref[0, pl.ds(off, past_blk), :].astype(BF16)
            vb = vp_ref[0, pl.ds(off, past_blk), :].astype(BF16)
            crows = [cp_scr[h:h + 1, pl.ds(off, past_blk)] for h in range(2)]
            return kv_step(st, kb, vb, crows, None)
        state = lax.fori_loop(0, past // past_blk, past_body, state)
    if nq > 1:
        def cur_body(j, st):
            off = pl.multiple_of(j * tq, tq)
            crows = [cc_scr[h:h + 1, pl.ds(off, tq)] for h in range(2)]
            return kv_step(st, kc_ref[pl.ds(off, tq), :], vc_ref[pl.ds(off, tq), :], crows, None)
        state = lax.fori_loop(0, i, cur_body, state)
        off = pl.multiple_of(i * tq, tq)
    else:
        off = 0
    causal = lax.broadcasted_iota(I32, (tq, tq), 1) <= lax.broadcasted_iota(I32, (tq, tq), 0)
    crows = [cc_scr[h:h + 1, pl.ds(off, tq)] for h in range(2)]
    state = kv_step(state, kc_ref[pl.ds(off, tq), :], vc_ref[pl.ds(off, tq), :], crows, causal)
    acc = jnp.where(lane < FOX_HD, state[0][1], state[1][1])
    den = jnp.where(lane < FOX_HD, pltpu.roll(state[0][1], FOX_HD, 1), pltpu.roll(state[1][1], FOX_HD, 1))
    o_ref[...] = (acc / den).astype(BF16)


def _fox(z, lf_cur_t, *, batch, seq, k_past=None, v_past=None, lf_past_t=None):
    has_past = k_past is not None
    past = k_past.shape[1] if has_past else 0
    tq = min(seq, FOX_BLOCK)
    nq = seq // tq
    triu = jnp.triu(jnp.ones((tq, tq), F32)).astype(BF16)
    in_specs = [pl.BlockSpec((tq, LANES), lambda b, p, i: (b * nq + i, ZC_FQ + p)),
                pl.BlockSpec((seq, LANES), lambda b, p, i: (b, ZC_FK + p)),
                pl.BlockSpec((seq, LANES), lambda b, p, i: (b, ZC_FV + p)),
                pl.BlockSpec((1, 1, 2, seq), lambda b, p, i: (b, p, 0, 0)),
                _const_spec((tq, tq))]
    args = [z, z, z, lf_cur_t, triu]
    scratch = [pltpu.VMEM((2, seq), F32)]
    if has_past:
        triup = jnp.triu(jnp.ones((FOX_BLOCK, FOX_BLOCK), F32)).astype(BF16)
        in_specs += [pl.BlockSpec((1, past, LANES), lambda b, p, i: (b, 0, p)),
                     pl.BlockSpec((1, past, LANES), lambda b, p, i: (b, 0, p)),
                     pl.BlockSpec((1, 1, 2, past), lambda b, p, i: (b, p, 0, 0)),
                     _const_spec((FOX_BLOCK, FOX_BLOCK))]
        args += [k_past, v_past, lf_past_t, triup]
        scratch.append(pltpu.VMEM((2, past), F32))
    return pl.pallas_call(
        functools.partial(_fox_kernel, seq=seq, past=past, tq=tq, has_past=has_past),
        grid=(batch, FOX_HEADS // 2, nq),
        in_specs=in_specs,
        out_specs=pl.BlockSpec((tq, LANES), lambda b, p, i: (b * nq + i, p)),
        out_shape=jax.ShapeDtypeStruct((batch * seq, FOX_W), BF16),
        scratch_shapes=scratch,
        compiler_params=_params(("parallel", "parallel", "arbitrary")),
    )(*args)


def _outproj_kernel(og_ref, of_ref, x_ref, w_ref, g2_ref, xo_ref, h2_ref):
    mixed = _dot(og_ref[...], w_ref[0:GLA_V, :]) + _dot(of_ref[...], w_ref[GLA_V:GLA_V + FOX_W, :])
    x = x_ref[...] + mixed
    xo_ref[...] = x
    ms = jnp.mean(x * x, axis=-1, keepdims=True)
    h2_ref[...] = (x * lax.rsqrt(ms + EPS) * g2_ref[...]).astype(BF16)


def _outproj(og, of, x, w, g2):
    n = x.shape[0]
    tm = _tile(n, 512)
    row = lambda i: (i, 0)
    return pl.pallas_call(
        _outproj_kernel,
        grid=(n // tm,),
        in_specs=[pl.BlockSpec((tm, GLA_V), row), pl.BlockSpec((tm, FOX_W), row),
                  pl.BlockSpec((tm, D_MODEL), row), _const_spec(w.shape), _const_spec(g2.shape)],
        out_specs=[pl.BlockSpec((tm, D_MODEL), row), pl.BlockSpec((tm, D_MODEL), row)],
        out_shape=[jax.ShapeDtypeStruct((n, D_MODEL), F32), jax.ShapeDtypeStruct((n, D_MODEL), BF16)],
        compiler_params=_params(("parallel",)),
    )(og, of, x, w, g2)


def _merge_exchange_pairs(n):
    full = 1
    while full < n:
        full *= 2
    pairs = []
    p = 1
    while p < full:
        k = p
        while k >= 1:
            for j in range(k % p, full - k, 2 * k):
                for i in range(min(k, full - j - k)):
                    if (i + j) // (2 * p) == (i + j + k) // (2 * p):
                        pairs.append((i + j, i + j + k))
            k //= 2
        p *= 2
    return tuple((a, b) for a, b in pairs if b < n)


def _topk_keys(s, k):
    nrows, tn = s.shape
    groups = nrows // SUBLANES
    sub = lax.broadcasted_iota(I32, (SUBLANES, tn), 0)
    vals = [s[SUBLANES * r:SUBLANES * (r + 1), :] for r in range(groups)]
    idxs = [sub + SUBLANES * r for r in range(groups)]
    for a, b in _merge_exchange_pairs(groups):
        va, vb, ia, ib = vals[a], vals[b], idxs[a], idxs[b]
        a_first = (va > vb) | ((va == vb) & (ia < ib))
        vals[a], vals[b] = jnp.maximum(va, vb), jnp.minimum(va, vb)
        idxs[a], idxs[b] = jnp.where(a_first, ia, ib), jnp.where(a_first, ib, ia)
    out_v, out_i = [], []
    for t in range(k):
        head_v, head_i = vals[0], idxs[0]
        m = jnp.max(head_v, axis=0, keepdims=True)
        idx = jnp.min(jnp.where(head_v == m, head_i, nrows), axis=0, keepdims=True)
        out_v.append(m)
        out_i.append(idx)
        won = head_i == idx
        for r in range(min(k - 1 - t, groups)):
            last = r + 1 == groups
            vals[r] = jnp.where(won, -jnp.inf if last else vals[r + 1], vals[r])
            idxs[r] = jnp.where(won, nrows if last else idxs[r + 1], idxs[r])
    return jnp.concatenate(out_v, axis=0), jnp.concatenate(out_i, axis=0)


def _pair_topk(v1, i1, v2, i2):
    tn = v1.shape[1]
    iota_k = lax.broadcasted_iota(I32, (PEER_TOPK, tn), 0)
    iota_8 = lax.broadcasted_iota(I32, (8, tn), 0)

    def pick(table, sel):
        rows = []
        for r in range(PEER_TOPK):
            rows.append(jnp.sum(jnp.where(iota_k == sel[r:r + 1, :], table, 0), axis=0, keepdims=True))
        return jnp.concatenate(rows, axis=0)

    blocks = [v1[0:1, :] + v2]
    for a in range(1, 8):
        blocks.append(jnp.where(iota_8 < PEER_TOPK // (a + 1), v1[a:a + 1, :] + v2[0:8, :], -jnp.inf))
    blocks.append(v1[8:16, :] + v2[0:1, :])
    top, ti = _topk_keys(jnp.concatenate(blocks, axis=0), PEER_TOPK)
    mid = ti - PEER_TOPK
    e1 = pick(i1, jnp.where(ti < 16, 0, jnp.where(ti < 72, (mid >> 3) + 1, ti - 64)))
    e2 = pick(i2, jnp.where(ti < 16, ti, jnp.where(ti < 72, mid & 7, 0)))
    ex = jnp.exp(top - jnp.max(top, axis=0, keepdims=True))
    return e1, e2, ex / jnp.sum(ex, axis=0, keepdims=True)


PEER_ROWS = 16
PEER_EC = PEER_ROWS * PEER_NKEYS
assert PEER_NEXP // PEER_EC == PEER_HEADS


def _gelu(a):
    return 0.5 * a * (1.0 + lax.erf(a * (2.0 ** -0.5)))


def _route_act_kernel(hn_ref, hp_ref, wq_ref, sk_ref, ut_ref, e1_ref, e2_ref, c_ref,
                      q_scr, r1_scr, r2_scr, rg_scr, t1_scr, t2_scr, tg_scr, acc_scr):
    i = pl.program_id(0)
    c = pl.program_id(1)
    slot_new = i % 2
    slot_old = 1 - slot_new

    @pl.when((i == 0) & (c == 0))
    def _():
        t1_scr[1] = jnp.zeros_like(t1_scr[1])
        t2_scr[1] = jnp.zeros_like(t2_scr[1])
        tg_scr[1] = jnp.zeros_like(tg_scr[1])

    @pl.when(c == 0)
    def _():
        acc_scr[...] = jnp.zeros_like(acc_scr)
        q = _dot(hn_ref[...], wq_ref[...])
        for hc in range(2 * PEER_HEADS):
            q_scr[hc] = q[:, hc * LANES:(hc + 1) * LANES].astype(BF16)

    v1, i1 = _topk_keys(_dot_nt(sk_ref[2 * c], q_scr[2 * c]), PEER_TOPK)
    v2, i2 = _topk_keys(_dot_nt(sk_ref[2 * c + 1], q_scr[2 * c + 1]), PEER_TOPK)
    e1h, e2h, gh = _pair_topk(v1, i1, v2, i2)
    rows = pl.ds(pl.multiple_of(c * PEER_TOPK, PEER_TOPK), PEER_TOPK)
    r1_scr[rows, :] = e1h
    r2_scr[rows, :] = e2h
    rg_scr[rows, :] = gh

    a_all = _dot(hp_ref[...], ut_ref[...])
    e1 = t1_scr[slot_old]
    e2 = t2_scr[slot_old]
    acc = acc_scr[...]
    for r in range(PEER_ROWS):
        picked = jnp.take_along_axis(a_all[:, r * LANES:(r + 1) * LANES], e2, axis=1,
                                     mode="promise_in_bounds")
        acc = jnp.where(e1 == c * PEER_ROWS + r, picked, acc)
    acc_scr[...] = acc

    @pl.when(c == pl.num_programs(1) - 1)
    def _():
        c_ref[...] = tg_scr[slot_old] * _gelu(acc)
        t1_scr[slot_new] = r1_scr[...].T
        t2_scr[slot_new] = r2_scr[...].T
        tg_scr[slot_new] = rg_scr[...].T
        e1_ref[...] = t1_scr[slot_new]
        e2_ref[...] = t2_scr[slot_new]


def _route_act(h2, wq, sk, ut):
    n = h2.shape[0]
    tn = _tile(n, 512)
    tiles = n // tn
    new = lambda i, c: (jnp.minimum(i, tiles - 1), 0)
    old = lambda i, c: (jnp.maximum(i - 1, 0), 0)
    return pl.pallas_call(
        _route_act_kernel,
        grid=(tiles + 1, PEER_HEADS),
        in_specs=[pl.BlockSpec((tn, D_MODEL), new), pl.BlockSpec((tn, D_MODEL), old),
                  _const_spec(wq.shape), _const_spec(sk.shape),
                  pl.BlockSpec((D_MODEL, PEER_EC), lambda i, c: (0, c))],
        out_specs=[pl.BlockSpec((tn, PEER_J), new), pl.BlockSpec((tn, PEER_J), new),
                   pl.BlockSpec((tn, PEER_J), old)],
        out_shape=[jax.ShapeDtypeStruct((n, PEER_J), I32), jax.ShapeDtypeStruct((n, PEER_J), I32),
                   jax.ShapeDtypeStruct((n, PEER_J), F32)],
        scratch_shapes=[pltpu.VMEM((2 * PEER_HEADS, tn, LANES), BF16),
                        pltpu.VMEM((PEER_J, tn), I32), pltpu.VMEM((PEER_J, tn), I32),
                        pltpu.VMEM((PEER_J, tn), F32),
                        pltpu.VMEM((2, tn, PEER_J), I32), pltpu.VMEM((2, tn, PEER_J), I32),
                        pltpu.VMEM((2, tn, PEER_J), F32), pltpu.VMEM((tn, PEER_J), F32)],
        compiler_params=_params(("arbitrary", "arbitrary")),
    )(h2, h2, wq, sk, ut)


PEER_HALF = PEER_ROWS // 2


def _peer_out_kernel(e1_ref, e2_ref, c_ref, v_ref, x_ref, o_ref, y_scr, acc_scr):
    tn = x_ref.shape[0]
    c = pl.program_id(1)
    nchunks = PEER_NEXP // PEER_EC

    @pl.when(c == 0)
    def _():
        acc_scr[...] = jnp.zeros_like(acc_scr)
        key = lax.broadcasted_iota(I32, (PEER_NKEYS, PEER_J), 0)

        def token(t, carry):
            row = pl.ds(t, 1)
            d = jnp.where(key == e1_ref[row, :], c_ref[row, :], 0.0).astype(BF16)
            w = jnp.where(key == e2_ref[row, :], 1.0, 0.0).astype(BF16)
            y = _dot_nt(d, w).astype(BF16).astype(F32)
            bits = lax.bitcast_convert_type(y, I32)
            base = pl.multiple_of(t * PEER_HALF, PEER_HALF)
            for cc in range(nchunks):
                hi = bits[cc * PEER_ROWS:cc * PEER_ROWS + PEER_HALF, :]
                lo = bits[cc * PEER_ROWS + PEER_HALF:(cc + 1) * PEER_ROWS, :]
                y_scr[cc, pl.ds(base, PEER_HALF), :] = hi | lax.shift_right_logical(lo, 16)
            return carry

        lax.fori_loop(0, tn, token, 0, unroll=32)

    his, los = [], []
    for i in range(PEER_HALF):
        words = y_scr[c, pl.ds(i, tn, stride=PEER_HALF), :]
        his.append(lax.bitcast_convert_type(words & -65536, F32).astype(BF16))
        los.append(lax.bitcast_convert_type(words << 16, F32).astype(BF16))
    acc_scr[...] += _dot(jnp.concatenate(his + los, axis=1), v_ref[...])

    @pl.when(c == nchunks - 1)
    def _():
        o_ref[...] = x_ref[...] + acc_scr[...]


def _peer_out(e1, e2, cj, vt, x):
    n = x.shape[0]
    tn = _tile(n, 512)
    row = lambda i, c: (i, 0)
    nchunks = PEER_NEXP // PEER_EC
    return pl.pallas_call(
        _peer_out_kernel,
        grid=(n // tn, nchunks),
        in_specs=[pl.BlockSpec((tn, PEER_J), row), pl.BlockSpec((tn, PEER_J), row),
                  pl.BlockSpec((tn, PEER_J), row), pl.BlockSpec((PEER_EC, D_MODEL), lambda i, c: (c, 0)),
                  pl.BlockSpec((tn, D_MODEL), row)],
        out_specs=pl.BlockSpec((tn, D_MODEL), row),
        out_shape=jax.ShapeDtypeStruct((n, D_MODEL), F32),
        scratch_shapes=[pltpu.VMEM((nchunks, tn * PEER_HALF, LANES), I32),
                        pltpu.VMEM((tn, D_MODEL), F32)],
        compiler_params=_params(("parallel", "arbitrary")),
    )(e1, e2, cj, vt, x)


def _pair_major(lf, batch, length):
    return lf.reshape(batch, length, FOX_HEADS // 2, 2).transpose(0, 2, 3, 1)


def _prep_weights(g_norm1, w_in, w_gla_a2, b_gla_a, g_gla_onorm, g_fox_qnorm, g_fox_knorm, b_fox_f, w_out,
                  g_norm2, w_peer_q, peer_subkeys, peer_u, peer_v):
    bounds = [0]
    for size in IN_SIZES:
        bounds.append(bounds[-1] + size)
    gq, gk, gv, gg, glr, fq, fk, fv, ff = [w_in[:, bounds[i]:bounds[i + 1]] for i in range(len(IN_SIZES))]
    pad = jnp.zeros((D_MODEL, Z_SMALL - FOX_HEADS - GLA_LOWRANK), w_in.dtype)
    w_cat = jnp.concatenate([gq, gk, gv, gg, fq, fk, fv, ff, glr, pad], axis=1).astype(BF16)
    wa2p = jnp.zeros((Z_SMALL, GLA_QK), F32).at[FOX_HEADS:FOX_HEADS + GLA_LOWRANK].set(w_gla_a2).astype(BF16)
    bsm = jnp.zeros((1, Z_SMALL), F32).at[0, :FOX_HEADS].set(b_fox_f)
    head_of_col = jnp.arange(FOX_W) // FOX_HD
    ind = (head_of_col[:, None] == jnp.arange(LANES)[None, :]).astype(BF16)
    return dict(
        inproj=(g_norm1[None], w_cat, wa2p, b_gla_a[None], bsm, jnp.tile(g_fox_qnorm, FOX_HEADS)[None],
                jnp.tile(g_fox_knorm, FOX_HEADS)[None], ind, ind.T),
        gon=g_gla_onorm[None], w_out=w_out.astype(BF16), g2=g_norm2[None], wq=w_peer_q.astype(BF16),
        sk=peer_subkeys.reshape(2 * PEER_HEADS, PEER_NKEYS, LANES).astype(BF16),
        ut=peer_u.T.astype(BF16), v=peer_v.astype(BF16))


def _path_layer(x, wts, layer, depth, stacked, batch, seq, s0, k_past=None, v_past=None, lf_past=None):
    z, la, k_st, v_st, lf_st = _inproj(x, wts["inproj"], layer, depth, stacked)
    og, state = _gla(z, la, s0, wts["gon"], batch=batch, seq=seq)
    past_args = {}
    if k_past is not None:
        past = k_past.shape[1]
        past_args = dict(k_past=k_past.reshape(batch, past, FOX_W),
                         v_past=v_past.reshape(batch, past, FOX_W),
                         lf_past_t=lf_past.reshape(batch, past, FOX_HEADS // 2, 2).transpose(0, 2, 3, 1))
    of = _fox(z, _pair_major(lf_st[layer], batch, seq), batch=batch, seq=seq, **past_args)
    x, h2 = _outproj(og, of, x, wts["w_out"], wts["g2"])
    e1, e2, cj = _route_act(h2, wts["wq"], wts["sk"], wts["ut"])
    x = _peer_out(e1, e2, cj, wts["v"], x)
    return x, (k_st, v_st, lf_st), state


def kernel(x_prompt, x_sample, cache_fox_k, cache_fox_v, cache_fox_logf, state_gla, g_norm1, w_in, w_gla_a2,
           b_gla_a, g_gla_onorm, g_fox_qnorm, g_fox_knorm, b_fox_f, w_out, g_norm2, w_peer_q, peer_subkeys,
           peer_u, peer_v):
    bp, tp, _ = x_prompt.shape
    bs, ts, _ = x_sample.shape
    depth = w_in.shape[0]
    xp = x_prompt.reshape(bp * tp, D_MODEL)
    xs = x_sample.reshape(bs * ts, D_MODEL)
    stk_p, stk_s, st_p, st_s = (), (), [], []
    zero_state = jnp.zeros((bp, GLA_HEADS, GLA_DK, GLA_DV), F32)
    for l in range(depth):
        wts = _prep_weights(g_norm1[l], w_in[l], w_gla_a2[l], b_gla_a[l], g_gla_onorm[l], g_fox_qnorm[l],
                            g_fox_knorm[l], b_fox_f[l], w_out[l], g_norm2[l], w_peer_q[l], peer_subkeys[l],
                            peer_u[l], peer_v[l])
        xp, stk_p, state = _path_layer(xp, wts, l, depth, stk_p, bp, tp, zero_state)
        st_p.append(state)
        xs, stk_s, state = _path_layer(xs, wts, l, depth, stk_s, bs, ts, state_gla[l],
                                       cache_fox_k[l], cache_fox_v[l], cache_fox_logf[l])
        st_s.append(state)

    def unstack(stk, b, t):
        k, v, lf = stk
        return (k.reshape(depth, b, t, FOX_HEADS, FOX_HD), v.reshape(depth, b, t, FOX_HEADS, FOX_HD),
                lf.reshape(depth, b, t, FOX_HEADS))

    return (xp.reshape(bp, tp, D_MODEL), xs.reshape(bs, ts, D_MODEL),
            *unstack(stk_p, bp, tp), jnp.stack(st_p), *unstack(stk_s, bs, ts), jnp.stack(st_s))
```

```python
import functools

import jax
import jax.numpy as jnp
from jax import lax
from jax.experimental import pallas as pl
from jax.experimental.pallas import tpu as pltpu

F32 = jnp.float32
BF16 = jnp.bfloat16
I32 = jnp.int32

D_MODEL = 1024
EPS = 1e-6
GLA_CHUNK = 64
GLA_HEADS = 4
GLA_DK = 64
GLA_DV = 128
GLA_LOWRANK = 16
GLA_GATE_NORM = 16.0
GLA_SUB = 16
FOX_HEADS = 8
FOX_HD = 64
FOX_SCALE = FOX_HD ** -0.5
LOG2E = 1.4426950408889634
FOX_BLOCK = 512
GLA_QK = GLA_HEADS * GLA_DK
GLA_V = GLA_HEADS * GLA_DV
FOX_W = FOX_HEADS * FOX_HD
IN_SIZES = (GLA_QK, GLA_QK, GLA_V, GLA_V, GLA_LOWRANK, FOX_W, FOX_W, FOX_W, FOX_HEADS)
PEER_HEADS = 8
PEER_NKEYS = 128
PEER_TOPK = 16
PEER_NEXP = PEER_NKEYS * PEER_NKEYS
PEER_J = PEER_HEADS * PEER_TOPK

LANES = 128
SUBLANES = 8
Z_MAIN = 3072
Z_SMALL = 128
VMEM_LIMIT = 56 * 1024 * 1024

ZC_GQ, ZC_GK, ZC_GV, ZC_GG, ZC_FQ, ZC_FK, ZC_FV = 0, 2, 4, 8, 12, 16, 20


def _params(sem):
    return pltpu.CompilerParams(dimension_semantics=sem, vmem_limit_bytes=VMEM_LIMIT)


def _tile(n, cap):
    t = cap
    while n % t:
        t //= 2
    return t


def _dot(a, b):
    return jnp.dot(a, b, preferred_element_type=F32)


def _dot_nt(a, b):
    return lax.dot_general(a, b, (((1,), (1,)), ((), ())), preferred_element_type=F32)


def _dot_tn(a, b):
    return lax.dot_general(a, b, (((0,), (0,)), ((), ())), preferred_element_type=F32)


def _split(x):
    hi = x.astype(BF16)
    lo = (x - hi.astype(F32)).astype(BF16)
    return hi, lo


def _dot_exact_rhs(x, m):
    hi, lo = _split(x)
    return _dot(hi, m) + _dot(lo, m)


def _dot_exact_lhs(m, x):
    hi, lo = _split(x)
    return _dot(m, hi) + _dot(m, lo)


def _log_sigmoid(y):
    return jnp.minimum(y, 0.0) - jnp.log(1.0 + jnp.exp(-jnp.abs(y)))


def _const_spec(shape):
    nd = len(shape)
    return pl.BlockSpec(shape, lambda *_: (0,) * nd)


def _inproj_kernel(*refs, n_alias, layer):
    (x_ref, g1_ref, w_ref, wa2_ref, ba_ref, bsm_ref, gq_ref, gk_ref, ind_ref, indt_ref) = refs[:10]
    z_ref, la_ref, k32_ref, v32_ref, lf_ref = refs[10 + n_alias:]
    for stack_ref in (k32_ref, v32_ref, lf_ref):
        for other in range(stack_ref.shape[0]):
            if other != layer:
                stack_ref[other] = jnp.zeros(stack_ref.shape[1:], F32)
    x = x_ref[...]
    ms = jnp.mean(x * x, axis=-1, keepdims=True)
    h = (x * lax.rsqrt(ms + EPS) * g1_ref[...]).astype(BF16)
    z = _dot(h, w_ref[...])

    def headnorm(t, gain):
        ss = _dot_exact_rhs(t * t, ind_ref[...])
        r = lax.rsqrt(ss * (1.0 / FOX_HD) + EPS)
        return t * _dot_exact_rhs(r, indt_ref[...]) * gain

    c_fq, c_fk, c_fv = ZC_FQ * LANES, ZC_FK * LANES, ZC_FV * LANES
    fq = headnorm(z[:, c_fq:c_fk], gq_ref[...])
    fk = headnorm(z[:, c_fk:c_fv], gk_ref[...])
    fv = z[:, c_fv:Z_MAIN]
    z_ref[:, 0:c_fq] = z[:, 0:c_fq].astype(BF16)
    z_ref[:, c_fq:c_fk] = fq.astype(BF16)
    z_ref[:, c_fk:c_fv] = fk.astype(BF16)
    z_ref[:, c_fv:Z_MAIN] = fv.astype(BF16)
    tm = x_ref.shape[0]
    for hd in range(FOX_HEADS):
        token_rows = pl.ds(hd, tm, stride=FOX_HEADS)
        k32_ref[layer, token_rows, :] = fk[:, hd * FOX_HD:(hd + 1) * FOX_HD]
        v32_ref[layer, token_rows, :] = fv[:, hd * FOX_HD:(hd + 1) * FOX_HD]
    small = z[:, Z_MAIN:Z_MAIN + Z_SMALL]
    lf_ref[layer] = _log_sigmoid(small + bsm_ref[...])[:, 0:FOX_HEADS]
    y = _dot(small.astype(BF16), wa2_ref[...]) + ba_ref[...]
    la_ref[...] = _log_sigmoid(y) * (1.0 / GLA_GATE_NORM)


def _inproj(x, consts, layer, depth, stacked):
    n = x.shape[0]
    tm = _tile(n, 512)
    row = lambda i: (i, 0)
    n_alias = len(stacked)
    slabs, slab = (1, lambda i: (layer, i, 0)) if n_alias else (depth, lambda i: (0, i, 0))
    kv_block = (slabs, tm * FOX_HEADS, FOX_HD)
    stack_shapes = ((depth, n * FOX_HEADS, FOX_HD), (depth, n * FOX_HEADS, FOX_HD), (depth, n, FOX_HEADS))
    return pl.pallas_call(
        functools.partial(_inproj_kernel, n_alias=n_alias, layer=0 if n_alias else layer),
        grid=(n // tm,),
        in_specs=[pl.BlockSpec((tm, D_MODEL), row)] + [_const_spec(c.shape) for c in consts]
        + [pl.BlockSpec(memory_space=pl.ANY)] * n_alias,
        out_specs=[pl.BlockSpec((tm, Z_MAIN), row), pl.BlockSpec((tm, GLA_QK), row)]
        + [pl.BlockSpec(kv_block, slab), pl.BlockSpec(kv_block, slab),
           pl.BlockSpec((slabs, tm, FOX_HEADS), slab)],
        out_shape=[jax.ShapeDtypeStruct((n, Z_MAIN), BF16), jax.ShapeDtypeStruct((n, GLA_QK), F32)]
        + [jax.ShapeDtypeStruct(shape, F32) for shape in stack_shapes],
        input_output_aliases={1 + len(consts) + i: 2 + i for i in range(n_alias)},
        compiler_params=_params(("parallel",)),
    )(x, *consts, *stacked)


def _gla_kernel(q_ref, k_ref, v_ref, gg_ref, la_ref, s0_ref, gon_ref, tri_ref, o_ref, sfin_ref, st_scr,
                *, chunk, nchunk):
    t = pl.program_id(1)
    nsub = chunk // GLA_SUB

    @pl.when(t == 0)
    def _():
        for p in range(2):
            st_scr[p] = s0_ref[0, p].T

    rowi = lax.broadcasted_iota(I32, (chunk, LANES), 0)
    lane = lax.broadcasted_iota(I32, (chunk, LANES), 1)
    lane_st = lax.broadcasted_iota(I32, (chunk, nsub * LANES), 1)
    lane_sq = lax.broadcasted_iota(I32, (LANES, LANES), 1)
    arow = lax.broadcasted_iota(I32, (chunk, chunk), 0)
    acol = lax.broadcasted_iota(I32, (chunk, chunk), 1)

    per_trip = next(n for n in (4, 2, 1) if nchunk % n == 0)

    def body(trip, carry):
        rows_u, prep = [], []
        for u in range(per_trip):
            rows = pl.ds(pl.multiple_of((trip * per_trip + u) * chunk, chunk), chunk)
            rows_u.append(rows)
            bcum = _dot_exact_lhs(tri_ref[...], la_ref[rows, :])
            q = q_ref[rows, :].astype(F32) * (GLA_DK ** -0.5)
            k = k_ref[rows, :].astype(F32)
            for p in range(2):
                cols = slice(p * LANES, (p + 1) * LANES)
                bp, qp, kp = bcum[:, cols], q[:, cols], k[:, cols]
                blast = bp[chunk - 1:chunk, :]
                qs, ks = [], []
                for sb in range(nsub):
                    beta = bp[sb * GLA_SUB - 1:sb * GLA_SUB, :] if sb else jnp.zeros((1, LANES), F32)
                    inblk = (rowi >= sb * GLA_SUB) & (rowi < (sb + 1) * GLA_SUB)
                    qs.append(jnp.where(inblk, qp * jnp.exp(jnp.where(inblk, bp - beta, 0.0)), 0.0))
                    valid = rowi < (sb + 1) * GLA_SUB
                    ks.append(jnp.where(valid, kp * jnp.exp(jnp.where(valid, beta - bp, 0.0)), 0.0))
                prep.append(dict(qst=jnp.concatenate(qs, axis=1), kst=jnp.concatenate(ks, axis=1).astype(BF16),
                                 qinter=qp * jnp.exp(bp), kdec=(kp * jnp.exp(blast - bp)).astype(BF16),
                                 decay=jnp.exp(blast)))
        heads = [(u, p, i) for u in range(per_trip) for p in range(2) for i in range(2)]
        hcols = [slice((2 * p + i) * GLA_DV, (2 * p + i + 1) * GLA_DV) for _, p, i in heads]
        vhs = [v_ref[rows_u[u], hc] for (u, _, _), hc in zip(heads, hcols)]
        a_s = [_dot_nt(jnp.where((lane_st & GLA_DK) == i * GLA_DK, prep[2 * u + p]["qst"], 0.0).astype(BF16),
                       prep[2 * u + p]["kst"]) for u, p, i in heads]
        upd = [_dot_tn(vh, prep[2 * u + p]["kdec"]) for (u, p, i), vh in zip(heads, vhs)]
        a_s = [jnp.where(acol <= arow, a, 0.0).astype(BF16) for a in a_s]
        intra = [_dot(a, vh) for a, vh in zip(a_s, vhs)]
        states = [st_scr[p] for p in range(2)]
        for u in range(per_trip):
            for p in range(2):
                new = states[2 * u + p] * prep[2 * u + p]["decay"] + jnp.where(
                    lane_sq < GLA_DK, upd[4 * u + 2 * p], upd[4 * u + 2 * p + 1])
                states.append(new)
        for p in range(2):
            st_scr[p] = states[2 * per_trip + p]
        stb = [st.astype(BF16) for st in states[:2 * per_trip]]
        o_s = [o_in + _dot_nt(jnp.where((lane & GLA_DK) == i * GLA_DK, prep[2 * u + p]["qinter"], 0.0)
                              .astype(BF16), stb[2 * u + p]) for (u, p, i), o_in in zip(heads, intra)]
        for (u, _, _), o, hc in zip(heads, o_s, hcols):
            on = o * lax.rsqrt(jnp.mean(o * o, axis=-1, keepdims=True) + EPS) * gon_ref[...]
            gate = gg_ref[rows_u[u], hc].astype(F32)
            o_ref[rows_u[u], hc] = (on * gate * (1.0 / (1.0 + jnp.exp(-gate)))).astype(BF16)
        return carry

    lax.fori_loop(0, nchunk // per_trip, body, 0)

    @pl.when(t == pl.num_programs(1) - 1)
    def _():
        for p in range(2):
            sfin_ref[0, p] = st_scr[p].T


def _gla(z, la, s0, gon, *, batch, seq):
    chunk = min(seq, GLA_CHUNK)
    tb = min(seq, 512)
    nt = seq // tb
    tri = jnp.tril(jnp.ones((chunk, chunk), F32)).astype(BF16)
    qk_w, v_w = GLA_QK, GLA_V
    rows = lambda col: (lambda b, t: (b * nt + t, col))
    state_spec = pl.BlockSpec((1, 2, LANES, LANES), lambda b, t: (b, 0, 0, 0))
    o, sfin = pl.pallas_call(
        functools.partial(_gla_kernel, chunk=chunk, nchunk=tb // chunk),
        grid=(batch, nt),
        in_specs=[pl.BlockSpec((tb, qk_w), rows(ZC_GQ * LANES // qk_w)),
                  pl.BlockSpec((tb, qk_w), rows(ZC_GK * LANES // qk_w)),
                  pl.BlockSpec((tb, v_w), rows(ZC_GV * LANES // v_w)),
                  pl.BlockSpec((tb, v_w), rows(ZC_GG * LANES // v_w)),
                  pl.BlockSpec((tb, qk_w), rows(0)), state_spec,
                  _const_spec((1, GLA_DV)), _const_spec((chunk, chunk))],
        out_specs=[pl.BlockSpec((tb, v_w), lambda b, t: (b * nt + t, 0)), state_spec],
        out_shape=[jax.ShapeDtypeStruct((batch * seq, v_w), BF16),
                   jax.ShapeDtypeStruct((batch, 2, LANES, LANES), F32)],
        scratch_shapes=[pltpu.VMEM((2, LANES, LANES), F32)],
        compiler_params=_params(("parallel", "arbitrary")),
    )(z, z, z, z, la, s0.reshape(batch, 2, LANES, LANES), gon, tri)
    return o, sfin.reshape(batch, GLA_HEADS, GLA_DK, GLA_DV)


def _fox_kernel(*refs, seq, past, tq, has_past):
    if has_past:
        (q_ref, kc_ref, vc_ref, lfc_ref, triu_ref, kp_ref, vp_ref, lfp_ref, triup_ref,
         o_ref, cc_scr, cp_scr) = refs
    else:
        q_ref, kc_ref, vc_ref, lfc_ref, triu_ref, o_ref, cc_scr = refs
    i = pl.program_id(2)
    nq = seq // tq
    pblk = FOX_BLOCK
    past_blk = min(past, FOX_BLOCK * FOX_BLOCK // tq)

    @pl.when(i == 0)
    def _():
        carry = jnp.zeros((2, 1), F32)
        if has_past:
            for jb in range(past // pblk):
                x = lfp_ref[0, 0, :, jb * pblk:(jb + 1) * pblk]
                cp_scr[:, jb * pblk:(jb + 1) * pblk] = _dot_exact_rhs(x, triup_ref[...]) + carry
                carry = carry + jnp.sum(x, axis=1, keepdims=True)
        for jb in range(nq):
            x = lfc_ref[0, 0, :, jb * tq:(jb + 1) * tq]
            cc_scr[:, jb * tq:(jb + 1) * tq] = _dot_exact_rhs(x, triu_ref[...]) + carry
            carry = carry + jnp.sum(x, axis=1, keepdims=True)

    q = q_ref[...]
    lane = lax.broadcasted_iota(I32, (tq, LANES), 1)
    qh = [jnp.where(lane < FOX_HD, q, jnp.zeros_like(q)), jnp.where(lane >= FOX_HD, q, jnp.zeros_like(q))]

    causal = lax.broadcasted_iota(I32, (tq, tq), 1) <= lax.broadcasted_iota(I32, (tq, tq), 0)

    def attend(blocks):
        hs = range(2)
        s = [[_dot_nt(qh[h], kb) * (FOX_SCALE * LOG2E) - crows[h] * LOG2E for kb, _, crows, _ in blocks]
             for h in hs]
        s = [[s_b if blk[3] is None else jnp.where(blk[3], s_b, -jnp.inf) for s_b, blk in zip(s[h], blocks)]
             for h in hs]
        m = [functools.reduce(jnp.maximum, [jnp.max(s_b, axis=1, keepdims=True) for s_b in s[h]]) for h in hs]
        acc = []
        for h in hs:
            total = None
            for s_b, (_, vb, _, _) in zip(s[h], blocks):
                lane_v = lax.broadcasted_iota(I32, vb.shape, 1)
                own = (lane_v < FOX_HD) if h == 0 else (lane_v >= FOX_HD)
                pv = _dot(jnp.exp2(s_b - m[h]).astype(BF16), jnp.where(own, vb, jnp.ones_like(vb)))
                total = pv if total is None else total + pv
            acc.append(total)
        num = jnp.where(lane < FOX_HD, acc[0], acc[1])
        den = jnp.where(lane < FOX_HD, pltpu.roll(acc[0], FOX_HD, 1), pltpu.roll(acc[1], FOX_HD, 1))
        o_ref[...] = (num / den).astype(BF16)

    past_blocks = []
    if has_past:
        for off in range(0, past, past_blk):
            past_blocks.append((kp_ref[0, off:off + past_blk, :].astype(BF16),
                                vp_ref[0, off:off + past_blk, :].astype(BF16),
                                [cp_scr[h:h + 1, off:off + past_blk] for h in range(2)], None))

    def cur_block(j, mask):
        rows = slice(j * tq, (j + 1) * tq)
        return (kc_ref[rows, :], vc_ref[rows, :], [cc_scr[h:h + 1, rows] for h in range(2)], mask)

    for qi in range(nq):
        @pl.when(i == qi)
        def _(qi=qi):
            attend(past_blocks + [cur_block(j, None) for j in range(qi)] + [cur_block(qi, causal)])


def _fox(z, lf_cur_t, *, batch, seq, k_past=None, v_past=None, lf_past_t=None):
    has_past = k_past is not None
    past = k_past.shape[1] if has_past else 0
    tq = min(seq, FOX_BLOCK)
    nq = seq // tq
    triu = jnp.triu(jnp.ones((tq, tq), F32)).astype(BF16)
    in_specs = [pl.BlockSpec((tq, LANES), lambda b, p, i: (b * nq + i, ZC_FQ + p)),
                pl.BlockSpec((seq, LANES), lambda b, p, i: (b, ZC_FK + p)),
                pl.BlockSpec((seq, LANES), lambda b, p, i: (b, ZC_FV + p)),
                pl.BlockSpec((1, 1, 2, seq), lambda b, p, i: (b, p, 0, 0)),
                _const_spec((tq, tq))]
    args = [z, z, z, lf_cur_t, triu]
    scratch = [pltpu.VMEM((2, seq), F32)]
    if has_past:
        triup = jnp.triu(jnp.ones((FOX_BLOCK, FOX_BLOCK), F32)).astype(BF16)
        in_specs += [pl.BlockSpec((1, past, LANES), lambda b, p, i: (b, 0, p)),
                     pl.BlockSpec((1, past, LANES), lambda b, p, i: (b, 0, p)),
                     pl.BlockSpec((1, 1, 2, past), lambda b, p, i: (b, p, 0, 0)),
                     _const_spec((FOX_BLOCK, FOX_BLOCK))]
        args += [k_past, v_past, lf_past_t, triup]
        scratch.append(pltpu.VMEM((2, past), F32))
    return pl.pallas_call(
        functools.partial(_fox_kernel, seq=seq, past=past, tq=tq, has_past=has_past),
        grid=(batch, FOX_HEADS // 2, nq),
        in_specs=in_specs,
        out_specs=pl.BlockSpec((tq, LANES), lambda b, p, i: (b * nq + i, p)),
        out_shape=jax.ShapeDtypeStruct((batch * seq, FOX_W), BF16),
        scratch_shapes=scratch,
        compiler_params=_params(("parallel", "parallel", "arbitrary")),
    )(*args)


def _outproj_kernel(og_ref, of_ref, x_ref, w_ref, g2_ref, xo_ref, h2_ref):
    mixed = _dot(og_ref[...], w_ref[0:GLA_V, :]) + _dot(of_ref[...], w_ref[GLA_V:GLA_V + FOX_W, :])
    x = x_ref[...] + mixed
    xo_ref[...] = x
    ms = jnp.mean(x * x, axis=-1, keepdims=True)
    h2_ref[...] = (x * lax.rsqrt(ms + EPS) * g2_ref[...]).astype(BF16)


def _outproj(og, of, x, w, g2):
    n = x.shape[0]
    tm = _tile(n, 512)
    row = lambda i: (i, 0)
    return pl.pallas_call(
        _outproj_kernel,
        grid=(n // tm,),
        in_specs=[pl.BlockSpec((tm, GLA_V), row), pl.BlockSpec((tm, FOX_W), row),
                  pl.BlockSpec((tm, D_MODEL), row), _const_spec(w.shape), _const_spec(g2.shape)],
        out_specs=[pl.BlockSpec((tm, D_MODEL), row), pl.BlockSpec((tm, D_MODEL), row)],
        out_shape=[jax.ShapeDtypeStruct((n, D_MODEL), F32), jax.ShapeDtypeStruct((n, D_MODEL), BF16)],
        compiler_params=_params(("parallel",)),
    )(og, of, x, w, g2)


def _merge_exchange_pairs(n):
    full = 1
    while full < n:
        full *= 2
    pairs = []
    p = 1
    while p < full:
        k = p
        while k >= 1:
            for j in range(k % p, full - k, 2 * k):
                for i in range(min(k, full - j - k)):
                    if (i + j) // (2 * p) == (i + j + k) // (2 * p):
                        pairs.append((i + j, i + j + k))
            k //= 2
        p *= 2
    return tuple((a, b) for a, b in pairs if b < n)


def _topk_keys(s, k):
    nrows, tn = s.shape
    groups = nrows // SUBLANES
    sub = lax.broadcasted_iota(I32, (SUBLANES, tn), 0)
    vals = [s[SUBLANES * r:SUBLANES * (r + 1), :] for r in range(groups)]
    idxs = [sub + SUBLANES * r for r in range(groups)]
    for a, b in _merge_exchange_pairs(groups):
        va, vb, ia, ib = vals[a], vals[b], idxs[a], idxs[b]
        a_first = (va > vb) | ((va == vb) & (ia < ib))
        vals[a], vals[b] = jnp.maximum(va, vb), jnp.minimum(va, vb)
        idxs[a], idxs[b] = jnp.where(a_first, ia, ib), jnp.where(a_first, ib, ia)
    out_v, out_i = [], []
    for t in range(k):
        head_v, head_i = vals[0], idxs[0]
        m = jnp.max(head_v, axis=0, keepdims=True)
        idx = jnp.min(jnp.where(head_v == m, head_i, nrows), axis=0, keepdims=True)
        out_v.append(m)
        out_i.append(idx)
        won = head_i == idx
        for r in range(min(k - 1 - t, groups)):
            last = r + 1 == groups
            vals[r] = jnp.where(won, -jnp.inf if last else vals[r + 1], vals[r])
            idxs[r] = jnp.where(won, nrows if last else idxs[r + 1], idxs[r])
    return jnp.concatenate(out_v, axis=0), jnp.concatenate(out_i, axis=0)


def _pair_topk(v1, i1, v2, i2):
    tn = v1.shape[1]
    iota_k = lax.broadcasted_iota(I32, (PEER_TOPK, tn), 0)
    iota_8 = lax.broadcasted_iota(I32, (8, tn), 0)

    def pick(table, sel):
        rows = []
        for r in range(PEER_TOPK):
            rows.append(jnp.sum(jnp.where(iota_k == sel[r:r + 1, :], table, 0), axis=0, keepdims=True))
        return jnp.concatenate(rows, axis=0)

    blocks = [v1[0:1, :] + v2]
    for a in range(1, 8):
        blocks.append(jnp.where(iota_8 < PEER_TOPK // (a + 1), v1[a:a + 1, :] + v2[0:8, :], -jnp.inf))
    blocks.append(v1[8:16, :] + v2[0:1, :])
    top, ti = _topk_keys(jnp.concatenate(blocks, axis=0), PEER_TOPK)
    mid = ti - PEER_TOPK
    e1 = pick(i1, jnp.where(ti < 16, 0, jnp.where(ti < 72, (mid >> 3) + 1, ti - 64)))
    e2 = pick(i2, jnp.where(ti < 16, ti, jnp.where(ti < 72, mid & 7, 0)))
    ex = jnp.exp(top - jnp.max(top, axis=0, keepdims=True))
    return e1, e2, ex / jnp.sum(ex, axis=0, keepdims=True)


PEER_ROWS = 16
PEER_EC = PEER_ROWS * PEER_NKEYS
assert PEER_NEXP // PEER_EC == PEER_HEADS


def _gelu(a):
    return 0.5 * a * (1.0 + lax.erf(a * (2.0 ** -0.5)))


def _route_act_kernel(hn_ref, hp_ref, wq_ref, sk_ref, ut_ref, e1_ref, e2_ref, c_ref,
                      q_scr, r1_scr, r2_scr, rg_scr, t1_scr, t2_scr, tg_scr, acc_scr):
    i = pl.program_id(0)
    c = pl.program_id(1)
    slot_new = i % 2
    slot_old = 1 - slot_new

    @pl.when((i == 0) & (c == 0))
    def _():
        t1_scr[1] = jnp.zeros_like(t1_scr[1])
        t2_scr[1] = jnp.zeros_like(t2_scr[1])
        tg_scr[1] = jnp.zeros_like(tg_scr[1])

    @pl.when(c == 0)
    def _():
        acc_scr[...] = jnp.zeros_like(acc_scr)
        q = _dot(hn_ref[...], wq_ref[...])
        for hc in range(2 * PEER_HEADS):
            q_scr[hc] = q[:, hc * LANES:(hc + 1) * LANES].astype(BF16)

    v1, i1 = _topk_keys(_dot_nt(sk_ref[2 * c], q_scr[2 * c]), PEER_TOPK)
    v2, i2 = _topk_keys(_dot_nt(sk_ref[2 * c + 1], q_scr[2 * c + 1]), PEER_TOPK)
    e1h, e2h, gh = _pair_topk(v1, i1, v2, i2)
    rows = pl.ds(pl.multiple_of(c * PEER_TOPK, PEER_TOPK), PEER_TOPK)
    r1_scr[rows, :] = e1h
    r2_scr[rows, :] = e2h
    rg_scr[rows, :] = gh

    a_all = _dot(hp_ref[...], ut_ref[...])
    e1 = t1_scr[slot_old]
    e2 = t2_scr[slot_old]
    acc = acc_scr[...]
    for r in range(PEER_ROWS):
        picked = jnp.take_along_axis(a_all[:, r * LANES:(r + 1) * LANES], e2, axis=1,
                                     mode="promise_in_bounds")
        acc = jnp.where(e1 == c * PEER_ROWS + r, picked, acc)
    acc_scr[...] = acc

    @pl.when(c == pl.num_programs(1) - 1)
    def _():
        c_ref[...] = tg_scr[slot_old] * _gelu(acc)
        t1_scr[slot_new] = r1_scr[...].T
        t2_scr[slot_new] = r2_scr[...].T
        tg_scr[slot_new] = rg_scr[...].T
        e1_ref[...] = t1_scr[slot_new]
        e2_ref[...] = t2_scr[slot_new]


def _route_act(h2, wq, sk, ut):
    n = h2.shape[0]
    tn = _tile(n, 512)
    tiles = n // tn
    new = lambda i, c: (jnp.minimum(i, tiles - 1), 0)
    old = lambda i, c: (jnp.maximum(i - 1, 0), 0)
    return pl.pallas_call(
        _route_act_kernel,
        grid=(tiles + 1, PEER_HEADS),
        in_specs=[pl.BlockSpec((tn, D_MODEL), new), pl.BlockSpec((tn, D_MODEL), old),
                  _const_spec(wq.shape), _const_spec(sk.shape),
                  pl.BlockSpec((D_MODEL, PEER_EC), lambda i, c: (0, c))],
        out_specs=[pl.BlockSpec((tn, PEER_J), new), pl.BlockSpec((tn, PEER_J), new),
                   pl.BlockSpec((tn, PEER_J), old)],
        out_shape=[jax.ShapeDtypeStruct((n, PEER_J), I32), jax.ShapeDtypeStruct((n, PEER_J), I32),
                   jax.ShapeDtypeStruct((n, PEER_J), F32)],
        scratch_shapes=[pltpu.VMEM((2 * PEER_HEADS, tn, LANES), BF16),
                        pltpu.VMEM((PEER_J, tn), I32), pltpu.VMEM((PEER_J, tn), I32),
                        pltpu.VMEM((PEER_J, tn), F32),
                        pltpu.VMEM((2, tn, PEER_J), I32), pltpu.VMEM((2, tn, PEER_J), I32),
                        pltpu.VMEM((2, tn, PEER_J), F32), pltpu.VMEM((tn, PEER_J), F32)],
        compiler_params=_params(("arbitrary", "arbitrary")),
    )(h2, h2, wq, sk, ut)


PEER_HALF = PEER_ROWS // 2


def _peer_out_kernel(e1_ref, e2_ref, c_ref, v_ref, x_ref, o_ref, y_scr, acc_scr):
    tn = x_ref.shape[0]
    c = pl.program_id(1)
    nchunks = PEER_NEXP // PEER_EC

    @pl.when(c == 0)
    def _():
        acc_scr[...] = jnp.zeros_like(acc_scr)
        key = lax.broadcasted_iota(I32, (PEER_NKEYS, PEER_J), 0)

        def token(t, carry):
            row = pl.ds(t, 1)
            d = jnp.where(key == e1_ref[row, :], c_ref[row, :], 0.0).astype(BF16)
            w = jnp.where(key == e2_ref[row, :], 1.0, 0.0).astype(BF16)
            y = _dot_nt(d, w).astype(BF16).astype(F32)
            bits = lax.bitcast_convert_type(y, I32)
            base = pl.multiple_of(t * PEER_HALF, PEER_HALF)
            for cc in range(nchunks):
                hi = bits[cc * PEER_ROWS:cc * PEER_ROWS + PEER_HALF, :]
                lo = bits[cc * PEER_ROWS + PEER_HALF:(cc + 1) * PEER_ROWS, :]
                y_scr[cc, pl.ds(base, PEER_HALF), :] = hi | lax.shift_right_logical(lo, 16)
            return carry

        lax.fori_loop(0, tn, token, 0, unroll=32)

    his, los = [], []
    for i in range(PEER_HALF):
        words = y_scr[c, pl.ds(i, tn, stride=PEER_HALF), :]
        his.append(lax.bitcast_convert_type(words & -65536, F32).astype(BF16))
        los.append(lax.bitcast_convert_type(words << 16, F32).astype(BF16))
    acc_scr[...] += _dot(jnp.concatenate(his + los, axis=1), v_ref[...])

    @pl.when(c == nchunks - 1)
    def _():
        o_ref[...] = x_ref[...] + acc_scr[...]


def _peer_out(e1, e2, cj, vt, x):
    n = x.shape[0]
    tn = _tile(n, 512)
    row = lambda i, c: (i, 0)
    nchunks = PEER_NEXP // PEER_EC
    return pl.pallas_call(
        _peer_out_kernel,
        grid=(n // tn, nchunks),
        in_specs=[pl.BlockSpec((tn, PEER_J), row), pl.BlockSpec((tn, PEER_J), row),
                  pl.BlockSpec((tn, PEER_J), row), pl.BlockSpec((PEER_EC, D_MODEL), lambda i, c: (c, 0)),
                  pl.BlockSpec((tn, D_MODEL), row)],
        out_specs=pl.BlockSpec((tn, D_MODEL), row),
        out_shape=jax.ShapeDtypeStruct((n, D_MODEL), F32),
        scratch_shapes=[pltpu.VMEM((nchunks, tn * PEER_HALF, LANES), I32),
                        pltpu.VMEM((tn, D_MODEL), F32)],
        compiler_params=_params(("parallel", "arbitrary")),
    )(e1, e2, cj, vt, x)


def _pair_major(lf, batch, length):
    return lf.reshape(batch, length, FOX_HEADS // 2, 2).transpose(0, 2, 3, 1)


def _prep_weights(g_norm1, w_in, w_gla_a2, b_gla_a, g_gla_onorm, g_fox_qnorm, g_fox_knorm, b_fox_f, w_out,
                  g_norm2, w_peer_q, peer_subkeys, peer_u, peer_v):
    bounds = [0]
    for size in IN_SIZES:
        bounds.append(bounds[-1] + size)
    gq, gk, gv, gg, glr, fq, fk, fv, ff = [w_in[:, bounds[i]:bounds[i + 1]] for i in range(len(IN_SIZES))]
    pad = jnp.zeros((D_MODEL, Z_SMALL - FOX_HEADS - GLA_LOWRANK), w_in.dtype)
    w_cat = jnp.concatenate([gq, gk, gv, gg, fq, fk, fv, ff, glr, pad], axis=1).astype(BF16)
    wa2p = jnp.zeros((Z_SMALL, GLA_QK), F32).at[FOX_HEADS:FOX_HEADS + GLA_LOWRANK].set(w_gla_a2).astype(BF16)
    bsm = jnp.zeros((1, Z_SMALL), F32).at[0, :FOX_HEADS].set(b_fox_f)
    head_of_col = jnp.arange(FOX_W) // FOX_HD
    ind = (head_of_col[:, None] == jnp.arange(LANES)[None, :]).astype(BF16)
    return dict(
        inproj=(g_norm1[None], w_cat, wa2p, b_gla_a[None], bsm, jnp.tile(g_fox_qnorm, FOX_HEADS)[None],
                jnp.tile(g_fox_knorm, FOX_HEADS)[None], ind, ind.T),
        gon=g_gla_onorm[None], w_out=w_out.astype(BF16), g2=g_norm2[None], wq=w_peer_q.astype(BF16),
        sk=peer_subkeys.reshape(2 * PEER_HEADS, PEER_NKEYS, LANES).astype(BF16),
        ut=peer_u.T.astype(BF16), v=peer_v.astype(BF16))


def _path_layer(x, wts, layer, depth, stacked, batch, seq, s0, k_past=None, v_past=None, lf_past=None):
    z, la, k_st, v_st, lf_st = _inproj(x, wts["inproj"], layer, depth, stacked)
    og, state = _gla(z, la, s0, wts["gon"], batch=batch, seq=seq)
    past_args = {}
    if k_past is not None:
        past = k_past.shape[1]
        past_args = dict(k_past=k_past.reshape(batch, past, FOX_W),
                         v_past=v_past.reshape(batch, past, FOX_W),
                         lf_past_t=lf_past.reshape(batch, past, FOX_HEADS // 2, 2).transpose(0, 2, 3, 1))
    of = _fox(z, _pair_major(lf_st[layer], batch, seq), batch=batch, seq=seq, **past_args)
    x, h2 = _outproj(og, of, x, wts["w_out"], wts["g2"])
    e1, e2, cj = _route_act(h2, wts["wq"], wts["sk"], wts["ut"])
    x = _peer_out(e1, e2, cj, wts["v"], x)
    return x, (k_st, v_st, lf_st), state


def kernel(x_prompt, x_sample, cache_fox_k, cache_fox_v, cache_fox_logf, state_gla, g_norm1, w_in, w_gla_a2,
           b_gla_a, g_gla_onorm, g_fox_qnorm, g_fox_knorm, b_fox_f, w_out, g_norm2, w_peer_q, peer_subkeys,
           peer_u, peer_v):
    bp, tp, _ = x_prompt.shape
    bs, ts, _ = x_sample.shape
    depth = w_in.shape[0]
    xp = x_prompt.reshape(bp * tp, D_MODEL)
    xs = x_sample.reshape(bs * ts, D_MODEL)
    stk_p, stk_s, st_p, st_s = (), (), [], []
    zero_state = jnp.zeros((bp, GLA_HEADS, GLA_DK, GLA_DV), F32)
    for l in range(depth):
        wts = _prep_weights(g_norm1[l], w_in[l], w_gla_a2[l], b_gla_a[l], g_gla_onorm[l], g_fox_qnorm[l],
                            g_fox_knorm[l], b_fox_f[l], w_out[l], g_norm2[l], w_peer_q[l], peer_subkeys[l],
                            peer_u[l], peer_v[l])
        xp, stk_p, state = _path_layer(xp, wts, l, depth, stk_p, bp, tp, zero_state)
        st_p.append(state)
        xs, stk_s, state = _path_layer(xs, wts, l, depth, stk_s, bs, ts, state_gla[l],
                                       cache_fox_k[l], cache_fox_v[l], cache_fox_logf[l])
        st_s.append(state)

    def unstack(stk, b, t):
        k, v, lf = stk
        return (k.reshape(depth, b, t, FOX_HEADS, FOX_HD), v.reshape(depth, b, t, FOX_HEADS, FOX_HD),
                lf.reshape(depth, b, t, FOX_HEADS))

    return (xp.reshape(bp, tp, D_MODEL), xs.reshape(bs, ts, D_MODEL),
            *unstack(stk_p, bp, tp), jnp.stack(st_p), *unstack(stk_s, bs, ts), jnp.stack(st_s))
```

```python
import functools

import jax
import jax.numpy as jnp
from jax import lax
from jax.experimental import pallas as pl
from jax.experimental.pallas import tpu as pltpu

F32 = jnp.float32
BF16 = jnp.bfloat16
I32 = jnp.int32

D_MODEL = 1024
EPS = 1e-6
GLA_CHUNK = 64
GLA_HEADS = 4
GLA_DK = 64
GLA_DV = 128
GLA_LOWRANK = 16
GLA_GATE_NORM = 16.0
GLA_SUB = 16
FOX_HEADS = 8
FOX_HD = 64
FOX_SCALE = FOX_HD ** -0.5
LOG2E = 1.4426950408889634
FOX_BLOCK = 512
GLA_QK = GLA_HEADS * GLA_DK
GLA_V = GLA_HEADS * GLA_DV
FOX_W = FOX_HEADS * FOX_HD
IN_SIZES = (GLA_QK, GLA_QK, GLA_V, GLA_V, GLA_LOWRANK, FOX_W, FOX_W, FOX_W, FOX_HEADS)
PEER_HEADS = 8
PEER_NKEYS = 128
PEER_TOPK = 16
PEER_NEXP = PEER_NKEYS * PEER_NKEYS
PEER_J = PEER_HEADS * PEER_TOPK

LANES = 128
SUBLANES = 8
Z_MAIN = 3072
Z_SMALL = 128
VMEM_LIMIT = 56 * 1024 * 1024

ZC_GQ, ZC_GK, ZC_GV, ZC_GG, ZC_FQ, ZC_FK, ZC_FV = 0, 2, 4, 8, 12, 16, 20


def _params(sem):
    return pltpu.CompilerParams(dimension_semantics=sem, vmem_limit_bytes=VMEM_LIMIT)


def _tile(n, cap):
    t = cap
    while n % t:
        t //= 2
    return t


def _dot(a, b):
    return jnp.dot(a, b, preferred_element_type=F32)


def _dot_nt(a, b):
    return lax.dot_general(a, b, (((1,), (1,)), ((), ())), preferred_element_type=F32)


def _dot_tn(a, b):
    return lax.dot_general(a, b, (((0,), (0,)), ((), ())), preferred_element_type=F32)


def _split(x):
    hi = x.astype(BF16)
    lo = (x - hi.astype(F32)).astype(BF16)
    return hi, lo


def _dot_exact_rhs(x, m):
    hi, lo = _split(x)
    return _dot(hi, m) + _dot(lo, m)


def _dot_exact_lhs(m, x):
    hi, lo = _split(x)
    return _dot(m, hi) + _dot(m, lo)


def _log_sigmoid(y):
    return jnp.minimum(y, 0.0) - jnp.log(1.0 + jnp.exp(-jnp.abs(y)))


def _const_spec(shape):
    nd = len(shape)
    return pl.BlockSpec(shape, lambda *_: (0,) * nd)


def _inproj_kernel(*refs, n_alias, layer):
    (x_ref, g1_ref, w_ref, wa2_ref, ba_ref, bsm_ref, gq_ref, gk_ref, ind_ref, indt_ref) = refs[:10]
    z_ref, la_ref, k32_ref, v32_ref, lf_ref = refs[10 + n_alias:]
    for stack_ref in (k32_ref, v32_ref, lf_ref):
        for other in range(stack_ref.shape[0]):
            if other != layer:
                stack_ref[other] = jnp.zeros(stack_ref.shape[1:], F32)
    x = x_ref[...]
    ms = jnp.mean(x * x, axis=-1, keepdims=True)
    h = (x * lax.rsqrt(ms + EPS) * g1_ref[...]).astype(BF16)
    z = _dot(h, w_ref[...])

    c_fq, c_fk, c_fv = ZC_FQ * LANES, ZC_FK * LANES, ZC_FV * LANES
    raw = [z[:, c_fq:c_fk], z[:, c_fk:c_fv]]
    ss = [_dot_exact_rhs(t * t, ind_ref[...]) for t in raw]
    scale = [_dot_exact_rhs(lax.rsqrt(v * (1.0 / FOX_HD) + EPS), indt_ref[...]) for v in ss]
    fq = raw[0] * scale[0] * gq_ref[...]
    fk = raw[1] * scale[1] * gk_ref[...]
    fv = z[:, c_fv:Z_MAIN]
    z_ref[:, 0:c_fq] = z[:, 0:c_fq].astype(BF16)
    z_ref[:, c_fq:c_fk] = fq.astype(BF16)
    z_ref[:, c_fk:c_fv] = fk.astype(BF16)
    z_ref[:, c_fv:Z_MAIN] = fv.astype(BF16)
    tm = x_ref.shape[0]
    for hd in range(FOX_HEADS):
        token_rows = pl.ds(hd, tm, stride=FOX_HEADS)
        k32_ref[layer, token_rows, :] = fk[:, hd * FOX_HD:(hd + 1) * FOX_HD]
        v32_ref[layer, token_rows, :] = fv[:, hd * FOX_HD:(hd + 1) * FOX_HD]
    small = z[:, Z_MAIN:Z_MAIN + Z_SMALL]
    lf_ref[layer] = _log_sigmoid(small + bsm_ref[...])[:, 0:FOX_HEADS]
    y = _dot(small.astype(BF16), wa2_ref[...]) + ba_ref[...]
    la_ref[...] = _log_sigmoid(y) * (1.0 / GLA_GATE_NORM)


def _inproj(x, consts, layer, depth, stacked):
    n = x.shape[0]
    tm = _tile(n, 512)
    row = lambda i: (i, 0)
    n_alias = len(stacked)
    slabs, slab = (1, lambda i: (layer, i, 0)) if n_alias else (depth, lambda i: (0, i, 0))
    kv_block = (slabs, tm * FOX_HEADS, FOX_HD)
    stack_shapes = ((depth, n * FOX_HEADS, FOX_HD), (depth, n * FOX_HEADS, FOX_HD), (depth, n, FOX_HEADS))
    return pl.pallas_call(
        functools.partial(_inproj_kernel, n_alias=n_alias, layer=0 if n_alias else layer),
        grid=(n // tm,),
        in_specs=[pl.BlockSpec((tm, D_MODEL), row)] + [_const_spec(c.shape) for c in consts]
        + [pl.BlockSpec(memory_space=pl.ANY)] * n_alias,
        out_specs=[pl.BlockSpec((tm, Z_MAIN), row), pl.BlockSpec((tm, GLA_QK), row)]
        + [pl.BlockSpec(kv_block, slab), pl.BlockSpec(kv_block, slab),
           pl.BlockSpec((slabs, tm, FOX_HEADS), slab)],
        out_shape=[jax.ShapeDtypeStruct((n, Z_MAIN), BF16), jax.ShapeDtypeStruct((n, GLA_QK), F32)]
        + [jax.ShapeDtypeStruct(shape, F32) for shape in stack_shapes],
        input_output_aliases={1 + len(consts) + i: 2 + i for i in range(n_alias)},
        compiler_params=_params(("parallel",)),
    )(x, *consts, *stacked)


def _gla_kernel(q_ref, k_ref, v_ref, gg_ref, la_ref, s0_ref, gon_ref, tri_ref, o_ref, sfin_ref, st_scr,
                *, chunk, nchunk):
    t = pl.program_id(1)
    nsub = chunk // GLA_SUB

    @pl.when(t == 0)
    def _():
        for p in range(2):
            st_scr[p] = s0_ref[0, p].T

    rowi = lax.broadcasted_iota(I32, (chunk, LANES), 0)
    lane = lax.broadcasted_iota(I32, (chunk, LANES), 1)
    lane_st = lax.broadcasted_iota(I32, (chunk, nsub * LANES), 1)
    lane_sq = lax.broadcasted_iota(I32, (LANES, LANES), 1)
    arow = lax.broadcasted_iota(I32, (chunk, chunk), 0)
    acol = lax.broadcasted_iota(I32, (chunk, chunk), 1)

    per_trip = next(n for n in (8, 4, 2, 1) if nchunk % n == 0)

    def body(trip, carry):
        rows_u, prep = [], []
        for u in range(per_trip):
            rows = pl.ds(pl.multiple_of((trip * per_trip + u) * chunk, chunk), chunk)
            rows_u.append(rows)
            bcum = _dot_exact_lhs(tri_ref[...], la_ref[rows, :])
            q = q_ref[rows, :].astype(F32) * (GLA_DK ** -0.5)
            k = k_ref[rows, :].astype(F32)
            for p in range(2):
                cols = slice(p * LANES, (p + 1) * LANES)
                bp, qp, kp = bcum[:, cols], q[:, cols], k[:, cols]
                blast = bp[chunk - 1:chunk, :]
                qs, ks = [], []
                for sb in range(nsub):
                    beta = bp[sb * GLA_SUB - 1:sb * GLA_SUB, :] if sb else jnp.zeros((1, LANES), F32)
                    inblk = (rowi >= sb * GLA_SUB) & (rowi < (sb + 1) * GLA_SUB)
                    qs.append(jnp.where(inblk, qp * jnp.exp(jnp.where(inblk, bp - beta, 0.0)), 0.0))
                    valid = rowi < (sb + 1) * GLA_SUB
                    ks.append(jnp.where(valid, kp * jnp.exp(jnp.where(valid, beta - bp, 0.0)), 0.0))
                prep.append(dict(qst=jnp.concatenate(qs, axis=1), kst=jnp.concatenate(ks, axis=1).astype(BF16),
                                 qinter=qp * jnp.exp(bp), kdec=(kp * jnp.exp(blast - bp)).astype(BF16),
                                 decay=jnp.exp(blast)))
        heads = [(u, p, i) for u in range(per_trip) for p in range(2) for i in range(2)]
        hcols = [slice((2 * p + i) * GLA_DV, (2 * p + i + 1) * GLA_DV) for _, p, i in heads]
        vhs = [v_ref[rows_u[u], hc] for (u, _, _), hc in zip(heads, hcols)]
        a_s = [_dot_nt(jnp.where((lane_st & GLA_DK) == i * GLA_DK, prep[2 * u + p]["qst"], 0.0).astype(BF16),
                       prep[2 * u + p]["kst"]) for u, p, i in heads]
        upd = [_dot_tn(vh, prep[2 * u + p]["kdec"]) for (u, p, i), vh in zip(heads, vhs)]
        a_s = [jnp.where(acol <= arow, a, 0.0).astype(BF16) for a in a_s]
        intra = [_dot(a, vh) for a, vh in zip(a_s, vhs)]
        states = [st_scr[p] for p in range(2)]
        for u in range(per_trip):
            for p in range(2):
                new = states[2 * u + p] * prep[2 * u + p]["decay"] + jnp.where(
                    lane_sq < GLA_DK, upd[4 * u + 2 * p], upd[4 * u + 2 * p + 1])
                states.append(new)
        for p in range(2):
            st_scr[p] = states[2 * per_trip + p]
        stb = [st.astype(BF16) for st in states[:2 * per_trip]]
        o_s = [o_in + _dot_nt(jnp.where((lane & GLA_DK) == i * GLA_DK, prep[2 * u + p]["qinter"], 0.0)
                              .astype(BF16), stb[2 * u + p]) for (u, p, i), o_in in zip(heads, intra)]
        for (u, _, _), o, hc in zip(heads, o_s, hcols):
            on = o * lax.rsqrt(jnp.mean(o * o, axis=-1, keepdims=True) + EPS) * gon_ref[...]
            gate = gg_ref[rows_u[u], hc].astype(F32)
            o_ref[rows_u[u], hc] = (on * gate * (1.0 / (1.0 + jnp.exp(-gate)))).astype(BF16)
        return carry

    lax.fori_loop(0, nchunk // per_trip, body, 0)

    @pl.when(t == pl.num_programs(1) - 1)
    def _():
        for p in range(2):
            sfin_ref[0, p] = st_scr[p].T


def _gla(z, la, s0, gon, *, batch, seq):
    chunk = min(seq, GLA_CHUNK)
    tb = min(seq, 512)
    nt = seq // tb
    tri = jnp.tril(jnp.ones((chunk, chunk), F32)).astype(BF16)
    qk_w, v_w = GLA_QK, GLA_V
    rows = lambda col: (lambda b, t: (b * nt + t, col))
    state_spec = pl.BlockSpec((1, 2, LANES, LANES), lambda b, t: (b, 0, 0, 0))
    o, sfin = pl.pallas_call(
        functools.partial(_gla_kernel, chunk=chunk, nchunk=tb // chunk),
        grid=(batch, nt),
        in_specs=[pl.BlockSpec((tb, qk_w), rows(ZC_GQ * LANES // qk_w)),
                  pl.BlockSpec((tb, qk_w), rows(ZC_GK * LANES // qk_w)),
                  pl.BlockSpec((tb, v_w), rows(ZC_GV * LANES // v_w)),
                  pl.BlockSpec((tb, v_w), rows(ZC_GG * LANES // v_w)),
                  pl.BlockSpec((tb, qk_w), rows(0)), state_spec,
                  _const_spec((1, GLA_DV)), _const_spec((chunk, chunk))],
        out_specs=[pl.BlockSpec((tb, v_w), lambda b, t: (b * nt + t, 0)), state_spec],
        out_shape=[jax.ShapeDtypeStruct((batch * seq, v_w), BF16),
                   jax.ShapeDtypeStruct((batch, 2, LANES, LANES), F32)],
        scratch_shapes=[pltpu.VMEM((2, LANES, LANES), F32)],
        compiler_params=_params(("parallel", "arbitrary")),
    )(z, z, z, z, la, s0.reshape(batch, 2, LANES, LANES), gon, tri)
    return o, sfin.reshape(batch, GLA_HEADS, GLA_DK, GLA_DV)


def _fox_kernel(*refs, seq, past, tq, has_past):
    if has_past:
        (q_ref, kc_ref, vc_ref, lfc_ref, triu_ref, kp_ref, vp_ref, lfp_ref, triup_ref,
         o_ref, cc_scr, cp_scr) = refs
    else:
        q_ref, kc_ref, vc_ref, lfc_ref, triu_ref, o_ref, cc_scr = refs
    i = pl.program_id(2)
    nq = seq // tq
    pblk = FOX_BLOCK
    past_blk = min(past, FOX_BLOCK * FOX_BLOCK // tq)

    @pl.when(i == 0)
    def _():
        carry = jnp.zeros((2, 1), F32)
        if has_past:
            for jb in range(past // pblk):
                x = lfp_ref[0, 0, :, jb * pblk:(jb + 1) * pblk]
                cp_scr[:, jb * pblk:(jb + 1) * pblk] = _dot_exact_rhs(x, triup_ref[...]) + carry
                carry = carry + jnp.sum(x, axis=1, keepdims=True)
        for jb in range(nq):
            x = lfc_ref[0, 0, :, jb * tq:(jb + 1) * tq]
            cc_scr[:, jb * tq:(jb + 1) * tq] = _dot_exact_rhs(x, triu_ref[...]) + carry
            carry = carry + jnp.sum(x, axis=1, keepdims=True)

    q = q_ref[...]
    lane = lax.broadcasted_iota(I32, (tq, LANES), 1)
    qh = [jnp.where(lane < FOX_HD, q, jnp.zeros_like(q)), jnp.where(lane >= FOX_HD, q, jnp.zeros_like(q))]

    causal = lax.broadcasted_iota(I32, (tq, tq), 1) <= lax.broadcasted_iota(I32, (tq, tq), 0)

    def attend(blocks):
        hs = range(2)
        s = [[_dot_nt(qh[h], kb) * (FOX_SCALE * LOG2E) - crows[h] * LOG2E for kb, _, crows, _ in blocks]
             for h in hs]
        s = [[s_b if blk[3] is None else jnp.where(blk[3], s_b, -jnp.inf) for s_b, blk in zip(s[h], blocks)]
             for h in hs]
        m = [functools.reduce(jnp.maximum, [jnp.max(s_b, axis=1, keepdims=True) for s_b in s[h]]) for h in hs]
        acc = []
        for h in hs:
            total = None
            for s_b, (_, vb, _, _) in zip(s[h], blocks):
                lane_v = lax.broadcasted_iota(I32, vb.shape, 1)
                own = (lane_v < FOX_HD) if h == 0 else (lane_v >= FOX_HD)
                pv = _dot(jnp.exp2(s_b - m[h]).astype(BF16), jnp.where(own, vb, jnp.ones_like(vb)))
                total = pv if total is None else total + pv
            acc.append(total)
        num = jnp.where(lane < FOX_HD, acc[0], acc[1])
        den = jnp.where(lane < FOX_HD, pltpu.roll(acc[0], FOX_HD, 1), pltpu.roll(acc[1], FOX_HD, 1))
        o_ref[...] = (num / den).astype(BF16)

    past_blocks = []
    if has_past:
        for off in range(0, past, past_blk):
            past_blocks.append((kp_ref[0, off:off + past_blk, :].astype(BF16),
                                vp_ref[0, off:off + past_blk, :].astype(BF16),
                                [cp_scr[h:h + 1, off:off + past_blk] for h in range(2)], None))

    def cur_block(j, mask):
        rows = slice(j * tq, (j + 1) * tq)
        return (kc_ref[rows, :], vc_ref[rows, :], [cc_scr[h:h + 1, rows] for h in range(2)], mask)

    for qi in range(nq):
        @pl.when(i == qi)
        def _(qi=qi):
            attend(past_blocks + [cur_block(j, None) for j in range(qi)] + [cur_block(qi, causal)])


def _fox(z, lf_cur_t, *, batch, seq, k_past=None, v_past=None, lf_past_t=None):
    has_past = k_past is not None
    past = k_past.shape[1] if has_past else 0
    tq = min(seq, FOX_BLOCK)
    nq = seq // tq
    triu = jnp.triu(jnp.ones((tq, tq), F32)).astype(BF16)
    in_specs = [pl.BlockSpec((tq, LANES), lambda b, p, i: (b * nq + i, ZC_FQ + p)),
                pl.BlockSpec((seq, LANES), lambda b, p, i: (b, ZC_FK + p)),
                pl.BlockSpec((seq, LANES), lambda b, p, i: (b, ZC_FV + p)),
                pl.BlockSpec((1, 1, 2, seq), lambda b, p, i: (b, p, 0, 0)),
                _const_spec((tq, tq))]
    args = [z, z, z, lf_cur_t, triu]
    scratch = [pltpu.VMEM((2, seq), F32)]
    if has_past:
        triup = jnp.triu(jnp.ones((FOX_BLOCK, FOX_BLOCK), F32)).astype(BF16)
        in_specs += [pl.BlockSpec((1, past, LANES), lambda b, p, i: (b, 0, p)),
                     pl.BlockSpec((1, past, LANES), lambda b, p, i: (b, 0, p)),
                     pl.BlockSpec((1, 1, 2, past), lambda b, p, i: (b, p, 0, 0)),
                     _const_spec((FOX_BLOCK, FOX_BLOCK))]
        args += [k_past, v_past, lf_past_t, triup]
        scratch.append(pltpu.VMEM((2, past), F32))
    return pl.pallas_call(
        functools.partial(_fox_kernel, seq=seq, past=past, tq=tq, has_past=has_past),
        grid=(batch, FOX_HEADS // 2, nq),
        in_specs=in_specs,
        out_specs=pl.BlockSpec((tq, LANES), lambda b, p, i: (b * nq + i, p)),
        out_shape=jax.ShapeDtypeStruct((batch * seq, FOX_W), BF16),
        scratch_shapes=scratch,
        compiler_params=_params(("parallel", "parallel", "arbitrary")),
    )(*args)


def _outproj_kernel(og_ref, of_ref, x_ref, w_ref, g2_ref, xo_ref, h2_ref):
    mixed = _dot(og_ref[...], w_ref[0:GLA_V, :]) + _dot(of_ref[...], w_ref[GLA_V:GLA_V + FOX_W, :])
    x = x_ref[...] + mixed
    xo_ref[...] = x
    ms = jnp.mean(x * x, axis=-1, keepdims=True)
    h2_ref[...] = (x * lax.rsqrt(ms + EPS) * g2_ref[...]).astype(BF16)


def _outproj(og, of, x, w, g2):
    n = x.shape[0]
    tm = _tile(n, 512)
    row = lambda i: (i, 0)
    return pl.pallas_call(
        _outproj_kernel,
        grid=(n // tm,),
        in_specs=[pl.BlockSpec((tm, GLA_V), row), pl.BlockSpec((tm, FOX_W), row),
                  pl.BlockSpec((tm, D_MODEL), row), _const_spec(w.shape), _const_spec(g2.shape)],
        out_specs=[pl.BlockSpec((tm, D_MODEL), row), pl.BlockSpec((tm, D_MODEL), row)],
        out_shape=[jax.ShapeDtypeStruct((n, D_MODEL), F32), jax.ShapeDtypeStruct((n, D_MODEL), BF16)],
        compiler_params=_params(("parallel",)),
    )(og, of, x, w, g2)


def _merge_exchange_pairs(n):
    full = 1
    while full < n:
        full *= 2
    pairs = []
    p = 1
    while p < full:
        k = p
        while k >= 1:
            for j in range(k % p, full - k, 2 * k):
                for i in range(min(k, full - j - k)):
                    if (i + j) // (2 * p) == (i + j + k) // (2 * p):
                        pairs.append((i + j, i + j + k))
            k //= 2
        p *= 2
    return tuple((a, b) for a, b in pairs if b < n)


def _topk_keys(s, k):
    nrows, tn = s.shape
    groups = nrows // SUBLANES
    sub = lax.broadcasted_iota(I32, (SUBLANES, tn), 0)
    vals = [s[SUBLANES * r:SUBLANES * (r + 1), :] for r in range(groups)]
    idxs = [sub + SUBLANES * r for r in range(groups)]
    for a, b in _merge_exchange_pairs(groups):
        va, vb, ia, ib = vals[a], vals[b], idxs[a], idxs[b]
        a_first = (va > vb) | ((va == vb) & (ia < ib))
        vals[a], vals[b] = jnp.maximum(va, vb), jnp.minimum(va, vb)
        idxs[a], idxs[b] = jnp.where(a_first, ia, ib), jnp.where(a_first, ib, ia)
    out_v, out_i = [], []
    for t in range(k):
        head_v, head_i = vals[0], idxs[0]
        m = jnp.max(head_v, axis=0, keepdims=True)
        idx = jnp.min(jnp.where(head_v == m, head_i, nrows), axis=0, keepdims=True)
        out_v.append(m)
        out_i.append(idx)
        won = head_i == idx
        for r in range(min(k - 1 - t, groups)):
            last = r + 1 == groups
            vals[r] = jnp.where(won, -jnp.inf if last else vals[r + 1], vals[r])
            idxs[r] = jnp.where(won, nrows if last else idxs[r + 1], idxs[r])
    return jnp.concatenate(out_v, axis=0), jnp.concatenate(out_i, axis=0)


def _pair_topk(v1, i1, v2, i2):
    tn = v1.shape[1]
    iota_k = lax.broadcasted_iota(I32, (PEER_TOPK, tn), 0)
    iota_8 = lax.broadcasted_iota(I32, (8, tn), 0)

    def pick(table, sel):
        rows = []
        for r in range(PEER_TOPK):
            rows.append(jnp.sum(jnp.where(iota_k == sel[r:r + 1, :], table, 0), axis=0, keepdims=True))
        return jnp.concatenate(rows, axis=0)

    blocks = [v1[0:1, :] + v2]
    for a in range(1, 8):
        blocks.append(jnp.where(iota_8 < PEER_TOPK // (a + 1), v1[a:a + 1, :] + v2[0:8, :], -jnp.inf))
    blocks.append(v1[8:16, :] + v2[0:1, :])
    top, ti = _topk_keys(jnp.concatenate(blocks, axis=0), PEER_TOPK)
    mid = ti - PEER_TOPK
    e1 = pick(i1, jnp.where(ti < 16, 0, jnp.where(ti < 72, (mid >> 3) + 1, ti - 64)))
    e2 = pick(i2, jnp.where(ti < 16, ti, jnp.where(ti < 72, mid & 7, 0)))
    ex = jnp.exp(top - jnp.max(top, axis=0, keepdims=True))
    return e1, e2, ex / jnp.sum(ex, axis=0, keepdims=True)


PEER_ROWS = 16
PEER_EC = PEER_ROWS * PEER_NKEYS
assert PEER_NEXP // PEER_EC == PEER_HEADS


def _gelu(a):
    return 0.5 * a * (1.0 + lax.erf(a * (2.0 ** -0.5)))


def _route_act_kernel(hn_ref, hp_ref, wq_ref, sk_ref, ut_ref, e1_ref, e2_ref, c_ref,
                      q_scr, r1_scr, r2_scr, rg_scr, t1_scr, t2_scr, tg_scr, acc_scr):
    i = pl.program_id(0)
    c = pl.program_id(1)
    slot_new = i % 2
    slot_old = 1 - slot_new

    @pl.when((i == 0) & (c == 0))
    def _():
        t1_scr[1] = jnp.zeros_like(t1_scr[1])
        t2_scr[1] = jnp.zeros_like(t2_scr[1])
        tg_scr[1] = jnp.zeros_like(tg_scr[1])

    @pl.when(c == 0)
    def _():
        acc_scr[...] = jnp.zeros_like(acc_scr)
        q = _dot(hn_ref[...], wq_ref[...])
        for hc in range(2 * PEER_HEADS):
            q_scr[hc] = q[:, hc * LANES:(hc + 1) * LANES].astype(BF16)

    v1, i1 = _topk_keys(_dot_nt(sk_ref[2 * c], q_scr[2 * c]), PEER_TOPK)
    v2, i2 = _topk_keys(_dot_nt(sk_ref[2 * c + 1], q_scr[2 * c + 1]), PEER_TOPK)
    e1h, e2h, gh = _pair_topk(v1, i1, v2, i2)
    rows = pl.ds(pl.multiple_of(c * PEER_TOPK, PEER_TOPK), PEER_TOPK)
    r1_scr[rows, :] = e1h
    r2_scr[rows, :] = e2h
    rg_scr[rows, :] = gh

    a_all = _dot(hp_ref[...], ut_ref[...])
    e1 = t1_scr[slot_old]
    e2 = t2_scr[slot_old]
    acc = acc_scr[...]
    for r in range(PEER_ROWS):
        picked = jnp.take_along_axis(a_all[:, r * LANES:(r + 1) * LANES], e2, axis=1,
                                     mode="promise_in_bounds")
        acc = jnp.where(e1 == c * PEER_ROWS + r, picked, acc)
    acc_scr[...] = acc

    @pl.when(c == pl.num_programs(1) - 1)
    def _():
        c_ref[...] = tg_scr[slot_old] * _gelu(acc)
        t1_scr[slot_new] = r1_scr[...].T
        t2_scr[slot_new] = r2_scr[...].T
        tg_scr[slot_new] = rg_scr[...].T
        e1_ref[...] = t1_scr[slot_new]
        e2_ref[...] = t2_scr[slot_new]


def _route_act(h2, wq, sk, ut):
    n = h2.shape[0]
    tn = _tile(n, 512)
    tiles = n // tn
    new = lambda i, c: (jnp.minimum(i, tiles - 1), 0)
    old = lambda i, c: (jnp.maximum(i - 1, 0), 0)
    return pl.pallas_call(
        _route_act_kernel,
        grid=(tiles + 1, PEER_HEADS),
        in_specs=[pl.BlockSpec((tn, D_MODEL), new), pl.BlockSpec((tn, D_MODEL), old),
                  _const_spec(wq.shape), _const_spec(sk.shape),
                  pl.BlockSpec((D_MODEL, PEER_EC), lambda i, c: (0, c))],
        out_specs=[pl.BlockSpec((tn, PEER_J), new), pl.BlockSpec((tn, PEER_J), new),
                   pl.BlockSpec((tn, PEER_J), old)],
        out_shape=[jax.ShapeDtypeStruct((n, PEER_J), I32), jax.ShapeDtypeStruct((n, PEER_J), I32),
                   jax.ShapeDtypeStruct((n, PEER_J), F32)],
        scratch_shapes=[pltpu.VMEM((2 * PEER_HEADS, tn, LANES), BF16),
                        pltpu.VMEM((PEER_J, tn), I32), pltpu.VMEM((PEER_J, tn), I32),
                        pltpu.VMEM((PEER_J, tn), F32),
                        pltpu.VMEM((2, tn, PEER_J), I32), pltpu.VMEM((2, tn, PEER_J), I32),
                        pltpu.VMEM((2, tn, PEER_J), F32), pltpu.VMEM((tn, PEER_J), F32)],
        compiler_params=_params(("arbitrary", "arbitrary")),
    )(h2, h2, wq, sk, ut)


PEER_HALF = PEER_ROWS // 2


def _peer_out_kernel(e1_ref, e2_ref, c_ref, v_ref, x_ref, o_ref, y_scr, acc_scr):
    tn = x_ref.shape[0]
    c = pl.program_id(1)
    nchunks = PEER_NEXP // PEER_EC

    @pl.when(c == 0)
    def _():
        acc_scr[...] = jnp.zeros_like(acc_scr)
        key = lax.broadcasted_iota(I32, (PEER_NKEYS, PEER_J), 0)

        def token(t, carry):
            row = pl.ds(t, 1)
            d = jnp.where(key == e1_ref[row, :], c_ref[row, :], 0.0).astype(BF16)
            w = jnp.where(key == e2_ref[row, :], 1.0, 0.0).astype(BF16)
            y = _dot_nt(d, w).astype(BF16).astype(F32)
            bits = lax.bitcast_convert_type(y, I32)
            base = pl.multiple_of(t * PEER_HALF, PEER_HALF)
            for cc in range(nchunks):
                hi = bits[cc * PEER_ROWS:cc * PEER_ROWS + PEER_HALF, :]
                lo = bits[cc * PEER_ROWS + PEER_HALF:(cc + 1) * PEER_ROWS, :]
                y_scr[cc, pl.ds(base, PEER_HALF), :] = hi | lax.shift_right_logical(lo, 16)
            return carry

        lax.fori_loop(0, tn, token, 0, unroll=32)

    his, los = [], []
    for i in range(PEER_HALF):
        words = y_scr[c, pl.ds(i, tn, stride=PEER_HALF), :]
        his.append(lax.bitcast_convert_type(words & -65536, F32).astype(BF16))
        los.append(lax.bitcast_convert_type(words << 16, F32).astype(BF16))
    acc_scr[...] += _dot(jnp.concatenate(his + los, axis=1), v_ref[...])

    @pl.when(c == nchunks - 1)
    def _():
        o_ref[...] = x_ref[...] + acc_scr[...]


def _peer_out(e1, e2, cj, vt, x):
    n = x.shape[0]
    tn = _tile(n, 512)
    row = lambda i, c: (i, 0)
    nchunks = PEER_NEXP // PEER_EC
    return pl.pallas_call(
        _peer_out_kernel,
        grid=(n // tn, nchunks),
        in_specs=[pl.BlockSpec((tn, PEER_J), row), pl.BlockSpec((tn, PEER_J), row),
                  pl.BlockSpec((tn, PEER_J), row), pl.BlockSpec((PEER_EC, D_MODEL), lambda i, c: (c, 0)),
                  pl.BlockSpec((tn, D_MODEL), row)],
        out_specs=pl.BlockSpec((tn, D_MODEL), row),
        out_shape=jax.ShapeDtypeStruct((n, D_MODEL), F32),
        scratch_shapes=[pltpu.VMEM((nchunks, tn * PEER_HALF, LANES), I32),
                        pltpu.VMEM((tn, D_MODEL), F32)],
        compiler_params=_params(("parallel", "arbitrary")),
    )(e1, e2, cj, vt, x)


def _pair_major(lf, batch, length):
    return lf.reshape(batch, length, FOX_HEADS // 2, 2).transpose(0, 2, 3, 1)


def _prep_weights(g_norm1, w_in, w_gla_a2, b_gla_a, g_gla_onorm, g_fox_qnorm, g_fox_knorm, b_fox_f, w_out,
                  g_norm2, w_peer_q, peer_subkeys, peer_u, peer_v):
    bounds = [0]
    for size in IN_SIZES:
        bounds.append(bounds[-1] + size)
    gq, gk, gv, gg, glr, fq, fk, fv, ff = [w_in[:, bounds[i]:bounds[i + 1]] for i in range(len(IN_SIZES))]
    pad = jnp.zeros((D_MODEL, Z_SMALL - FOX_HEADS - GLA_LOWRANK), w_in.dtype)
    w_cat = jnp.concatenate([gq, gk, gv, gg, fq, fk, fv, ff, glr, pad], axis=1).astype(BF16)
    wa2p = jnp.zeros((Z_SMALL, GLA_QK), F32).at[FOX_HEADS:FOX_HEADS + GLA_LOWRANK].set(w_gla_a2).astype(BF16)
    bsm = jnp.zeros((1, Z_SMALL), F32).at[0, :FOX_HEADS].set(b_fox_f)
    head_of_col = jnp.arange(FOX_W) // FOX_HD
    ind = (head_of_col[:, None] == jnp.arange(LANES)[None, :]).astype(BF16)
    return dict(
        inproj=(g_norm1[None], w_cat, wa2p, b_gla_a[None], bsm, jnp.tile(g_fox_qnorm, FOX_HEADS)[None],
                jnp.tile(g_fox_knorm, FOX_HEADS)[None], ind, ind.T),
        gon=g_gla_onorm[None], w_out=w_out.astype(BF16), g2=g_norm2[None], wq=w_peer_q.astype(BF16),
        sk=peer_subkeys.reshape(2 * PEER_HEADS, PEER_NKEYS, LANES).astype(BF16),
        ut=peer_u.T.astype(BF16), v=peer_v.astype(BF16))


def _path_layer(x, wts, layer, depth, stacked, batch, seq, s0, k_past=None, v_past=None, lf_past=None):
    z, la, k_st, v_st, lf_st = _inproj(x, wts["inproj"], layer, depth, stacked)
    og, state = _gla(z, la, s0, wts["gon"], batch=batch, seq=seq)
    past_args = {}
    if k_past is not None:
        past = k_past.shape[1]
        past_args = dict(k_past=k_past.reshape(batch, past, FOX_W),
                         v_past=v_past.reshape(batch, past, FOX_W),
                         lf_past_t=lf_past.reshape(batch, past, FOX_HEADS // 2, 2).transpose(0, 2, 3, 1))
    of = _fox(z, _pair_major(lf_st[layer], batch, seq), batch=batch, seq=seq, **past_args)
    x, h2 = _outproj(og, of, x, wts["w_out"], wts["g2"])
    e1, e2, cj = _route_act(h2, wts["wq"], wts["sk"], wts["ut"])
    x = _peer_out(e1, e2, cj, wts["v"], x)
    return x, (k_st, v_st, lf_st), state


def kernel(x_prompt, x_sample, cache_fox_k, cache_fox_v, cache_fox_logf, state_gla, g_norm1, w_in, w_gla_a2,
           b_gla_a, g_gla_onorm, g_fox_qnorm, g_fox_knorm, b_fox_f, w_out, g_norm2, w_peer_q, peer_subkeys,
           peer_u, peer_v):
    bp, tp, _ = x_prompt.shape
    bs, ts, _ = x_sample.shape
    depth = w_in.shape[0]
    xp = x_prompt.reshape(bp * tp, D_MODEL)
    xs = x_sample.reshape(bs * ts, D_MODEL)
    stk_p, stk_s, st_p, st_s = (), (), [], []
    zero_state = jnp.zeros((bp, GLA_HEADS, GLA_DK, GLA_DV), F32)
    for l in range(depth):
        wts = _prep_weights(g_norm1[l], w_in[l], w_gla_a2[l], b_gla_a[l], g_gla_onorm[l], g_fox_qnorm[l],
                            g_fox_knorm[l], b_fox_f[l], w_out[l], g_norm2[l], w_peer_q[l], peer_subkeys[l],
                            peer_u[l], peer_v[l])
        xp, stk_p, state = _path_layer(xp, wts, l, depth, stk_p, bp, tp, zero_state)
        st_p.append(state)
        xs, stk_s, state = _path_layer(xs, wts, l, depth, stk_s, bs, ts, state_gla[l],
                                       cache_fox_k[l], cache_fox_v[l], cache_fox_logf[l])
        st_s.append(state)

    def unstack(stk, b, t):
        k, v, lf = stk
        return (k.reshape(depth, b, t, FOX_HEADS, FOX_HD), v.reshape(depth, b, t, FOX_HEADS, FOX_HD),
                lf.reshape(depth, b, t, FOX_HEADS))

    return (xp.reshape(bp, tp, D_MODEL), xs.reshape(bs, ts, D_MODEL),
            *unstack(stk_p, bp, tp), jnp.stack(st_p), *unstack(stk_s, bs, ts), jnp.stack(st_s))
```

```python
import functools

import jax
import jax.numpy as jnp
from jax import lax
from jax.experimental import pallas as pl
from jax.experimental.pallas import tpu as pltpu

F32 = jnp.float32
BF16 = jnp.bfloat16
I32 = jnp.int32

D_MODEL = 1024
EPS = 1e-6
GLA_CHUNK = 64
GLA_HEADS = 4
GLA_DK = 64
GLA_DV = 128
GLA_LOWRANK = 16
GLA_GATE_NORM = 16.0
GLA_SUB = 16
FOX_HEADS = 8
FOX_HD = 64
FOX_SCALE = FOX_HD ** -0.5
LOG2E = 1.4426950408889634
FOX_BLOCK = 512
GLA_QK = GLA_HEADS * GLA_DK
GLA_V = GLA_HEADS * GLA_DV
FOX_W = FOX_HEADS * FOX_HD
IN_SIZES = (GLA_QK, GLA_QK, GLA_V, GLA_V, GLA_LOWRANK, FOX_W, FOX_W, FOX_W, FOX_HEADS)
PEER_HEADS = 8
PEER_NKEYS = 128
PEER_TOPK = 16
PEER_NEXP = PEER_NKEYS * PEER_NKEYS
PEER_J = PEER_HEADS * PEER_TOPK

LANES = 128
SUBLANES = 8
Z_MAIN = 3072
Z_SMALL = 128
VMEM_LIMIT = 56 * 1024 * 1024

ZC_GQ, ZC_GK, ZC_GV, ZC_GG, ZC_FQ, ZC_FK, ZC_FV = 0, 2, 4, 8, 12, 16, 20


def _params(sem):
    return pltpu.CompilerParams(dimension_semantics=sem, vmem_limit_bytes=VMEM_LIMIT)


def _tile(n, cap):
    t = cap
    while n % t:
        t //= 2
    return t


def _dot(a, b):
    return jnp.dot(a, b, preferred_element_type=F32)


def _dot_nt(a, b):
    return lax.dot_general(a, b, (((1,), (1,)), ((), ())), preferred_element_type=F32)


def _dot_tn(a, b):
    return lax.dot_general(a, b, (((0,), (0,)), ((), ())), preferred_element_type=F32)


def _split(x):
    hi = x.astype(BF16)
    lo = (x - hi.astype(F32)).astype(BF16)
    return hi, lo


def _dot_exact_rhs(x, m):
    hi, lo = _split(x)
    return _dot(hi, m) + _dot(lo, m)


def _dot_exact_lhs(m, x):
    hi, lo = _split(x)
    return _dot(m, hi) + _dot(m, lo)


def _log_sigmoid(y):
    return jnp.minimum(y, 0.0) - jnp.log(1.0 + jnp.exp(-jnp.abs(y)))


def _const_spec(shape):
    nd = len(shape)
    return pl.BlockSpec(shape, lambda *_: (0,) * nd)


def _inproj_kernel(*refs, n_alias, layer):
    (x_ref, g1_ref, w_ref, wa2_ref, ba_ref, bsm_ref, gq_ref, gk_ref, ind_ref, indt_ref) = refs[:10]
    z_ref, la_ref, k32_ref, v32_ref, lf_ref = refs[10 + n_alias:]
    for stack_ref in (k32_ref, v32_ref, lf_ref):
        for other in range(stack_ref.shape[0]):
            if other != layer:
                stack_ref[other] = jnp.zeros(stack_ref.shape[1:], F32)
    x = x_ref[...]
    ms = jnp.mean(x * x, axis=-1, keepdims=True)
    h = (x * lax.rsqrt(ms + EPS) * g1_ref[...]).astype(BF16)
    z = _dot(h, w_ref[...])

    c_fq, c_fk, c_fv = ZC_FQ * LANES, ZC_FK * LANES, ZC_FV * LANES
    raw = [z[:, c_fq:c_fk], z[:, c_fk:c_fv]]
    ss = [_dot_exact_rhs(t * t, ind_ref[...]) for t in raw]
    scale = [_dot_exact_rhs(lax.rsqrt(v * (1.0 / FOX_HD) + EPS), indt_ref[...]) for v in ss]
    fq = raw[0] * scale[0] * gq_ref[...]
    fk = raw[1] * scale[1] * gk_ref[...]
    fv = z[:, c_fv:Z_MAIN]
    z_ref[:, 0:c_fq] = z[:, 0:c_fq].astype(BF16)
    z_ref[:, c_fq:c_fk] = fq.astype(BF16)
    z_ref[:, c_fk:c_fv] = fk.astype(BF16)
    z_ref[:, c_fv:Z_MAIN] = fv.astype(BF16)
    tm = x_ref.shape[0]
    for hd in range(FOX_HEADS):
        token_rows = pl.ds(hd, tm, stride=FOX_HEADS)
        k32_ref[layer, token_rows, :] = fk[:, hd * FOX_HD:(hd + 1) * FOX_HD]
        v32_ref[layer, token_rows, :] = fv[:, hd * FOX_HD:(hd + 1) * FOX_HD]
    small = z[:, Z_MAIN:Z_MAIN + Z_SMALL]
    lf_ref[layer] = _log_sigmoid(small + bsm_ref[...])[:, 0:FOX_HEADS]
    y = _dot(small.astype(BF16), wa2_ref[...]) + ba_ref[...]
    la_ref[...] = _log_sigmoid(y) * (1.0 / GLA_GATE_NORM)


def _inproj(x, consts, layer, depth, stacked):
    n = x.shape[0]
    tm = _tile(n, 512)
    row = lambda i: (i, 0)
    n_alias = len(stacked)
    slabs, slab = (1, lambda i: (layer, i, 0)) if n_alias else (depth, lambda i: (0, i, 0))
    kv_block = (slabs, tm * FOX_HEADS, FOX_HD)
    stack_shapes = ((depth, n * FOX_HEADS, FOX_HD), (depth, n * FOX_HEADS, FOX_HD), (depth, n, FOX_HEADS))
    return pl.pallas_call(
        functools.partial(_inproj_kernel, n_alias=n_alias, layer=0 if n_alias else layer),
        grid=(n // tm,),
        in_specs=[pl.BlockSpec((tm, D_MODEL), row)] + [_const_spec(c.shape) for c in consts]
        + [pl.BlockSpec(memory_space=pl.ANY)] * n_alias,
        out_specs=[pl.BlockSpec((tm, Z_MAIN), row), pl.BlockSpec((tm, GLA_QK), row)]
        + [pl.BlockSpec(kv_block, slab), pl.BlockSpec(kv_block, slab),
           pl.BlockSpec((slabs, tm, FOX_HEADS), slab)],
        out_shape=[jax.ShapeDtypeStruct((n, Z_MAIN), BF16), jax.ShapeDtypeStruct((n, GLA_QK), F32)]
        + [jax.ShapeDtypeStruct(shape, F32) for shape in stack_shapes],
        input_output_aliases={1 + len(consts) + i: 2 + i for i in range(n_alias)},
        compiler_params=_params(("parallel",)),
    )(x, *consts, *stacked)


def _gla_kernel(q_ref, k_ref, v_ref, gg_ref, la_ref, s0_ref, gon_ref, tri_ref, o_ref, sfin_ref, st_scr,
                *, chunk, nchunk):
    t = pl.program_id(1)
    nsub = chunk // GLA_SUB

    @pl.when(t == 0)
    def _():
        for p in range(2):
            st_scr[p] = s0_ref[0, p].T

    rowi = lax.broadcasted_iota(I32, (chunk, LANES), 0)
    lane = lax.broadcasted_iota(I32, (chunk, LANES), 1)
    lane_st = lax.broadcasted_iota(I32, (chunk, nsub * LANES), 1)
    lane_sq = lax.broadcasted_iota(I32, (LANES, LANES), 1)
    arow = lax.broadcasted_iota(I32, (chunk, chunk), 0)
    acol = lax.broadcasted_iota(I32, (chunk, chunk), 1)

    per_trip = next(n for n in (8, 4, 2, 1) if nchunk % n == 0)

    def body(trip, carry):
        rows_u, prep = [], []
        for u in range(per_trip):
            rows = pl.ds(pl.multiple_of((trip * per_trip + u) * chunk, chunk), chunk)
            rows_u.append(rows)
            bcum = _dot_exact_lhs(tri_ref[...], la_ref[rows, :])
            q = q_ref[rows, :].astype(F32) * (GLA_DK ** -0.5)
            k = k_ref[rows, :].astype(F32)
            for p in range(2):
                cols = slice(p * LANES, (p + 1) * LANES)
                bp, qp, kp = bcum[:, cols], q[:, cols], k[:, cols]
                blast = bp[chunk - 1:chunk, :]
                qs, ks = [], []
                for sb in range(nsub):
                    beta = bp[sb * GLA_SUB - 1:sb * GLA_SUB, :] if sb else jnp.zeros((1, LANES), F32)
                    inblk = (rowi >= sb * GLA_SUB) & (rowi < (sb + 1) * GLA_SUB)
                    qs.append(jnp.where(inblk, qp * jnp.exp(jnp.where(inblk, bp - beta, 0.0)), 0.0))
                    valid = rowi < (sb + 1) * GLA_SUB
                    ks.append(jnp.where(valid, kp * jnp.exp(jnp.where(valid, beta - bp, 0.0)), 0.0))
                prep.append(dict(qst=jnp.concatenate(qs, axis=1), kst=jnp.concatenate(ks, axis=1).astype(BF16),
                                 qinter=qp * jnp.exp(bp), kdec=(kp * jnp.exp(blast - bp)).astype(BF16),
                                 decay=jnp.exp(blast)))
        heads = [(u, p, i) for u in range(per_trip) for p in range(2) for i in range(2)]
        hcols = [slice((2 * p + i) * GLA_DV, (2 * p + i + 1) * GLA_DV) for _, p, i in heads]
        vhs = [v_ref[rows_u[u], hc] for (u, _, _), hc in zip(heads, hcols)]
        a_s = [_dot_nt(jnp.where((lane_st & GLA_DK) == i * GLA_DK, prep[2 * u + p]["qst"], 0.0).astype(BF16),
                       prep[2 * u + p]["kst"]) for u, p, i in heads]
        upd = [_dot_tn(vh, prep[2 * u + p]["kdec"]) for (u, p, i), vh in zip(heads, vhs)]
        a_s = [jnp.where(acol <= arow, a, 0.0).astype(BF16) for a in a_s]
        intra = [_dot(a, vh) for a, vh in zip(a_s, vhs)]
        states = [st_scr[p] for p in range(2)]
        for u in range(per_trip):
            for p in range(2):
                new = states[2 * u + p] * prep[2 * u + p]["decay"] + jnp.where(
                    lane_sq < GLA_DK, upd[4 * u + 2 * p], upd[4 * u + 2 * p + 1])
                states.append(new)
        for p in range(2):
            st_scr[p] = states[2 * per_trip + p]
        stb = [st.astype(BF16) for st in states[:2 * per_trip]]
        o_s = [o_in + _dot_nt(jnp.where((lane & GLA_DK) == i * GLA_DK, prep[2 * u + p]["qinter"], 0.0)
                              .astype(BF16), stb[2 * u + p]) for (u, p, i), o_in in zip(heads, intra)]
        for (u, _, _), o, hc in zip(heads, o_s, hcols):
            on = o * lax.rsqrt(jnp.mean(o * o, axis=-1, keepdims=True) + EPS) * gon_ref[...]
            gate = gg_ref[rows_u[u], hc].astype(F32)
            o_ref[rows_u[u], hc] = (on * gate * (1.0 / (1.0 + jnp.exp(-gate)))).astype(BF16)
        return carry

    lax.fori_loop(0, nchunk // per_trip, body, 0)

    @pl.when(t == pl.num_programs(1) - 1)
    def _():
        for p in range(2):
            sfin_ref[0, p] = st_scr[p].T


def _gla(z, la, s0, gon, *, batch, seq):
    chunk = min(seq, GLA_CHUNK)
    tb = min(seq, 512)
    nt = seq // tb
    tri = jnp.tril(jnp.ones((chunk, chunk), F32)).astype(BF16)
    qk_w, v_w = GLA_QK, GLA_V
    rows = lambda col: (lambda b, t: (b * nt + t, col))
    state_spec = pl.BlockSpec((1, 2, LANES, LANES), lambda b, t: (b, 0, 0, 0))
    o, sfin = pl.pallas_call(
        functools.partial(_gla_kernel, chunk=chunk, nchunk=tb // chunk),
        grid=(batch, nt),
        in_specs=[pl.BlockSpec((tb, qk_w), rows(ZC_GQ * LANES // qk_w)),
                  pl.BlockSpec((tb, qk_w), rows(ZC_GK * LANES // qk_w)),
                  pl.BlockSpec((tb, v_w), rows(ZC_GV * LANES // v_w)),
                  pl.BlockSpec((tb, v_w), rows(ZC_GG * LANES // v_w)),
                  pl.BlockSpec((tb, qk_w), rows(0)), state_spec,
                  _const_spec((1, GLA_DV)), _const_spec((chunk, chunk))],
        out_specs=[pl.BlockSpec((tb, v_w), lambda b, t: (b * nt + t, 0)), state_spec],
        out_shape=[jax.ShapeDtypeStruct((batch * seq, v_w), BF16),
                   jax.ShapeDtypeStruct((batch, 2, LANES, LANES), F32)],
        scratch_shapes=[pltpu.VMEM((2, LANES, LANES), F32)],
        compiler_params=_params(("parallel", "arbitrary")),
    )(z, z, z, z, la, s0.reshape(batch, 2, LANES, LANES), gon, tri)
    return o, sfin.reshape(batch, GLA_HEADS, GLA_DK, GLA_DV)


def _fox_kernel(*refs, seq, past, tq, has_past):
    if has_past:
        (q_ref, kc_ref, vc_ref, lfc_ref, triu_ref, kp_ref, vp_ref, lfp_ref, triup_ref,
         o_ref, cc_scr, cp_scr) = refs
    else:
        q_ref, kc_ref, vc_ref, lfc_ref, triu_ref, o_ref, cc_scr = refs
    i = pl.program_id(2)
    nq = seq // tq
    pblk = FOX_BLOCK
    past_blk = min(past, FOX_BLOCK * FOX_BLOCK // tq)

    @pl.when(i == 0)
    def _():
        carry = jnp.zeros((2, 1), F32)
        if has_past:
            for jb in range(past // pblk):
                x = lfp_ref[0, 0, :, jb * pblk:(jb + 1) * pblk]
                cp_scr[:, jb * pblk:(jb + 1) * pblk] = _dot_exact_rhs(x, triup_ref[...]) + carry
                carry = carry + jnp.sum(x, axis=1, keepdims=True)
        for jb in range(nq):
            x = lfc_ref[0, 0, :, jb * tq:(jb + 1) * tq]
            cc_scr[:, jb * tq:(jb + 1) * tq] = _dot_exact_rhs(x, triu_ref[...]) + carry
            carry = carry + jnp.sum(x, axis=1, keepdims=True)

    q = q_ref[...]
    lane = lax.broadcasted_iota(I32, (tq, LANES), 1)
    qh = [jnp.where(lane < FOX_HD, q, jnp.zeros_like(q)), jnp.where(lane >= FOX_HD, q, jnp.zeros_like(q))]

    causal = lax.broadcasted_iota(I32, (tq, tq), 1) <= lax.broadcasted_iota(I32, (tq, tq), 0)

    def attend(blocks):
        hs = range(2)
        s = [[_dot_nt(qh[h], kb) * (FOX_SCALE * LOG2E) - crows[h] * LOG2E for kb, _, crows, _ in blocks]
             for h in hs]
        s = [[s_b if blk[3] is None else jnp.where(blk[3], s_b, -jnp.inf) for s_b, blk in zip(s[h], blocks)]
             for h in hs]
        m = [functools.reduce(jnp.maximum, [jnp.max(s_b, axis=1, keepdims=True) for s_b in s[h]]) for h in hs]
        acc = []
        for h in hs:
            total = None
            for s_b, (_, vb, _, _) in zip(s[h], blocks):
                lane_v = lax.broadcasted_iota(I32, vb.shape, 1)
                own = (lane_v < FOX_HD) if h == 0 else (lane_v >= FOX_HD)
                pv = _dot(jnp.exp2(s_b - m[h]).astype(BF16), jnp.where(own, vb, jnp.ones_like(vb)))
                total = pv if total is None else total + pv
            acc.append(total)
        num = jnp.where(lane < FOX_HD, acc[0], acc[1])
        den = jnp.where(lane < FOX_HD, pltpu.roll(acc[0], FOX_HD, 1), pltpu.roll(acc[1], FOX_HD, 1))
        o_ref[...] = (num / den).astype(BF16)

    past_blocks = []
    if has_past:
        for off in range(0, past, past_blk):
            past_blocks.append((kp_ref[0, off:off + past_blk, :].astype(BF16),
                                vp_ref[0, off:off + past_blk, :].astype(BF16),
                                [cp_scr[h:h + 1, off:off + past_blk] for h in range(2)], None))

    def cur_block(j, mask):
        rows = slice(j * tq, (j + 1) * tq)
        return (kc_ref[rows, :], vc_ref[rows, :], [cc_scr[h:h + 1, rows] for h in range(2)], mask)

    for qi in range(nq):
        @pl.when(i == qi)
        def _(qi=qi):
            attend(past_blocks + [cur_block(j, None) for j in range(qi)] + [cur_block(qi, causal)])


def _fox(z, lf_cur_t, *, batch, seq, k_past=None, v_past=None, lf_past_t=None):
    has_past = k_past is not None
    past = k_past.shape[1] if has_past else 0
    tq = min(seq, FOX_BLOCK)
    nq = seq // tq
    triu = jnp.triu(jnp.ones((tq, tq), F32)).astype(BF16)
    in_specs = [pl.BlockSpec((tq, LANES), lambda b, p, i: (b * nq + i, ZC_FQ + p)),
                pl.BlockSpec((seq, LANES), lambda b, p, i: (b, ZC_FK + p)),
                pl.BlockSpec((seq, LANES), lambda b, p, i: (b, ZC_FV + p)),
                pl.BlockSpec((1, 1, 2, seq), lambda b, p, i: (b, p, 0, 0)),
                _const_spec((tq, tq))]
    args = [z, z, z, lf_cur_t, triu]
    scratch = [pltpu.VMEM((2, seq), F32)]
    if has_past:
        triup = jnp.triu(jnp.ones((FOX_BLOCK, FOX_BLOCK), F32)).astype(BF16)
        in_specs += [pl.BlockSpec((1, past, LANES), lambda b, p, i: (b, 0, p)),
                     pl.BlockSpec((1, past, LANES), lambda b, p, i: (b, 0, p)),
                     pl.BlockSpec((1, 1, 2, past), lambda b, p, i: (b, p, 0, 0)),
                     _const_spec((FOX_BLOCK, FOX_BLOCK))]
        args += [k_past, v_past, lf_past_t, triup]
        scratch.append(pltpu.VMEM((2, past), F32))
    return pl.pallas_call(
        functools.partial(_fox_kernel, seq=seq, past=past, tq=tq, has_past=has_past),
        grid=(batch, FOX_HEADS // 2, nq),
        in_specs=in_specs,
        out_specs=pl.BlockSpec((tq, LANES), lambda b, p, i: (b * nq + i, p)),
        out_shape=jax.ShapeDtypeStruct((batch * seq, FOX_W), BF16),
        scratch_shapes=scratch,
        compiler_params=_params(("parallel", "parallel", "arbitrary")),
    )(*args)


def _outproj_kernel(og_ref, of_ref, x_ref, w_ref, g2_ref, xo_ref, h2_ref):
    mixed = _dot(og_ref[...], w_ref[0:GLA_V, :]) + _dot(of_ref[...], w_ref[GLA_V:GLA_V + FOX_W, :])
    x = x_ref[...] + mixed
    xo_ref[...] = x
    ms = jnp.mean(x * x, axis=-1, keepdims=True)
    h2_ref[...] = (x * lax.rsqrt(ms + EPS) * g2_ref[...]).astype(BF16)


def _outproj(og, of, x, w, g2):
    n = x.shape[0]
    tm = _tile(n, 512)
    row = lambda i: (i, 0)
    return pl.pallas_call(
        _outproj_kernel,
        grid=(n // tm,),
        in_specs=[pl.BlockSpec((tm, GLA_V), row), pl.BlockSpec((tm, FOX_W), row),
                  pl.BlockSpec((tm, D_MODEL), row), _const_spec(w.shape), _const_spec(g2.shape)],
        out_specs=[pl.BlockSpec((tm, D_MODEL), row), pl.BlockSpec((tm, D_MODEL), row)],
        out_shape=[jax.ShapeDtypeStruct((n, D_MODEL), F32), jax.ShapeDtypeStruct((n, D_MODEL), BF16)],
        compiler_params=_params(("parallel",)),
    )(og, of, x, w, g2)


def _merge_exchange_pairs(n):
    full = 1
    while full < n:
        full *= 2
    pairs = []
    p = 1
    while p < full:
        k = p
        while k >= 1:
            for j in range(k % p, full - k, 2 * k):
                for i in range(min(k, full - j - k)):
                    if (i + j) // (2 * p) == (i + j + k) // (2 * p):
                        pairs.append((i + j, i + j + k))
            k //= 2
        p *= 2
    return tuple((a, b) for a, b in pairs if b < n)


def _topk_keys(s, k):
    nrows, tn = s.shape
    groups = nrows // SUBLANES
    sub = lax.broadcasted_iota(I32, (SUBLANES, tn), 0)
    vals = [s[SUBLANES * r:SUBLANES * (r + 1), :] for r in range(groups)]
    idxs = [sub + SUBLANES * r for r in range(groups)]
    for a, b in _merge_exchange_pairs(groups):
        va, vb, ia, ib = vals[a], vals[b], idxs[a], idxs[b]
        a_first = (va > vb) | ((va == vb) & (ia < ib))
        vals[a], vals[b] = jnp.maximum(va, vb), jnp.minimum(va, vb)
        idxs[a], idxs[b] = jnp.where(a_first, ia, ib), jnp.where(a_first, ib, ia)
    out_v, out_i = [], []
    for t in range(k):
        head_v, head_i = vals[0], idxs[0]
        m = jnp.max(head_v, axis=0, keepdims=True)
        idx = jnp.min(jnp.where(head_v == m, head_i, nrows), axis=0, keepdims=True)
        out_v.append(m)
        out_i.append(idx)
        won = head_i == idx
        for r in range(min(k - 1 - t, groups)):
            last = r + 1 == groups
            vals[r] = jnp.where(won, -jnp.inf if last else vals[r + 1], vals[r])
            idxs[r] = jnp.where(won, nrows if last else idxs[r + 1], idxs[r])
    return jnp.concatenate(out_v, axis=0), jnp.concatenate(out_i, axis=0)


def _pair_topk(v1, i1, v2, i2):
    tn = v1.shape[1]
    iota_k = lax.broadcasted_iota(I32, (PEER_TOPK, tn), 0)
    iota_8 = lax.broadcasted_iota(I32, (8, tn), 0)

    def pick(table, sel):
        rows = []
        for r in range(PEER_TOPK):
            rows.append(jnp.sum(jnp.where(iota_k == sel[r:r + 1, :], table, 0), axis=0, keepdims=True))
        return jnp.concatenate(rows, axis=0)

    blocks = [v1[0:1, :] + v2]
    for a in range(1, 8):
        blocks.append(jnp.where(iota_8 < PEER_TOPK // (a + 1), v1[a:a + 1, :] + v2[0:8, :], -jnp.inf))
    blocks.append(v1[8:16, :] + v2[0:1, :])
    top, ti = _topk_keys(jnp.concatenate(blocks, axis=0), PEER_TOPK)
    mid = ti - PEER_TOPK
    e1 = pick(i1, jnp.where(ti < 16, 0, jnp.where(ti < 72, (mid >> 3) + 1, ti - 64)))
    e2 = pick(i2, jnp.where(ti < 16, ti, jnp.where(ti < 72, mid & 7, 0)))
    ex = jnp.exp(top - jnp.max(top, axis=0, keepdims=True))
    return e1, e2, ex / jnp.sum(ex, axis=0, keepdims=True)


PEER_ROWS = 16
PEER_EC = PEER_ROWS * PEER_NKEYS
assert PEER_NEXP // PEER_EC == PEER_HEADS


def _gelu(a):
    return 0.5 * a * (1.0 + lax.erf(a * (2.0 ** -0.5)))


def _route_act_kernel(hn_ref, hp_ref, wq_ref, sk_ref, ut_ref, e1_ref, e2_ref, c_ref,
                      q_scr, r1_scr, r2_scr, rg_scr, t1_scr, t2_scr, tg_scr, acc_scr):
    i = pl.program_id(0)
    c = pl.program_id(1)
    slot_new = i % 2
    slot_old = 1 - slot_new

    @pl.when((i == 0) & (c == 0))
    def _():
        t1_scr[1] = jnp.zeros_like(t1_scr[1])
        t2_scr[1] = jnp.zeros_like(t2_scr[1])
        tg_scr[1] = jnp.zeros_like(tg_scr[1])

    @pl.when(c == 0)
    def _():
        acc_scr[...] = jnp.zeros_like(acc_scr)
        q = _dot(hn_ref[...], wq_ref[...])
        for hc in range(2 * PEER_HEADS):
            q_scr[hc] = q[:, hc * LANES:(hc + 1) * LANES].astype(BF16)

    v1, i1 = _topk_keys(_dot_nt(sk_ref[2 * c], q_scr[2 * c]), PEER_TOPK)
    v2, i2 = _topk_keys(_dot_nt(sk_ref[2 * c + 1], q_scr[2 * c + 1]), PEER_TOPK)
    e1h, e2h, gh = _pair_topk(v1, i1, v2, i2)
    rows = pl.ds(pl.multiple_of(c * PEER_TOPK, PEER_TOPK), PEER_TOPK)
    r1_scr[rows, :] = e1h
    r2_scr[rows, :] = e2h
    rg_scr[rows, :] = gh

    a_all = _dot(hp_ref[...], ut_ref[...])
    e1 = t1_scr[slot_old]
    e2 = t2_scr[slot_old]
    acc = acc_scr[...]
    for r in range(PEER_ROWS):
        picked = jnp.take_along_axis(a_all[:, r * LANES:(r + 1) * LANES], e2, axis=1,
                                     mode="promise_in_bounds")
        acc = jnp.where(e1 == c * PEER_ROWS + r, picked, acc)
    acc_scr[...] = acc

    @pl.when(c == pl.num_programs(1) - 1)
    def _():
        c_ref[...] = tg_scr[slot_old] * _gelu(acc)
        t1_scr[slot_new] = r1_scr[...].T
        t2_scr[slot_new] = r2_scr[...].T
        tg_scr[slot_new] = rg_scr[...].T
        e1_ref[...] = t1_scr[slot_new]
        e2_ref[...] = t2_scr[slot_new]


def _route_act(h2, wq, sk, ut):
    n = h2.shape[0]
    tn = _tile(n, 1024)
    tiles = n // tn
    new = lambda i, c: (jnp.minimum(i, tiles - 1), 0)
    old = lambda i, c: (jnp.maximum(i - 1, 0), 0)
    return pl.pallas_call(
        _route_act_kernel,
        grid=(tiles + 1, PEER_HEADS),
        in_specs=[pl.BlockSpec((tn, D_MODEL), new), pl.BlockSpec((tn, D_MODEL), old),
                  _const_spec(wq.shape), _const_spec(sk.shape),
                  pl.BlockSpec((D_MODEL, PEER_EC), lambda i, c: (0, c))],
        out_specs=[pl.BlockSpec((tn, PEER_J), new), pl.BlockSpec((tn, PEER_J), new),
                   pl.BlockSpec((tn, PEER_J), old)],
        out_shape=[jax.ShapeDtypeStruct((n, PEER_J), I32), jax.ShapeDtypeStruct((n, PEER_J), I32),
                   jax.ShapeDtypeStruct((n, PEER_J), F32)],
        scratch_shapes=[pltpu.VMEM((2 * PEER_HEADS, tn, LANES), BF16),
                        pltpu.VMEM((PEER_J, tn), I32), pltpu.VMEM((PEER_J, tn), I32),
                        pltpu.VMEM((PEER_J, tn), F32),
                        pltpu.VMEM((2, tn, PEER_J), I32), pltpu.VMEM((2, tn, PEER_J), I32),
                        pltpu.VMEM((2, tn, PEER_J), F32), pltpu.VMEM((tn, PEER_J), F32)],
        compiler_params=_params(("arbitrary", "arbitrary")),
    )(h2, h2, wq, sk, ut)


PEER_HALF = PEER_ROWS // 2


def _peer_out_kernel(e1_ref, e2_ref, c_ref, v_ref, x_ref, o_ref, y_scr, acc_scr):
    tn = x_ref.shape[0]
    c = pl.program_id(1)
    nchunks = PEER_NEXP // PEER_EC

    @pl.when(c == 0)
    def _():
        acc_scr[...] = jnp.zeros_like(acc_scr)
        key = lax.broadcasted_iota(I32, (PEER_NKEYS, PEER_J), 0)

        def token(t, carry):
            row = pl.ds(t, 1)
            d = jnp.where(key == e1_ref[row, :], c_ref[row, :], 0.0).astype(BF16)
            w = jnp.where(key == e2_ref[row, :], 1.0, 0.0).astype(BF16)
            y = _dot_nt(d, w).astype(BF16).astype(F32)
            bits = lax.bitcast_convert_type(y, I32)
            base = pl.multiple_of(t * PEER_HALF, PEER_HALF)
            for cc in range(nchunks):
                hi = bits[cc * PEER_ROWS:cc * PEER_ROWS + PEER_HALF, :]
                lo = bits[cc * PEER_ROWS + PEER_HALF:(cc + 1) * PEER_ROWS, :]
                y_scr[cc, pl.ds(base, PEER_HALF), :] = hi | lax.shift_right_logical(lo, 16)
            return carry

        lax.fori_loop(0, tn, token, 0, unroll=32)

    his, los = [], []
    for i in range(PEER_HALF):
        words = y_scr[c, pl.ds(i, tn, stride=PEER_HALF), :]
        his.append(lax.bitcast_convert_type(words & -65536, F32).astype(BF16))
        los.append(lax.bitcast_convert_type(words << 16, F32).astype(BF16))
    acc_scr[...] += _dot(jnp.concatenate(his + los, axis=1), v_ref[...])

    @pl.when(c == nchunks - 1)
    def _():
        o_ref[...] = x_ref[...] + acc_scr[...]


def _peer_out(e1, e2, cj, vt, x):
    n = x.shape[0]
    tn = _tile(n, 512)
    row = lambda i, c: (i, 0)
    nchunks = PEER_NEXP // PEER_EC
    return pl.pallas_call(
        _peer_out_kernel,
        grid=(n // tn, nchunks),
        in_specs=[pl.BlockSpec((tn, PEER_J), row), pl.BlockSpec((tn, PEER_J), row),
                  pl.BlockSpec((tn, PEER_J), row), pl.BlockSpec((PEER_EC, D_MODEL), lambda i, c: (c, 0)),
                  pl.BlockSpec((tn, D_MODEL), row)],
        out_specs=pl.BlockSpec((tn, D_MODEL), row),
        out_shape=jax.ShapeDtypeStruct((n, D_MODEL), F32),
        scratch_shapes=[pltpu.VMEM((nchunks, tn * PEER_HALF, LANES), I32),
                        pltpu.VMEM((tn, D_MODEL), F32)],
        compiler_params=_params(("parallel", "arbitrary")),
    )(e1, e2, cj, vt, x)


def _pair_major(lf, batch, length):
    return lf.reshape(batch, length, FOX_HEADS // 2, 2).transpose(0, 2, 3, 1)


def _prep_weights(g_norm1, w_in, w_gla_a2, b_gla_a, g_gla_onorm, g_fox_qnorm, g_fox_knorm, b_fox_f, w_out,
                  g_norm2, w_peer_q, peer_subkeys, peer_u, peer_v):
    bounds = [0]
    for size in IN_SIZES:
        bounds.append(bounds[-1] + size)
    gq, gk, gv, gg, glr, fq, fk, fv, ff = [w_in[:, bounds[i]:bounds[i + 1]] for i in range(len(IN_SIZES))]
    pad = jnp.zeros((D_MODEL, Z_SMALL - FOX_HEADS - GLA_LOWRANK), w_in.dtype)
    w_cat = jnp.concatenate([gq, gk, gv, gg, fq, fk, fv, ff, glr, pad], axis=1).astype(BF16)
    wa2p = jnp.zeros((Z_SMALL, GLA_QK), F32).at[FOX_HEADS:FOX_HEADS + GLA_LOWRANK].set(w_gla_a2).astype(BF16)
    bsm = jnp.zeros((1, Z_SMALL), F32).at[0, :FOX_HEADS].set(b_fox_f)
    head_of_col = jnp.arange(FOX_W) // FOX_HD
    ind = (head_of_col[:, None] == jnp.arange(LANES)[None, :]).astype(BF16)
    return dict(
        inproj=(g_norm1[None], w_cat, wa2p, b_gla_a[None], bsm, jnp.tile(g_fox_qnorm, FOX_HEADS)[None],
                jnp.tile(g_fox_knorm, FOX_HEADS)[None], ind, ind.T),
        gon=g_gla_onorm[None], w_out=w_out.astype(BF16), g2=g_norm2[None], wq=w_peer_q.astype(BF16),
        sk=peer_subkeys.reshape(2 * PEER_HEADS, PEER_NKEYS, LANES).astype(BF16),
        ut=peer_u.T.astype(BF16), v=peer_v.astype(BF16))


def _path_layer(x, wts, layer, depth, stacked, batch, seq, s0, k_past=None, v_past=None, lf_past=None):
    z, la, k_st, v_st, lf_st = _inproj(x, wts["inproj"], layer, depth, stacked)
    og, state = _gla(z, la, s0, wts["gon"], batch=batch, seq=seq)
    past_args = {}
    if k_past is not None:
        past = k_past.shape[1]
        past_args = dict(k_past=k_past.reshape(batch, past, FOX_W),
                         v_past=v_past.reshape(batch, past, FOX_W),
                         lf_past_t=lf_past.reshape(batch, past, FOX_HEADS // 2, 2).transpose(0, 2, 3, 1))
    of = _fox(z, _pair_major(lf_st[layer], batch, seq), batch=batch, seq=seq, **past_args)
    x, h2 = _outproj(og, of, x, wts["w_out"], wts["g2"])
    e1, e2, cj = _route_act(h2, wts["wq"], wts["sk"], wts["ut"])
    x = _peer_out(e1, e2, cj, wts["v"], x)
    return x, (k_st, v_st, lf_st), state


def kernel(x_prompt, x_sample, cache_fox_k, cache_fox_v, cache_fox_logf, state_gla, g_norm1, w_in, w_gla_a2,
           b_gla_a, g_gla_onorm, g_fox_qnorm, g_fox_knorm, b_fox_f, w_out, g_norm2, w_peer_q, peer_subkeys,
           peer_u, peer_v):
    bp, tp, _ = x_prompt.shape
    bs, ts, _ = x_sample.shape
    depth = w_in.shape[0]
    xp = x_prompt.reshape(bp * tp, D_MODEL)
    xs = x_sample.reshape(bs * ts, D_MODEL)
    stk_p, stk_s, st_p, st_s = (), (), [], []
    zero_state = jnp.zeros((bp, GLA_HEADS, GLA_DK, GLA_DV), F32)
    for l in range(depth):
        wts = _prep_weights(g_norm1[l], w_in[l], w_gla_a2[l], b_gla_a[l], g_gla_onorm[l], g_fox_qnorm[l],
                            g_fox_knorm[l], b_fox_f[l], w_out[l], g_norm2[l], w_peer_q[l], peer_subkeys[l],
                            peer_u[l], peer_v[l])
        xp, stk_p, state = _path_layer(xp, wts, l, depth, stk_p, bp, tp, zero_state)
        st_p.append(state)
        xs, stk_s, state = _path_layer(xs, wts, l, depth, stk_s, bs, ts, state_gla[l],
                                       cache_fox_k[l], cache_fox_v[l], cache_fox_logf[l])
        st_s.append(state)

    def unstack(stk, b, t):
        k, v, lf = stk
        return (k.reshape(depth, b, t, FOX_HEADS, FOX_HD), v.reshape(depth, b, t, FOX_HEADS, FOX_HD),
                lf.reshape(depth, b, t, FOX_HEADS))

    return (xp.reshape(bp, tp, D_MODEL), xs.reshape(bs, ts, D_MODEL),
            *unstack(stk_p, bp, tp), jnp.stack(st_p), *unstack(stk_s, bs, ts), jnp.stack(st_s))
```

```python
import functools

import jax
import jax.numpy as jnp
from jax import lax
from jax.experimental import pallas as pl
from jax.experimental.pallas import tpu as pltpu

F32 = jnp.float32
BF16 = jnp.bfloat16
I32 = jnp.int32

D_MODEL = 1024
EPS = 1e-6
GLA_CHUNK = 64
GLA_HEADS = 4
GLA_DK = 64
GLA_DV = 128
GLA_LOWRANK = 16
GLA_GATE_NORM = 16.0
GLA_SUB = 16
FOX_HEADS = 8
FOX_HD = 64
FOX_SCALE = FOX_HD ** -0.5
LOG2E = 1.4426950408889634
FOX_BLOCK = 512
GLA_QK = GLA_HEADS * GLA_DK
GLA_V = GLA_HEADS * GLA_DV
FOX_W = FOX_HEADS * FOX_HD
IN_SIZES = (GLA_QK, GLA_QK, GLA_V, GLA_V, GLA_LOWRANK, FOX_W, FOX_W, FOX_W, FOX_HEADS)
PEER_HEADS = 8
PEER_NKEYS = 128
PEER_TOPK = 16
PEER_NEXP = PEER_NKEYS * PEER_NKEYS
PEER_J = PEER_HEADS * PEER_TOPK

LANES = 128
SUBLANES = 8
Z_MAIN = 3072
Z_SMALL = 128
VMEM_LIMIT = 56 * 1024 * 1024

ZC_GQ, ZC_GK, ZC_GV, ZC_GG, ZC_FQ, ZC_FK, ZC_FV = 0, 2, 4, 8, 12, 16, 20


def _params(sem):
    return pltpu.CompilerParams(dimension_semantics=sem, vmem_limit_bytes=VMEM_LIMIT)


def _tile(n, cap):
    t = cap
    while n % t:
        t //= 2
    return t


def _dot(a, b):
    return jnp.dot(a, b, preferred_element_type=F32)


def _dot_nt(a, b):
    return lax.dot_general(a, b, (((1,), (1,)), ((), ())), preferred_element_type=F32)


def _dot_tn(a, b):
    return lax.dot_general(a, b, (((0,), (0,)), ((), ())), preferred_element_type=F32)


def _split(x):
    hi = x.astype(BF16)
    lo = (x - hi.astype(F32)).astype(BF16)
    return hi, lo


def _dot_exact_rhs(x, m):
    hi, lo = _split(x)
    return _dot(hi, m) + _dot(lo, m)


def _dot_exact_lhs(m, x):
    hi, lo = _split(x)
    return _dot(m, hi) + _dot(m, lo)


def _log_sigmoid(y):
    return jnp.minimum(y, 0.0) - jnp.log(1.0 + jnp.exp(-jnp.abs(y)))


def _const_spec(shape):
    nd = len(shape)
    return pl.BlockSpec(shape, lambda *_: (0,) * nd)


def _inproj_kernel(*refs, n_alias, layer):
    (x_ref, g1_ref, w_ref, wa2_ref, ba_ref, bsm_ref, gq_ref, gk_ref, ind_ref, indt_ref) = refs[:10]
    z_ref, la_ref, k32_ref, v32_ref, lf_ref = refs[10 + n_alias:]
    for stack_ref in (k32_ref, v32_ref, lf_ref):
        for other in range(stack_ref.shape[0]):
            if other != layer:
                stack_ref[other] = jnp.zeros(stack_ref.shape[1:], F32)
    x = x_ref[...]
    ms = jnp.mean(x * x, axis=-1, keepdims=True)
    h = (x * lax.rsqrt(ms + EPS) * g1_ref[...]).astype(BF16)
    z = _dot(h, w_ref[...])

    c_fq, c_fk, c_fv = ZC_FQ * LANES, ZC_FK * LANES, ZC_FV * LANES
    raw = [z[:, c_fq:c_fk], z[:, c_fk:c_fv]]
    ss = [_dot_exact_rhs(t * t, ind_ref[...]) for t in raw]
    scale = [_dot_exact_rhs(lax.rsqrt(v * (1.0 / FOX_HD) + EPS), indt_ref[...]) for v in ss]
    fq = raw[0] * scale[0] * gq_ref[...]
    fk = raw[1] * scale[1] * gk_ref[...]
    fv = z[:, c_fv:Z_MAIN]
    z_ref[:, 0:c_fq] = z[:, 0:c_fq].astype(BF16)
    z_ref[:, c_fq:c_fk] = fq.astype(BF16)
    z_ref[:, c_fk:c_fv] = fk.astype(BF16)
    z_ref[:, c_fv:Z_MAIN] = fv.astype(BF16)
    tm = x_ref.shape[0]
    for hd in range(FOX_HEADS):
        token_rows = pl.ds(hd, tm, stride=FOX_HEADS)
        k32_ref[layer, token_rows, :] = fk[:, hd * FOX_HD:(hd + 1) * FOX_HD]
        v32_ref[layer, token_rows, :] = fv[:, hd * FOX_HD:(hd + 1) * FOX_HD]
    small = z[:, Z_MAIN:Z_MAIN + Z_SMALL]
    lf_ref[layer] = _log_sigmoid(small + bsm_ref[...])[:, 0:FOX_HEADS]
    y = _dot(small.astype(BF16), wa2_ref[...]) + ba_ref[...]
    la_ref[...] = _log_sigmoid(y) * (1.0 / GLA_GATE_NORM)


def _inproj(x, consts, layer, depth, stacked):
    n = x.shape[0]
    tm = _tile(n, 512)
    row = lambda i: (i, 0)
    n_alias = len(stacked)
    slabs, slab = (1, lambda i: (layer, i, 0)) if n_alias else (depth, lambda i: (0, i, 0))
    kv_block = (slabs, tm * FOX_HEADS, FOX_HD)
    stack_shapes = ((depth, n * FOX_HEADS, FOX_HD), (depth, n * FOX_HEADS, FOX_HD), (depth, n, FOX_HEADS))
    return pl.pallas_call(
        functools.partial(_inproj_kernel, n_alias=n_alias, layer=0 if n_alias else layer),
        grid=(n // tm,),
        in_specs=[pl.BlockSpec((tm, D_MODEL), row)] + [_const_spec(c.shape) for c in consts]
        + [pl.BlockSpec(memory_space=pl.ANY)] * n_alias,
        out_specs=[pl.BlockSpec((tm, Z_MAIN), row), pl.BlockSpec((tm, GLA_QK), row)]
        + [pl.BlockSpec(kv_block, slab), pl.BlockSpec(kv_block, slab),
           pl.BlockSpec((slabs, tm, FOX_HEADS), slab)],
        out_shape=[jax.ShapeDtypeStruct((n, Z_MAIN), BF16), jax.ShapeDtypeStruct((n, GLA_QK), F32)]
        + [jax.ShapeDtypeStruct(shape, F32) for shape in stack_shapes],
        input_output_aliases={1 + len(consts) + i: 2 + i for i in range(n_alias)},
        compiler_params=_params(("parallel",)),
    )(x, *consts, *stacked)


def _gla_kernel(q_ref, k_ref, v_ref, gg_ref, la_ref, s0_ref, gon_ref, tri_ref, o_ref, sfin_ref, st_scr,
                *, chunk, nchunk):
    t = pl.program_id(1)
    nsub = chunk // GLA_SUB

    @pl.when(t == 0)
    def _():
        for p in range(2):
            st_scr[p] = s0_ref[0, p].T

    rowi = lax.broadcasted_iota(I32, (chunk, LANES), 0)
    lane = lax.broadcasted_iota(I32, (chunk, LANES), 1)
    lane_st = lax.broadcasted_iota(I32, (chunk, nsub * LANES), 1)
    lane_sq = lax.broadcasted_iota(I32, (LANES, LANES), 1)
    arow = lax.broadcasted_iota(I32, (chunk, chunk), 0)
    acol = lax.broadcasted_iota(I32, (chunk, chunk), 1)

    per_trip = next(n for n in (8, 4, 2, 1) if nchunk % n == 0)

    def body(trip, carry):
        rows_u, prep = [], []
        for u in range(per_trip):
            rows = pl.ds(pl.multiple_of((trip * per_trip + u) * chunk, chunk), chunk)
            rows_u.append(rows)
            bcum = _dot_exact_lhs(tri_ref[...], la_ref[rows, :])
            q = q_ref[rows, :].astype(F32) * (GLA_DK ** -0.5)
            k = k_ref[rows, :].astype(F32)
            for p in range(2):
                cols = slice(p * LANES, (p + 1) * LANES)
                bp, qp, kp = bcum[:, cols], q[:, cols], k[:, cols]
                blast = bp[chunk - 1:chunk, :]
                qs, ks = [], []
                for sb in range(nsub):
                    beta = bp[sb * GLA_SUB - 1:sb * GLA_SUB, :] if sb else jnp.zeros((1, LANES), F32)
                    inblk = (rowi >= sb * GLA_SUB) & (rowi < (sb + 1) * GLA_SUB)
                    qs.append(jnp.where(inblk, qp * jnp.exp(jnp.where(inblk, bp - beta, 0.0)), 0.0))
                    valid = rowi < (sb + 1) * GLA_SUB
                    ks.append(jnp.where(valid, kp * jnp.exp(jnp.where(valid, beta - bp, 0.0)), 0.0))
                prep.append(dict(qst=jnp.concatenate(qs, axis=1), kst=jnp.concatenate(ks, axis=1).astype(BF16),
                                 qinter=qp * jnp.exp(bp), kdec=(kp * jnp.exp(blast - bp)).astype(BF16),
                                 decay=jnp.exp(blast)))
        heads = [(u, p, i) for u in range(per_trip) for p in range(2) for i in range(2)]
        hcols = [slice((2 * p + i) * GLA_DV, (2 * p + i + 1) * GLA_DV) for _, p, i in heads]
        vhs = [v_ref[rows_u[u], hc] for (u, _, _), hc in zip(heads, hcols)]
        a_s = [_dot_nt(jnp.where((lane_st & GLA_DK) == i * GLA_DK, prep[2 * u + p]["qst"], 0.0).astype(BF16),
                       prep[2 * u + p]["kst"]) for u, p, i in heads]
        upd = [_dot_tn(vh, prep[2 * u + p]["kdec"]) for (u, p, i), vh in zip(heads, vhs)]
        a_s = [jnp.where(acol <= arow, a, 0.0).astype(BF16) for a in a_s]
        intra = [_dot(a, vh) for a, vh in zip(a_s, vhs)]
        states = [st_scr[p] for p in range(2)]
        for u in range(per_trip):
            for p in range(2):
                new = states[2 * u + p] * prep[2 * u + p]["decay"] + jnp.where(
                    lane_sq < GLA_DK, upd[4 * u + 2 * p], upd[4 * u + 2 * p + 1])
                states.append(new)
        for p in range(2):
            st_scr[p] = states[2 * per_trip + p]
        stb = [st.astype(BF16) for st in states[:2 * per_trip]]
        o_s = [o_in + _dot_nt(jnp.where((lane & GLA_DK) == i * GLA_DK, prep[2 * u + p]["qinter"], 0.0)
                              .astype(BF16), stb[2 * u + p]) for (u, p, i), o_in in zip(heads, intra)]
        for (u, _, _), o, hc in zip(heads, o_s, hcols):
            on = o * lax.rsqrt(jnp.mean(o * o, axis=-1, keepdims=True) + EPS) * gon_ref[...]
            gate = gg_ref[rows_u[u], hc].astype(F32)
            o_ref[rows_u[u], hc] = (on * gate * (1.0 / (1.0 + jnp.exp(-gate)))).astype(BF16)
        return carry

    lax.fori_loop(0, nchunk // per_trip, body, 0)

    @pl.when(t == pl.num_programs(1) - 1)
    def _():
        for p in range(2):
            sfin_ref[0, p] = st_scr[p].T


def _gla(z, la, s0, gon, *, batch, seq):
    chunk = min(seq, GLA_CHUNK)
    tb = min(seq, 512)
    nt = seq // tb
    tri = jnp.tril(jnp.ones((chunk, chunk), F32)).astype(BF16)
    qk_w, v_w = GLA_QK, GLA_V
    rows = lambda col: (lambda b, t: (b * nt + t, col))
    state_spec = pl.BlockSpec((1, 2, LANES, LANES), lambda b, t: (b, 0, 0, 0))
    o, sfin = pl.pallas_call(
        functools.partial(_gla_kernel, chunk=chunk, nchunk=tb // chunk),
        grid=(batch, nt),
        in_specs=[pl.BlockSpec((tb, qk_w), rows(ZC_GQ * LANES // qk_w)),
                  pl.BlockSpec((tb, qk_w), rows(ZC_GK * LANES // qk_w)),
                  pl.BlockSpec((tb, v_w), rows(ZC_GV * LANES // v_w)),
                  pl.BlockSpec((tb, v_w), rows(ZC_GG * LANES // v_w)),
                  pl.BlockSpec((tb, qk_w), rows(0)), state_spec,
                  _const_spec((1, GLA_DV)), _const_spec((chunk, chunk))],
        out_specs=[pl.BlockSpec((tb, v_w), lambda b, t: (b * nt + t, 0)), state_spec],
        out_shape=[jax.ShapeDtypeStruct((batch * seq, v_w), BF16),
                   jax.ShapeDtypeStruct((batch, 2, LANES, LANES), F32)],
        scratch_shapes=[pltpu.VMEM((2, LANES, LANES), F32)],
        compiler_params=_params(("parallel", "arbitrary")),
    )(z, z, z, z, la, s0.reshape(batch, 2, LANES, LANES), gon, tri)
    return o, sfin.reshape(batch, GLA_HEADS, GLA_DK, GLA_DV)


def _fox_kernel(*refs, seq, past, tq, has_past):
    if has_past:
        (q_ref, kc_ref, vc_ref, lfc_ref, triu_ref, kp_ref, vp_ref, lfp_ref, triup_ref,
         o_ref, cc_scr, cp_scr) = refs
    else:
        q_ref, kc_ref, vc_ref, lfc_ref, triu_ref, o_ref, cc_scr = refs
    i = pl.program_id(2)
    nq = seq // tq
    pblk = FOX_BLOCK
    past_blk = min(past, FOX_BLOCK * FOX_BLOCK // tq)

    @pl.when(i == 0)
    def _():
        carry = jnp.zeros((2, 1), F32)
        if has_past:
            for jb in range(past // pblk):
                x = lfp_ref[0, 0, :, jb * pblk:(jb + 1) * pblk]
                cp_scr[:, jb * pblk:(jb + 1) * pblk] = _dot_exact_rhs(x, triup_ref[...]) + carry
                carry = carry + jnp.sum(x, axis=1, keepdims=True)
        for jb in range(nq):
            x = lfc_ref[0, 0, :, jb * tq:(jb + 1) * tq]
            cc_scr[:, jb * tq:(jb + 1) * tq] = _dot_exact_rhs(x, triu_ref[...]) + carry
            carry = carry + jnp.sum(x, axis=1, keepdims=True)

    q = q_ref[...]
    lane = lax.broadcasted_iota(I32, (tq, LANES), 1)
    qh = [jnp.where(lane < FOX_HD, q, jnp.zeros_like(q)), jnp.where(lane >= FOX_HD, q, jnp.zeros_like(q))]

    causal = lax.broadcasted_iota(I32, (tq, tq), 1) <= lax.broadcasted_iota(I32, (tq, tq), 0)

    def attend(blocks):
        hs = range(2)
        s = [[_dot_nt(qh[h], kb) * (FOX_SCALE * LOG2E) - crows[h] * LOG2E for kb, _, crows, _ in blocks]
             for h in hs]
        s = [[s_b if blk[3] is None else jnp.where(blk[3], s_b, -jnp.inf) for s_b, blk in zip(s[h], blocks)]
             for h in hs]
        m = [functools.reduce(jnp.maximum, [jnp.max(s_b, axis=1, keepdims=True) for s_b in s[h]]) for h in hs]
        acc = []
        for h in hs:
            total = None
            for s_b, (_, vb, _, _) in zip(s[h], blocks):
                lane_v = lax.broadcasted_iota(I32, vb.shape, 1)
                own = (lane_v < FOX_HD) if h == 0 else (lane_v >= FOX_HD)
                pv = _dot(jnp.exp2(s_b - m[h]).astype(BF16), jnp.where(own, vb, jnp.ones_like(vb)))
                total = pv if total is None else total + pv
            acc.append(total)
        num = jnp.where(lane < FOX_HD, acc[0], acc[1])
        den = jnp.where(lane < FOX_HD, pltpu.roll(acc[0], FOX_HD, 1), pltpu.roll(acc[1], FOX_HD, 1))
        o_ref[...] = (num / den).astype(BF16)

    past_blocks = []
    if has_past:
        for off in range(0, past, past_blk):
            past_blocks.append((kp_ref[0, off:off + past_blk, :].astype(BF16),
                                vp_ref[0, off:off + past_blk, :].astype(BF16),
                                [cp_scr[h:h + 1, off:off + past_blk] for h in range(2)], None))

    def cur_block(j, mask):
        rows = slice(j * tq, (j + 1) * tq)
        return (kc_ref[rows, :], vc_ref[rows, :], [cc_scr[h:h + 1, rows] for h in range(2)], mask)

    for qi in range(nq):
        @pl.when(i == qi)
        def _(qi=qi):
            attend(past_blocks + [cur_block(j, None) for j in range(qi)] + [cur_block(qi, causal)])


def _fox(z, lf_cur_t, *, batch, seq, k_past=None, v_past=None, lf_past_t=None):
    has_past = k_past is not None
    past = k_past.shape[1] if has_past else 0
    tq = min(seq, FOX_BLOCK)
    nq = seq // tq
    triu = jnp.triu(jnp.ones((tq, tq), F32)).astype(BF16)
    in_specs = [pl.BlockSpec((tq, LANES), lambda b, p, i: (b * nq + i, ZC_FQ + p)),
                pl.BlockSpec((seq, LANES), lambda b, p, i: (b, ZC_FK + p)),
                pl.BlockSpec((seq, LANES), lambda b, p, i: (b, ZC_FV + p)),
                pl.BlockSpec((1, 1, 2, seq), lambda b, p, i: (b, p, 0, 0)),
                _const_spec((tq, tq))]
    args = [z, z, z, lf_cur_t, triu]
    scratch = [pltpu.VMEM((2, seq), F32)]
    if has_past:
        triup = jnp.triu(jnp.ones((FOX_BLOCK, FOX_BLOCK), F32)).astype(BF16)
        in_specs += [pl.BlockSpec((1, past, LANES), lambda b, p, i: (b, 0, p)),
                     pl.BlockSpec((1, past, LANES), lambda b, p, i: (b, 0, p)),
                     pl.BlockSpec((1, 1, 2, past), lambda b, p, i: (b, p, 0, 0)),
                     _const_spec((FOX_BLOCK, FOX_BLOCK))]
        args += [k_past, v_past, lf_past_t, triup]
        scratch.append(pltpu.VMEM((2, past), F32))
    return pl.pallas_call(
        functools.partial(_fox_kernel, seq=seq, past=past, tq=tq, has_past=has_past),
        grid=(batch, FOX_HEADS // 2, nq),
        in_specs=in_specs,
        out_specs=pl.BlockSpec((tq, LANES), lambda b, p, i: (b * nq + i, p)),
        out_shape=jax.ShapeDtypeStruct((batch * seq, FOX_W), BF16),
        scratch_shapes=scratch,
        compiler_params=_params(("parallel", "parallel", "arbitrary")),
    )(*args)


def _outproj_kernel(og_ref, of_ref, x_ref, w_ref, g2_ref, xo_ref, h2_ref):
    mixed = _dot(og_ref[...], w_ref[0:GLA_V, :]) + _dot(of_ref[...], w_ref[GLA_V:GLA_V + FOX_W, :])
    x = x_ref[...] + mixed
    xo_ref[...] = x
    ms = jnp.mean(x * x, axis=-1, keepdims=True)
    h2_ref[...] = (x * lax.rsqrt(ms + EPS) * g2_ref[...]).astype(BF16)


def _outproj(og, of, x, w, g2):
    n = x.shape[0]
    tm = _tile(n, 512)
    row = lambda i: (i, 0)
    return pl.pallas_call(
        _outproj_kernel,
        grid=(n // tm,),
        in_specs=[pl.BlockSpec((tm, GLA_V), row), pl.BlockSpec((tm, FOX_W), row),
                  pl.BlockSpec((tm, D_MODEL), row), _const_spec(w.shape), _const_spec(g2.shape)],
        out_specs=[pl.BlockSpec((tm, D_MODEL), row), pl.BlockSpec((tm, D_MODEL), row)],
        out_shape=[jax.ShapeDtypeStruct((n, D_MODEL), F32), jax.ShapeDtypeStruct((n, D_MODEL), BF16)],
        compiler_params=_params(("parallel",)),
    )(og, of, x, w, g2)


def _merge_exchange_pairs(n):
    full = 1
    while full < n:
        full *= 2
    pairs = []
    p = 1
    while p < full:
        k = p
        while k >= 1:
            for j in range(k % p, full - k, 2 * k):
                for i in range(min(k, full - j - k)):
                    if (i + j) // (2 * p) == (i + j + k) // (2 * p):
                        pairs.append((i + j, i + j + k))
            k //= 2
        p *= 2
    return tuple((a, b) for a, b in pairs if b < n)


def _topk_keys(s, k):
    nrows, tn = s.shape
    groups = nrows // SUBLANES
    sub = lax.broadcasted_iota(I32, (SUBLANES, tn), 0)
    vals = [s[SUBLANES * r:SUBLANES * (r + 1), :] for r in range(groups)]
    idxs = [sub + SUBLANES * r for r in range(groups)]
    for a, b in _merge_exchange_pairs(groups):
        va, vb, ia, ib = vals[a], vals[b], idxs[a], idxs[b]
        a_first = (va > vb) | ((va == vb) & (ia < ib))
        vals[a], vals[b] = jnp.maximum(va, vb), jnp.minimum(va, vb)
        idxs[a], idxs[b] = jnp.where(a_first, ia, ib), jnp.where(a_first, ib, ia)
    out_v, out_i = [], []
    for t in range(k):
        head_v, head_i = vals[0], idxs[0]
        m = jnp.max(head_v, axis=0, keepdims=True)
        idx = jnp.min(jnp.where(head_v == m, head_i, nrows), axis=0, keepdims=True)
        out_v.append(m)
        out_i.append(idx)
        won = head_i == idx
        for r in range(min(k - 1 - t, groups)):
            last = r + 1 == groups
            vals[r] = jnp.where(won, -jnp.inf if last else vals[r + 1], vals[r])
            idxs[r] = jnp.where(won, nrows if last else idxs[r + 1], idxs[r])
    return jnp.concatenate(out_v, axis=0), jnp.concatenate(out_i, axis=0)


def _pair_topk(v1, i1, v2, i2):
    tn = v1.shape[1]
    iota_k = lax.broadcasted_iota(I32, (PEER_TOPK, tn), 0)
    iota_8 = lax.broadcasted_iota(I32, (8, tn), 0)

    def pick(table, sel):
        rows = []
        for r in range(PEER_TOPK):
            rows.append(jnp.sum(jnp.where(iota_k == sel[r:r + 1, :], table, 0), axis=0, keepdims=True))
        return jnp.concatenate(rows, axis=0)

    blocks = [v1[0:1, :] + v2]
    for a in range(1, 8):
        blocks.append(jnp.where(iota_8 < PEER_TOPK // (a + 1), v1[a:a + 1, :] + v2[0:8, :], -jnp.inf))
    blocks.append(v1[8:16, :] + v2[0:1, :])
    top, ti = _topk_keys(jnp.concatenate(blocks, axis=0), PEER_TOPK)
    mid = ti - PEER_TOPK
    e1 = pick(i1, jnp.where(ti < 16, 0, jnp.where(ti < 72, (mid >> 3) + 1, ti - 64)))
    e2 = pick(i2, jnp.where(ti < 16, ti, jnp.where(ti < 72, mid & 7, 0)))
    ex = jnp.exp(top - jnp.max(top, axis=0, keepdims=True))
    return e1, e2, ex / jnp.sum(ex, axis=0, keepdims=True)


PEER_ROWS = 16
PEER_EC = PEER_ROWS * PEER_NKEYS
assert PEER_NEXP // PEER_EC == PEER_HEADS


def _gelu(a):
    return 0.5 * a * (1.0 + lax.erf(a * (2.0 ** -0.5)))


def _route_act_kernel(hn_ref, hp_ref, wq_ref, sk_ref, ut_ref, e1_ref, e2_ref, c_ref,
                      q_scr, r1_scr, r2_scr, rg_scr, t1_scr, t2_scr, tg_scr, acc_scr):
    i = pl.program_id(0)
    c = pl.program_id(1)
    slot_new = i % 2
    slot_old = 1 - slot_new

    @pl.when((i == 0) & (c == 0))
    def _():
        t1_scr[1] = jnp.zeros_like(t1_scr[1])
        t2_scr[1] = jnp.zeros_like(t2_scr[1])
        tg_scr[1] = jnp.zeros_like(tg_scr[1])

    @pl.when(c == 0)
    def _():
        acc_scr[...] = jnp.zeros_like(acc_scr)
        q = _dot(hn_ref[...], wq_ref[...])
        for hc in range(2 * PEER_HEADS):
            q_scr[hc] = q[:, hc * LANES:(hc + 1) * LANES].astype(BF16)

    v1, i1 = _topk_keys(_dot_nt(sk_ref[2 * c], q_scr[2 * c]), PEER_TOPK)
    v2, i2 = _topk_keys(_dot_nt(sk_ref[2 * c + 1], q_scr[2 * c + 1]), PEER_TOPK)
    e1h, e2h, gh = _pair_topk(v1, i1, v2, i2)
    rows = pl.ds(pl.multiple_of(c * PEER_TOPK, PEER_TOPK), PEER_TOPK)
    r1_scr[rows, :] = e1h
    r2_scr[rows, :] = e2h
    rg_scr[rows, :] = gh

    a_all = _dot(hp_ref[...], ut_ref[...])
    e1 = t1_scr[slot_old]
    e2 = t2_scr[slot_old]
    acc = acc_scr[...]
    for r in range(PEER_ROWS):
        picked = jnp.take_along_axis(a_all[:, r * LANES:(r + 1) * LANES], e2, axis=1,
                                     mode="promise_in_bounds")
        acc = jnp.where(e1 == c * PEER_ROWS + r, picked, acc)
    acc_scr[...] = acc

    @pl.when(c == pl.num_programs(1) - 1)
    def _():
        c_ref[...] = tg_scr[slot_old] * _gelu(acc)
        t1_scr[slot_new] = r1_scr[...].T
        t2_scr[slot_new] = r2_scr[...].T
        tg_scr[slot_new] = rg_scr[...].T
        e1_ref[...] = t1_scr[slot_new]
        e2_ref[...] = t2_scr[slot_new]


def _route_act(h2, wq, sk, ut):
    n = h2.shape[0]
    tn = _tile(n, 1024)
    tiles = n // tn
    new = lambda i, c: (jnp.minimum(i, tiles - 1), 0)
    old = lambda i, c: (jnp.maximum(i - 1, 0), 0)
    return pl.pallas_call(
        _route_act_kernel,
        grid=(tiles + 1, PEER_HEADS),
        in_specs=[pl.BlockSpec((tn, D_MODEL), new), pl.BlockSpec((tn, D_MODEL), old),
                  _const_spec(wq.shape), _const_spec(sk.shape),
                  pl.BlockSpec((D_MODEL, PEER_EC), lambda i, c: (0, c))],
        out_specs=[pl.BlockSpec((tn, PEER_J), new), pl.BlockSpec((tn, PEER_J), new),
                   pl.BlockSpec((tn, PEER_J), old)],
        out_shape=[jax.ShapeDtypeStruct((n, PEER_J), I32), jax.ShapeDtypeStruct((n, PEER_J), I32),
                   jax.ShapeDtypeStruct((n, PEER_J), F32)],
        scratch_shapes=[pltpu.VMEM((2 * PEER_HEADS, tn, LANES), BF16),
                        pltpu.VMEM((PEER_J, tn), I32), pltpu.VMEM((PEER_J, tn), I32),
                        pltpu.VMEM((PEER_J, tn), F32),
                        pltpu.VMEM((2, tn, PEER_J), I32), pltpu.VMEM((2, tn, PEER_J), I32),
                        pltpu.VMEM((2, tn, PEER_J), F32), pltpu.VMEM((tn, PEER_J), F32)],
        compiler_params=_params(("arbitrary", "arbitrary")),
    )(h2, h2, wq, sk, ut)


PEER_HALF = PEER_ROWS // 2


def _peer_out_kernel(e1_ref, e2_ref, c_ref, v_ref, x_ref, o_ref, y_scr, acc_scr):
    tn = x_ref.shape[0]
    c = pl.program_id(1)
    nchunks = PEER_NEXP // PEER_EC

    @pl.when(c == 0)
    def _():
        acc_scr[...] = jnp.zeros_like(acc_scr)
        key = lax.broadcasted_iota(I32, (PEER_NKEYS, PEER_J), 0)

        def token(t, carry):
            row = pl.ds(t, 1)
            d = jnp.where(key == e1_ref[row, :], c_ref[row, :], 0.0).astype(BF16)
            w = jnp.where(key == e2_ref[row, :], 1.0, 0.0).astype(BF16)
            y = _dot_nt(d, w).astype(BF16).astype(F32)
            bits = lax.bitcast_convert_type(y, I32)
            base = pl.multiple_of(t * PEER_HALF, PEER_HALF)
            for cc in range(nchunks):
                hi = bits[cc * PEER_ROWS:cc * PEER_ROWS + PEER_HALF, :]
                lo = bits[cc * PEER_ROWS + PEER_HALF:(cc + 1) * PEER_ROWS, :]
                y_scr[cc, pl.ds(base, PEER_HALF), :] = hi | lax.shift_right_logical(lo, 16)
            return carry

        lax.fori_loop(0, tn, token, 0, unroll=128)

    his, los = [], []
    for i in range(PEER_HALF):
        words = y_scr[c, pl.ds(i, tn, stride=PEER_HALF), :]
        his.append(lax.bitcast_convert_type(words & -65536, F32).astype(BF16))
        los.append(lax.bitcast_convert_type(words << 16, F32).astype(BF16))
    acc_scr[...] += _dot(jnp.concatenate(his + los, axis=1), v_ref[...])

    @pl.when(c == nchunks - 1)
    def _():
        o_ref[...] = x_ref[...] + acc_scr[...]


def _peer_out(e1, e2, cj, vt, x):
    n = x.shape[0]
    tn = _tile(n, 512)
    row = lambda i, c: (i, 0)
    nchunks = PEER_NEXP // PEER_EC
    return pl.pallas_call(
        _peer_out_kernel,
        grid=(n // tn, nchunks),
        in_specs=[pl.BlockSpec((tn, PEER_J), row), pl.BlockSpec((tn, PEER_J), row),
                  pl.BlockSpec((tn, PEER_J), row), pl.BlockSpec((PEER_EC, D_MODEL), lambda i, c: (c, 0)),
                  pl.BlockSpec((tn, D_MODEL), row)],
        out_specs=pl.BlockSpec((tn, D_MODEL), row),
        out_shape=jax.ShapeDtypeStruct((n, D_MODEL), F32),
        scratch_shapes=[pltpu.VMEM((nchunks, tn * PEER_HALF, LANES), I32),
                        pltpu.VMEM((tn, D_MODEL), F32)],
        compiler_params=_params(("parallel", "arbitrary")),
    )(e1, e2, cj, vt, x)


def _pair_major(lf, batch, length):
    return lf.reshape(batch, length, FOX_HEADS // 2, 2).transpose(0, 2, 3, 1)


def _prep_weights(g_norm1, w_in, w_gla_a2, b_gla_a, g_gla_onorm, g_fox_qnorm, g_fox_knorm, b_fox_f, w_out,
                  g_norm2, w_peer_q, peer_subkeys, peer_u, peer_v):
    bounds = [0]
    for size in IN_SIZES:
        bounds.append(bounds[-1] + size)
    gq, gk, gv, gg, glr, fq, fk, fv, ff = [w_in[:, bounds[i]:bounds[i + 1]] for i in range(len(IN_SIZES))]
    pad = jnp.zeros((D_MODEL, Z_SMALL - FOX_HEADS - GLA_LOWRANK), w_in.dtype)
    w_cat = jnp.concatenate([gq, gk, gv, gg, fq, fk, fv, ff, glr, pad], axis=1).astype(BF16)
    wa2p = jnp.zeros((Z_SMALL, GLA_QK), F32).at[FOX_HEADS:FOX_HEADS + GLA_LOWRANK].set(w_gla_a2).astype(BF16)
    bsm = jnp.zeros((1, Z_SMALL), F32).at[0, :FOX_HEADS].set(b_fox_f)
    head_of_col = jnp.arange(FOX_W) // FOX_HD
    ind = (head_of_col[:, None] == jnp.arange(LANES)[None, :]).astype(BF16)
    return dict(
        inproj=(g_norm1[None], w_cat, wa2p, b_gla_a[None], bsm, jnp.tile(g_fox_qnorm, FOX_HEADS)[None],
                jnp.tile(g_fox_knorm, FOX_HEADS)[None], ind, ind.T),
        gon=g_gla_onorm[None], w_out=w_out.astype(BF16), g2=g_norm2[None], wq=w_peer_q.astype(BF16),
        sk=peer_subkeys.reshape(2 * PEER_HEADS, PEER_NKEYS, LANES).astype(BF16),
        ut=peer_u.T.astype(BF16), v=peer_v.astype(BF16))


def _path_layer(x, wts, layer, depth, stacked, batch, seq, s0, k_past=None, v_past=None, lf_past=None):
    z, la, k_st, v_st, lf_st = _inproj(x, wts["inproj"], layer, depth, stacked)
    og, state = _gla(z, la, s0, wts["gon"], batch=batch, seq=seq)
    past_args = {}
    if k_past is not None:
        past = k_past.shape[1]
        past_args = dict(k_past=k_past.reshape(batch, past, FOX_W),
                         v_past=v_past.reshape(batch, past, FOX_W),
                         lf_past_t=lf_past.reshape(batch, past, FOX_HEADS // 2, 2).transpose(0, 2, 3, 1))
    of = _fox(z, _pair_major(lf_st[layer], batch, seq), batch=batch, seq=seq, **past_args)
    x, h2 = _outproj(og, of, x, wts["w_out"], wts["g2"])
    e1, e2, cj = _route_act(h2, wts["wq"], wts["sk"], wts["ut"])
    x = _peer_out(e1, e2, cj, wts["v"], x)
    return x, (k_st, v_st, lf_st), state


def kernel(x_prompt, x_sample, cache_fox_k, cache_fox_v, cache_fox_logf, state_gla, g_norm1, w_in, w_gla_a2,
           b_gla_a, g_gla_onorm, g_fox_qnorm, g_fox_knorm, b_fox_f, w_out, g_norm2, w_peer_q, peer_subkeys,
           peer_u, peer_v):
    bp, tp, _ = x_prompt.shape
    bs, ts, _ = x_sample.shape
    depth = w_in.shape[0]
    xp = x_prompt.reshape(bp * tp, D_MODEL)
    xs = x_sample.reshape(bs * ts, D_MODEL)
    stk_p, stk_s, st_p, st_s = (), (), [], []
    zero_state = jnp.zeros((bp, GLA_HEADS, GLA_DK, GLA_DV), F32)
    for l in range(depth):
        wts = _prep_weights(g_norm1[l], w_in[l], w_gla_a2[l], b_gla_a[l], g_gla_onorm[l], g_fox_qnorm[l],
                            g_fox_knorm[l], b_fox_f[l], w_out[l], g_norm2[l], w_peer_q[l], peer_subkeys[l],
                            peer_u[l], peer_v[l])
        xp, stk_p, state = _path_layer(xp, wts, l, depth, stk_p, bp, tp, zero_state)
        st_p.append(state)
        xs, stk_s, state = _path_layer(xs, wts, l, depth, stk_s, bs, ts, state_gla[l],
                                       cache_fox_k[l], cache_fox_v[l], cache_fox_logf[l])
        st_s.append(state)

    def unstack(stk, b, t):
        k, v, lf = stk
        return (k.reshape(depth, b, t, FOX_HEADS, FOX_HD), v.reshape(depth, b, t, FOX_HEADS, FOX_HD),
                lf.reshape(depth, b, t, FOX_HEADS))

    return (xp.reshape(bp, tp, D_MODEL), xs.reshape(bs, ts, D_MODEL),
            *unstack(stk_p, bp, tp), jnp.stack(st_p), *unstack(stk_s, bs, ts), jnp.stack(st_s))
```

```python
import functools

import jax
import jax.numpy as jnp
from jax import lax
from jax.experimental import pallas as pl
from jax.experimental.pallas import tpu as pltpu

F32 = jnp.float32
BF16 = jnp.bfloat16
I32 = jnp.int32

D_MODEL = 1024
EPS = 1e-6
GLA_CHUNK = 64
GLA_HEADS = 4
GLA_DK = 64
GLA_DV = 128
GLA_LOWRANK = 16
GLA_GATE_NORM = 16.0
GLA_SUB = 16
FOX_HEADS = 8
FOX_HD = 64
FOX_SCALE = FOX_HD ** -0.5
LOG2E = 1.4426950408889634
FOX_BLOCK = 512
GLA_QK = GLA_HEADS * GLA_DK
GLA_V = GLA_HEADS * GLA_DV
FOX_W = FOX_HEADS * FOX_HD
IN_SIZES = (GLA_QK, GLA_QK, GLA_V, GLA_V, GLA_LOWRANK, FOX_W, FOX_W, FOX_W, FOX_HEADS)
PEER_HEADS = 8
PEER_NKEYS = 128
PEER_TOPK = 16
PEER_NEXP = PEER_NKEYS * PEER_NKEYS
PEER_J = PEER_HEADS * PEER_TOPK

LANES = 128
SUBLANES = 8
Z_MAIN = 3072
Z_SMALL = 128
VMEM_LIMIT = 56 * 1024 * 1024

ZC_GQ, ZC_GK, ZC_GV, ZC_GG, ZC_FQ, ZC_FK, ZC_FV = 0, 2, 4, 8, 12, 16, 20


def _params(sem):
    return pltpu.CompilerParams(dimension_semantics=sem, vmem_limit_bytes=VMEM_LIMIT)


def _tile(n, cap):
    t = cap
    while n % t:
        t //= 2
    return t


def _dot(a, b):
    return jnp.dot(a, b, preferred_element_type=F32)


def _dot_nt(a, b):
    return lax.dot_general(a, b, (((1,), (1,)), ((), ())), preferred_element_type=F32)


def _dot_tn(a, b):
    return lax.dot_general(a, b, (((0,), (0,)), ((), ())), preferred_element_type=F32)


def _split(x):
    hi = x.astype(BF16)
    lo = (x - hi.astype(F32)).astype(BF16)
    return hi, lo


def _dot_exact_rhs(x, m):
    hi, lo = _split(x)
    return _dot(hi, m) + _dot(lo, m)


def _dot_exact_lhs(m, x):
    hi, lo = _split(x)
    return _dot(m, hi) + _dot(m, lo)


def _log_sigmoid(y):
    return jnp.minimum(y, 0.0) - jnp.log(1.0 + jnp.exp(-jnp.abs(y)))


def _const_spec(shape):
    nd = len(shape)
    return pl.BlockSpec(shape, lambda *_: (0,) * nd)


def _inproj_kernel(*refs, n_alias, layer):
    (x_ref, g1_ref, w_ref, wa2_ref, ba_ref, bsm_ref, gq_ref, gk_ref, ind_ref, indt_ref) = refs[:10]
    z_ref, la_ref, k32_ref, v32_ref, lf_ref = refs[10 + n_alias:]
    for stack_ref in (k32_ref, v32_ref, lf_ref):
        for other in range(stack_ref.shape[0]):
            if other != layer:
                stack_ref[other] = jnp.zeros(stack_ref.shape[1:], F32)
    x = x_ref[...]
    ms = jnp.mean(x * x, axis=-1, keepdims=True)
    h = (x * lax.rsqrt(ms + EPS) * g1_ref[...]).astype(BF16)
    z = _dot(h, w_ref[...])

    c_fq, c_fk, c_fv = ZC_FQ * LANES, ZC_FK * LANES, ZC_FV * LANES
    raw = [z[:, c_fq:c_fk], z[:, c_fk:c_fv]]
    ss = [_dot_exact_rhs(t * t, ind_ref[...]) for t in raw]
    scale = [_dot_exact_rhs(lax.rsqrt(v * (1.0 / FOX_HD) + EPS), indt_ref[...]) for v in ss]
    fq = raw[0] * scale[0] * gq_ref[...]
    fk = raw[1] * scale[1] * gk_ref[...]
    fv = z[:, c_fv:Z_MAIN]
    z_ref[:, 0:c_fq] = z[:, 0:c_fq].astype(BF16)
    z_ref[:, c_fq:c_fk] = fq.astype(BF16)
    z_ref[:, c_fk:c_fv] = fk.astype(BF16)
    z_ref[:, c_fv:Z_MAIN] = fv.astype(BF16)
    tm = x_ref.shape[0]
    for hd in range(FOX_HEADS):
        token_rows = pl.ds(hd, tm, stride=FOX_HEADS)
        k32_ref[layer, token_rows, :] = fk[:, hd * FOX_HD:(hd + 1) * FOX_HD]
        v32_ref[layer, token_rows, :] = fv[:, hd * FOX_HD:(hd + 1) * FOX_HD]
    small = z[:, Z_MAIN:Z_MAIN + Z_SMALL]
    lf_ref[layer] = _log_sigmoid(small + bsm_ref[...])[:, 0:FOX_HEADS]
    y = _dot(small.astype(BF16), wa2_ref[...]) + ba_ref[...]
    la_ref[...] = _log_sigmoid(y) * (1.0 / GLA_GATE_NORM)


def _inproj(x, consts, layer, depth, stacked):
    n = x.shape[0]
    tm = _tile(n, 512)
    row = lambda i: (i, 0)
    n_alias = len(stacked)
    slabs, slab = (1, lambda i: (layer, i, 0)) if n_alias else (depth, lambda i: (0, i, 0))
    kv_block = (slabs, tm * FOX_HEADS, FOX_HD)
    stack_shapes = ((depth, n * FOX_HEADS, FOX_HD), (depth, n * FOX_HEADS, FOX_HD), (depth, n, FOX_HEADS))
    return pl.pallas_call(
        functools.partial(_inproj_kernel, n_alias=n_alias, layer=0 if n_alias else layer),
        grid=(n // tm,),
        in_specs=[pl.BlockSpec((tm, D_MODEL), row)] + [_const_spec(c.shape) for c in consts]
        + [pl.BlockSpec(memory_space=pl.ANY)] * n_alias,
        out_specs=[pl.BlockSpec((tm, Z_MAIN), row), pl.BlockSpec((tm, GLA_QK), row)]
        + [pl.BlockSpec(kv_block, slab), pl.BlockSpec(kv_block, slab),
           pl.BlockSpec((slabs, tm, FOX_HEADS), slab)],
        out_shape=[jax.ShapeDtypeStruct((n, Z_MAIN), BF16), jax.ShapeDtypeStruct((n, GLA_QK), F32)]
        + [jax.ShapeDtypeStruct(shape, F32) for shape in stack_shapes],
        input_output_aliases={1 + len(consts) + i: 2 + i for i in range(n_alias)},
        compiler_params=_params(("parallel",)),
    )(x, *consts, *stacked)


def _gla_kernel(q_ref, k_ref, v_ref, gg_ref, la_ref, s0_ref, gon_ref, tri_ref, o_ref, sfin_ref, st_scr,
                *, chunk, nchunk):
    t = pl.program_id(1)
    nsub = chunk // GLA_SUB

    @pl.when(t == 0)
    def _():
        for p in range(2):
            st_scr[p] = s0_ref[0, p].T

    rowi = lax.broadcasted_iota(I32, (chunk, LANES), 0)
    lane = lax.broadcasted_iota(I32, (chunk, LANES), 1)
    lane_st = lax.broadcasted_iota(I32, (chunk, nsub * LANES), 1)
    lane_sq = lax.broadcasted_iota(I32, (LANES, LANES), 1)
    arow = lax.broadcasted_iota(I32, (chunk, chunk), 0)
    acol = lax.broadcasted_iota(I32, (chunk, chunk), 1)

    per_trip = next(n for n in (8, 4, 2, 1) if nchunk % n == 0)

    def body(trip, carry):
        rows_u, prep = [], []
        for u in range(per_trip):
            rows = pl.ds(pl.multiple_of((trip * per_trip + u) * chunk, chunk), chunk)
            rows_u.append(rows)
            bcum = _dot_exact_lhs(tri_ref[...], la_ref[rows, :])
            q = q_ref[rows, :].astype(F32) * (GLA_DK ** -0.5)
            k = k_ref[rows, :].astype(F32)
            for p in range(2):
                cols = slice(p * LANES, (p + 1) * LANES)
                bp, qp, kp = bcum[:, cols], q[:, cols], k[:, cols]
                blast = bp[chunk - 1:chunk, :]
                qs, ks = [], []
                for sb in range(nsub):
                    beta = bp[sb * GLA_SUB - 1:sb * GLA_SUB, :] if sb else jnp.zeros((1, LANES), F32)
                    inblk = (rowi >= sb * GLA_SUB) & (rowi < (sb + 1) * GLA_SUB)
                    qs.append(jnp.where(inblk, qp * jnp.exp(jnp.where(inblk, bp - beta, 0.0)), 0.0))
                    valid = rowi < (sb + 1) * GLA_SUB
                    ks.append(jnp.where(valid, kp * jnp.exp(jnp.where(valid, beta - bp, 0.0)), 0.0))
                prep.append(dict(qst=jnp.concatenate(qs, axis=1), kst=jnp.concatenate(ks, axis=1).astype(BF16),
                                 qinter=qp * jnp.exp(bp), kdec=(kp * jnp.exp(blast - bp)).astype(BF16),
                                 decay=jnp.exp(blast)))
        heads = [(u, p, i) for u in range(per_trip) for p in range(2) for i in range(2)]
        hcols = [slice((2 * p + i) * GLA_DV, (2 * p + i + 1) * GLA_DV) for _, p, i in heads]
        vhs = [v_ref[rows_u[u], hc] for (u, _, _), hc in zip(heads, hcols)]
        a_s = [_dot_nt(jnp.where((lane_st & GLA_DK) == i * GLA_DK, prep[2 * u + p]["qst"], 0.0).astype(BF16),
                       prep[2 * u + p]["kst"]) for u, p, i in heads]
        upd = [_dot_tn(vh, prep[2 * u + p]["kdec"]) for (u, p, i), vh in zip(heads, vhs)]
        a_s = [jnp.where(acol <= arow, a, 0.0).astype(BF16) for a in a_s]
        intra = [_dot(a, vh) for a, vh in zip(a_s, vhs)]
        states = [st_scr[p] for p in range(2)]
        for u in range(per_trip):
            for p in range(2):
                new = states[2 * u + p] * prep[2 * u + p]["decay"] + jnp.where(
                    lane_sq < GLA_DK, upd[4 * u + 2 * p], upd[4 * u + 2 * p + 1])
                states.append(new)
        for p in range(2):
            st_scr[p] = states[2 * per_trip + p]
        stb = [st.astype(BF16) for st in states[:2 * per_trip]]
        o_s = [o_in + _dot_nt(jnp.where((lane & GLA_DK) == i * GLA_DK, prep[2 * u + p]["qinter"], 0.0)
                              .astype(BF16), stb[2 * u + p]) for (u, p, i), o_in in zip(heads, intra)]
        for (u, _, _), o, hc in zip(heads, o_s, hcols):
            on = o * lax.rsqrt(jnp.mean(o * o, axis=-1, keepdims=True) + EPS) * gon_ref[...]
            gate = gg_ref[rows_u[u], hc].astype(F32)
            o_ref[rows_u[u], hc] = (on * gate * (1.0 / (1.0 + jnp.exp(-gate)))).astype(BF16)
        return carry

    lax.fori_loop(0, nchunk // per_trip, body, 0)

    @pl.when(t == pl.num_programs(1) - 1)
    def _():
        for p in range(2):
            sfin_ref[0, p] = st_scr[p].T


def _gla(z, la, s0, gon, *, batch, seq):
    chunk = min(seq, GLA_CHUNK)
    tb = min(seq, 512)
    nt = seq // tb
    tri = jnp.tril(jnp.ones((chunk, chunk), F32)).astype(BF16)
    qk_w, v_w = GLA_QK, GLA_V
    rows = lambda col: (lambda b, t: (b * nt + t, col))
    state_spec = pl.BlockSpec((1, 2, LANES, LANES), lambda b, t: (b, 0, 0, 0))
    o, sfin = pl.pallas_call(
        functools.partial(_gla_kernel, chunk=chunk, nchunk=tb // chunk),
        grid=(batch, nt),
        in_specs=[pl.BlockSpec((tb, qk_w), rows(ZC_GQ * LANES // qk_w)),
                  pl.BlockSpec((tb, qk_w), rows(ZC_GK * LANES // qk_w)),
                  pl.BlockSpec((tb, v_w), rows(ZC_GV * LANES // v_w)),
                  pl.BlockSpec((tb, v_w), rows(ZC_GG * LANES // v_w)),
                  pl.BlockSpec((tb, qk_w), rows(0)), state_spec,
                  _const_spec((1, GLA_DV)), _const_spec((chunk, chunk))],
        out_specs=[pl.BlockSpec((tb, v_w), lambda b, t: (b * nt + t, 0)), state_spec],
        out_shape=[jax.ShapeDtypeStruct((batch * seq, v_w), BF16),
                   jax.ShapeDtypeStruct((batch, 2, LANES, LANES), F32)],
        scratch_shapes=[pltpu.VMEM((2, LANES, LANES), F32)],
        compiler_params=_params(("parallel", "arbitrary")),
    )(z, z, z, z, la, s0.reshape(batch, 2, LANES, LANES), gon, tri)
    return o, sfin.reshape(batch, GLA_HEADS, GLA_DK, GLA_DV)


def _fox_kernel(*refs, seq, past, tq, has_past):
    if has_past:
        (q_ref, kc_ref, vc_ref, lfc_ref, triu_ref, kp_ref, vp_ref, lfp_ref, triup_ref,
         o_ref, cc_scr, cp_scr) = refs
    else:
        q_ref, kc_ref, vc_ref, lfc_ref, triu_ref, o_ref, cc_scr = refs
    i = pl.program_id(2)
    nq = seq // tq
    pblk = FOX_BLOCK
    past_blk = min(past, FOX_BLOCK * FOX_BLOCK // tq)

    @pl.when(i == 0)
    def _():
        carry = jnp.zeros((2, 1), F32)
        if has_past:
            for jb in range(past // pblk):
                x = lfp_ref[0, 0, :, jb * pblk:(jb + 1) * pblk]
                cp_scr[:, jb * pblk:(jb + 1) * pblk] = _dot_exact_rhs(x, triup_ref[...]) + carry
                carry = carry + jnp.sum(x, axis=1, keepdims=True)
        for jb in range(nq):
            x = lfc_ref[0, 0, :, jb * tq:(jb + 1) * tq]
            cc_scr[:, jb * tq:(jb + 1) * tq] = _dot_exact_rhs(x, triu_ref[...]) + carry
            carry = carry + jnp.sum(x, axis=1, keepdims=True)

    q = q_ref[...]
    lane = lax.broadcasted_iota(I32, (tq, LANES), 1)
    qh = [jnp.where(lane < FOX_HD, q, jnp.zeros_like(q)), jnp.where(lane >= FOX_HD, q, jnp.zeros_like(q))]

    causal = lax.broadcasted_iota(I32, (tq, tq), 1) <= lax.broadcasted_iota(I32, (tq, tq), 0)

    def attend(blocks):
        hs = range(2)
        s = [[_dot_nt(qh[h], kb) * (FOX_SCALE * LOG2E) - crows[h] * LOG2E for kb, _, crows, _ in blocks]
             for h in hs]
        s = [[s_b if blk[3] is None else jnp.where(blk[3], s_b, -jnp.inf) for s_b, blk in zip(s[h], blocks)]
             for h in hs]
        m = [functools.reduce(jnp.maximum, [jnp.max(s_b, axis=1, keepdims=True) for s_b in s[h]]) for h in hs]
        acc = []
        for h in hs:
            total = None
            for s_b, (_, vb, _, _) in zip(s[h], blocks):
                lane_v = lax.broadcasted_iota(I32, vb.shape, 1)
                own = (lane_v < FOX_HD) if h == 0 else (lane_v >= FOX_HD)
                pv = _dot(jnp.exp2(s_b - m[h]).astype(BF16), jnp.where(own, vb, jnp.ones_like(vb)))
                total = pv if total is None else total + pv
            acc.append(total)
        num = jnp.where(lane < FOX_HD, acc[0], acc[1])
        den = jnp.where(lane < FOX_HD, pltpu.roll(acc[0], FOX_HD, 1), pltpu.roll(acc[1], FOX_HD, 1))
        o_ref[...] = (num / den).astype(BF16)

    past_blocks = []
    if has_past:
        for off in range(0, past, past_blk):
            past_blocks.append((kp_ref[0, off:off + past_blk, :].astype(BF16),
                                vp_ref[0, off:off + past_blk, :].astype(BF16),
                                [cp_scr[h:h + 1, off:off + past_blk] for h in range(2)], None))

    def cur_block(j, mask):
        rows = slice(j * tq, (j + 1) * tq)
        return (kc_ref[rows, :], vc_ref[rows, :], [cc_scr[h:h + 1, rows] for h in range(2)], mask)

    for qi in range(nq):
        @pl.when(i == qi)
        def _(qi=qi):
            attend(past_blocks + [cur_block(j, None) for j in range(qi)] + [cur_block(qi, causal)])


def _fox(z, lf_cur_t, *, batch, seq, k_past=None, v_past=None, lf_past_t=None):
    has_past = k_past is not None
    past = k_past.shape[1] if has_past else 0
    tq = min(seq, FOX_BLOCK)
    nq = seq // tq
    triu = jnp.triu(jnp.ones((tq, tq), F32)).astype(BF16)
    in_specs = [pl.BlockSpec((tq, LANES), lambda b, p, i: (b * nq + i, ZC_FQ + p)),
                pl.BlockSpec((seq, LANES), lambda b, p, i: (b, ZC_FK + p)),
                pl.BlockSpec((seq, LANES), lambda b, p, i: (b, ZC_FV + p)),
                pl.BlockSpec((1, 1, 2, seq), lambda b, p, i: (b, p, 0, 0)),
                _const_spec((tq, tq))]
    args = [z, z, z, lf_cur_t, triu]
    scratch = [pltpu.VMEM((2, seq), F32)]
    if has_past:
        triup = jnp.triu(jnp.ones((FOX_BLOCK, FOX_BLOCK), F32)).astype(BF16)
        in_specs += [pl.BlockSpec((1, past, LANES), lambda b, p, i: (b, 0, p)),
                     pl.BlockSpec((1, past, LANES), lambda b, p, i: (b, 0, p)),
                     pl.BlockSpec((1, 1, 2, past), lambda b, p, i: (b, p, 0, 0)),
                     _const_spec((FOX_BLOCK, FOX_BLOCK))]
        args += [k_past, v_past, lf_past_t, triup]
        scratch.append(pltpu.VMEM((2, past), F32))
    return pl.pallas_call(
        functools.partial(_fox_kernel, seq=seq, past=past, tq=tq, has_past=has_past),
        grid=(batch, FOX_HEADS // 2, nq),
        in_specs=in_specs,
        out_specs=pl.BlockSpec((tq, LANES), lambda b, p, i: (b * nq + i, p)),
        out_shape=jax.ShapeDtypeStruct((batch * seq, FOX_W), BF16),
        scratch_shapes=scratch,
        compiler_params=_params(("parallel", "parallel", "arbitrary")),
    )(*args)


def _outproj_kernel(og_ref, of_ref, x_ref, w_ref, g2_ref, xo_ref, h2_ref):
    mixed = _dot(og_ref[...], w_ref[0:GLA_V, :]) + _dot(of_ref[...], w_ref[GLA_V:GLA_V + FOX_W, :])
    x = x_ref[...] + mixed
    xo_ref[...] = x
    ms = jnp.mean(x * x, axis=-1, keepdims=True)
    h2_ref[...] = (x * lax.rsqrt(ms + EPS) * g2_ref[...]).astype(BF16)


def _outproj(og, of, x, w, g2):
    n = x.shape[0]
    tm = _tile(n, 512)
    row = lambda i: (i, 0)
    return pl.pallas_call(
        _outproj_kernel,
        grid=(n // tm,),
        in_specs=[pl.BlockSpec((tm, GLA_V), row), pl.BlockSpec((tm, FOX_W), row),
                  pl.BlockSpec((tm, D_MODEL), row), _const_spec(w.shape), _const_spec(g2.shape)],
        out_specs=[pl.BlockSpec((tm, D_MODEL), row), pl.BlockSpec((tm, D_MODEL), row)],
        out_shape=[jax.ShapeDtypeStruct((n, D_MODEL), F32), jax.ShapeDtypeStruct((n, D_MODEL), BF16)],
        compiler_params=_params(("parallel",)),
    )(og, of, x, w, g2)


def _merge_exchange_pairs(n):
    full = 1
    while full < n:
        full *= 2
    pairs = []
    p = 1
    while p < full:
        k = p
        while k >= 1:
            for j in range(k % p, full - k, 2 * k):
                for i in range(min(k, full - j - k)):
                    if (i + j) // (2 * p) == (i + j + k) // (2 * p):
                        pairs.append((i + j, i + j + k))
            k //= 2
        p *= 2
    return tuple((a, b) for a, b in pairs if b < n)


def _topk_keys(s, k):
    nrows, tn = s.shape
    groups = nrows // SUBLANES
    sub = lax.broadcasted_iota(I32, (SUBLANES, tn), 0)
    vals = [s[SUBLANES * r:SUBLANES * (r + 1), :] for r in range(groups)]
    idxs = [sub + SUBLANES * r for r in range(groups)]
    for a, b in _merge_exchange_pairs(groups):
        va, vb, ia, ib = vals[a], vals[b], idxs[a], idxs[b]
        a_first = (va > vb) | ((va == vb) & (ia < ib))
        vals[a], vals[b] = jnp.maximum(va, vb), jnp.minimum(va, vb)
        idxs[a], idxs[b] = jnp.where(a_first, ia, ib), jnp.where(a_first, ib, ia)
    out_v, out_i = [], []
    for t in range(k):
        head_v, head_i = vals[0], idxs[0]
        m = jnp.max(head_v, axis=0, keepdims=True)
        idx = jnp.min(jnp.where(head_v == m, head_i, nrows), axis=0, keepdims=True)
        out_v.append(m)
        out_i.append(idx)
        won = head_i == idx
        for r in range(min(k - 1 - t, groups)):
            last = r + 1 == groups
            vals[r] = jnp.where(won, -jnp.inf if last else vals[r + 1], vals[r])
            idxs[r] = jnp.where(won, nrows if last else idxs[r + 1], idxs[r])
    return jnp.concatenate(out_v, axis=0), jnp.concatenate(out_i, axis=0)


def _pair_topk(v1, i1, v2, i2):
    tn = v1.shape[1]
    iota_k = lax.broadcasted_iota(I32, (PEER_TOPK, tn), 0)
    iota_8 = lax.broadcasted_iota(I32, (8, tn), 0)

    def pick(table, sel):
        halves = [table[0:SUBLANES], table[SUBLANES:PEER_TOPK]]
        out = []
        for rows in (slice(0, SUBLANES), slice(SUBLANES, PEER_TOPK)):
            idx = sel[rows]
            low = idx & (SUBLANES - 1)
            got = [jnp.take_along_axis(h, low, axis=0, mode="promise_in_bounds") for h in halves]
            out.append(jnp.where(idx < SUBLANES, got[0], got[1]))
        return jnp.concatenate(out, axis=0)

    blocks = [v1[0:1, :] + v2]
    for a in range(1, 8):
        blocks.append(jnp.where(iota_8 < PEER_TOPK // (a + 1), v1[a:a + 1, :] + v2[0:8, :], -jnp.inf))
    blocks.append(v1[8:16, :] + v2[0:1, :])
    top, ti = _topk_keys(jnp.concatenate(blocks, axis=0), PEER_TOPK)
    mid = ti - PEER_TOPK
    e1 = pick(i1, jnp.where(ti < 16, 0, jnp.where(ti < 72, (mid >> 3) + 1, ti - 64)))
    e2 = pick(i2, jnp.where(ti < 16, ti, jnp.where(ti < 72, mid & 7, 0)))
    ex = jnp.exp(top - jnp.max(top, axis=0, keepdims=True))
    return e1, e2, ex / jnp.sum(ex, axis=0, keepdims=True)


PEER_ROWS = 16
PEER_EC = PEER_ROWS * PEER_NKEYS
assert PEER_NEXP // PEER_EC == PEER_HEADS


def _gelu(a):
    return 0.5 * a * (1.0 + lax.erf(a * (2.0 ** -0.5)))


def _route_act_kernel(hn_ref, hp_ref, wq_ref, sk_ref, ut_ref, e1_ref, e2_ref, c_ref,
                      q_scr, r1_scr, r2_scr, rg_scr, t1_scr, t2_scr, tg_scr, acc_scr):
    i = pl.program_id(0)
    c = pl.program_id(1)
    slot_new = i % 2
    slot_old = 1 - slot_new

    @pl.when((i == 0) & (c == 0))
    def _():
        t1_scr[1] = jnp.zeros_like(t1_scr[1])
        t2_scr[1] = jnp.zeros_like(t2_scr[1])
        tg_scr[1] = jnp.zeros_like(tg_scr[1])

    @pl.when(c == 0)
    def _():
        acc_scr[...] = jnp.zeros_like(acc_scr)
        q = _dot(hn_ref[...], wq_ref[...])
        for hc in range(2 * PEER_HEADS):
            q_scr[hc] = q[:, hc * LANES:(hc + 1) * LANES].astype(BF16)

    v1, i1 = _topk_keys(_dot_nt(sk_ref[2 * c], q_scr[2 * c]), PEER_TOPK)
    v2, i2 = _topk_keys(_dot_nt(sk_ref[2 * c + 1], q_scr[2 * c + 1]), PEER_TOPK)
    e1h, e2h, gh = _pair_topk(v1, i1, v2, i2)
    rows = pl.ds(pl.multiple_of(c * PEER_TOPK, PEER_TOPK), PEER_TOPK)
    r1_scr[rows, :] = e1h
    r2_scr[rows, :] = e2h
    rg_scr[rows, :] = gh

    a_all = _dot(hp_ref[...], ut_ref[...])
    e1 = t1_scr[slot_old]
    e2 = t2_scr[slot_old]
    acc = acc_scr[...]
    for r in range(PEER_ROWS):
        picked = jnp.take_along_axis(a_all[:, r * LANES:(r + 1) * LANES], e2, axis=1,
                                     mode="promise_in_bounds")
        acc = jnp.where(e1 == c * PEER_ROWS + r, picked, acc)
    acc_scr[...] = acc

    @pl.when(c == pl.num_programs(1) - 1)
    def _():
        c_ref[...] = tg_scr[slot_old] * _gelu(acc)
        t1_scr[slot_new] = r1_scr[...].T
        t2_scr[slot_new] = r2_scr[...].T
        tg_scr[slot_new] = rg_scr[...].T
        e1_ref[...] = t1_scr[slot_new]
        e2_ref[...] = t2_scr[slot_new]


def _route_act(h2, wq, sk, ut):
    n = h2.shape[0]
    tn = _tile(n, 1024)
    tiles = n // tn
    new = lambda i, c: (jnp.minimum(i, tiles - 1), 0)
    old = lambda i, c: (jnp.maximum(i - 1, 0), 0)
    return pl.pallas_call(
        _route_act_kernel,
        grid=(tiles + 1, PEER_HEADS),
        in_specs=[pl.BlockSpec((tn, D_MODEL), new), pl.BlockSpec((tn, D_MODEL), old),
                  _const_spec(wq.shape), _const_spec(sk.shape),
                  pl.BlockSpec((D_MODEL, PEER_EC), lambda i, c: (0, c))],
        out_specs=[pl.BlockSpec((tn, PEER_J), new), pl.BlockSpec((tn, PEER_J), new),
                   pl.BlockSpec((tn, PEER_J), old)],
        out_shape=[jax.ShapeDtypeStruct((n, PEER_J), I32), jax.ShapeDtypeStruct((n, PEER_J), I32),
                   jax.ShapeDtypeStruct((n, PEER_J), F32)],
        scratch_shapes=[pltpu.VMEM((2 * PEER_HEADS, tn, LANES), BF16),
                        pltpu.VMEM((PEER_J, tn), I32), pltpu.VMEM((PEER_J, tn), I32),
                        pltpu.VMEM((PEER_J, tn), F32),
                        pltpu.VMEM((2, tn, PEER_J), I32), pltpu.VMEM((2, tn, PEER_J), I32),
                        pltpu.VMEM((2, tn, PEER_J), F32), pltpu.VMEM((tn, PEER_J), F32)],
        compiler_params=_params(("arbitrary", "arbitrary")),
    )(h2, h2, wq, sk, ut)


PEER_HALF = PEER_ROWS // 2


def _peer_out_kernel(e1_ref, e2_ref, c_ref, v_ref, x_ref, o_ref, y_scr, acc_scr):
    tn = x_ref.shape[0]
    c = pl.program_id(1)
    nchunks = PEER_NEXP // PEER_EC

    @pl.when(c == 0)
    def _():
        acc_scr[...] = jnp.zeros_like(acc_scr)
        key = lax.broadcasted_iota(I32, (PEER_NKEYS, PEER_J), 0)

        def token(t, carry):
            row = pl.ds(t, 1)
            d = jnp.where(key == e1_ref[row, :], c_ref[row, :], 0.0).astype(BF16)
            w = jnp.where(key == e2_ref[row, :], 1.0, 0.0).astype(BF16)
            y = _dot_nt(d, w).astype(BF16).astype(F32)
            bits = lax.bitcast_convert_type(y, I32)
            base = pl.multiple_of(t * PEER_HALF, PEER_HALF)
            for cc in range(nchunks):
                hi = bits[cc * PEER_ROWS:cc * PEER_ROWS + PEER_HALF, :]
                lo = bits[cc * PEER_ROWS + PEER_HALF:(cc + 1) * PEER_ROWS, :]
                y_scr[cc, pl.ds(base, PEER_HALF), :] = hi | lax.shift_right_logical(lo, 16)
            return carry

        lax.fori_loop(0, tn, token, 0, unroll=128)

    his, los = [], []
    for i in range(PEER_HALF):
        words = y_scr[c, pl.ds(i, tn, stride=PEER_HALF), :]
        his.append(lax.bitcast_convert_type(words & -65536, F32).astype(BF16))
        los.append(lax.bitcast_convert_type(words << 16, F32).astype(BF16))
    acc_scr[...] += _dot(jnp.concatenate(his + los, axis=1), v_ref[...])

    @pl.when(c == nchunks - 1)
    def _():
        o_ref[...] = x_ref[...] + acc_scr[...]


def _peer_out(e1, e2, cj, vt, x):
    n = x.shape[0]
    tn = _tile(n, 512)
    row = lambda i, c: (i, 0)
    nchunks = PEER_NEXP // PEER_EC
    return pl.pallas_call(
        _peer_out_kernel,
        grid=(n // tn, nchunks),
        in_specs=[pl.BlockSpec((tn, PEER_J), row), pl.BlockSpec((tn, PEER_J), row),
                  pl.BlockSpec((tn, PEER_J), row), pl.BlockSpec((PEER_EC, D_MODEL), lambda i, c: (c, 0)),
                  pl.BlockSpec((tn, D_MODEL), row)],
        out_specs=pl.BlockSpec((tn, D_MODEL), row),
        out_shape=jax.ShapeDtypeStruct((n, D_MODEL), F32),
        scratch_shapes=[pltpu.VMEM((nchunks, tn * PEER_HALF, LANES), I32),
                        pltpu.VMEM((tn, D_MODEL), F32)],
        compiler_params=_params(("parallel", "arbitrary")),
    )(e1, e2, cj, vt, x)


def _pair_major(lf, batch, length):
    return lf.reshape(batch, length, FOX_HEADS // 2, 2).transpose(0, 2, 3, 1)


def _prep_weights(g_norm1, w_in, w_gla_a2, b_gla_a, g_gla_onorm, g_fox_qnorm, g_fox_knorm, b_fox_f, w_out,
                  g_norm2, w_peer_q, peer_subkeys, peer_u, peer_v):
    bounds = [0]
    for size in IN_SIZES:
        bounds.append(bounds[-1] + size)
    gq, gk, gv, gg, glr, fq, fk, fv, ff = [w_in[:, bounds[i]:bounds[i + 1]] for i in range(len(IN_SIZES))]
    pad = jnp.zeros((D_MODEL, Z_SMALL - FOX_HEADS - GLA_LOWRANK), w_in.dtype)
    w_cat = jnp.concatenate([gq, gk, gv, gg, fq, fk, fv, ff, glr, pad], axis=1).astype(BF16)
    wa2p = jnp.zeros((Z_SMALL, GLA_QK), F32).at[FOX_HEADS:FOX_HEADS + GLA_LOWRANK].set(w_gla_a2).astype(BF16)
    bsm = jnp.zeros((1, Z_SMALL), F32).at[0, :FOX_HEADS].set(b_fox_f)
    head_of_col = jnp.arange(FOX_W) // FOX_HD
    ind = (head_of_col[:, None] == jnp.arange(LANES)[None, :]).astype(BF16)
    return dict(
        inproj=(g_norm1[None], w_cat, wa2p, b_gla_a[None], bsm, jnp.tile(g_fox_qnorm, FOX_HEADS)[None],
                jnp.tile(g_fox_knorm, FOX_HEADS)[None], ind, ind.T),
        gon=g_gla_onorm[None], w_out=w_out.astype(BF16), g2=g_norm2[None], wq=w_peer_q.astype(BF16),
        sk=peer_subkeys.reshape(2 * PEER_HEADS, PEER_NKEYS, LANES).astype(BF16),
        ut=peer_u.T.astype(BF16), v=peer_v.astype(BF16))


def _path_layer(x, wts, layer, depth, stacked, batch, seq, s0, k_past=None, v_past=None, lf_past=None):
    z, la, k_st, v_st, lf_st = _inproj(x, wts["inproj"], layer, depth, stacked)
    og, state = _gla(z, la, s0, wts["gon"], batch=batch, seq=seq)
    past_args = {}
    if k_past is not None:
        past = k_past.shape[1]
        past_args = dict(k_past=k_past.reshape(batch, past, FOX_W),
                         v_past=v_past.reshape(batch, past, FOX_W),
                         lf_past_t=lf_past.reshape(batch, past, FOX_HEADS // 2, 2).transpose(0, 2, 3, 1))
    of = _fox(z, _pair_major(lf_st[layer], batch, seq), batch=batch, seq=seq, **past_args)
    x, h2 = _outproj(og, of, x, wts["w_out"], wts["g2"])
    e1, e2, cj = _route_act(h2, wts["wq"], wts["sk"], wts["ut"])
    x = _peer_out(e1, e2, cj, wts["v"], x)
    return x, (k_st, v_st, lf_st), state


def kernel(x_prompt, x_sample, cache_fox_k, cache_fox_v, cache_fox_logf, state_gla, g_norm1, w_in, w_gla_a2,
           b_gla_a, g_gla_onorm, g_fox_qnorm, g_fox_knorm, b_fox_f, w_out, g_norm2, w_peer_q, peer_subkeys,
           peer_u, peer_v):
    bp, tp, _ = x_prompt.shape
    bs, ts, _ = x_sample.shape
    depth = w_in.shape[0]
    xp = x_prompt.reshape(bp * tp, D_MODEL)
    xs = x_sample.reshape(bs * ts, D_MODEL)
    stk_p, stk_s, st_p, st_s = (), (), [], []
    zero_state = jnp.zeros((bp, GLA_HEADS, GLA_DK, GLA_DV), F32)
    for l in range(depth):
        wts = _prep_weights(g_norm1[l], w_in[l], w_gla_a2[l], b_gla_a[l], g_gla_onorm[l], g_fox_qnorm[l],
                            g_fox_knorm[l], b_fox_f[l], w_out[l], g_norm2[l], w_peer_q[l], peer_subkeys[l],
                            peer_u[l], peer_v[l])
        xp, stk_p, state = _path_layer(xp, wts, l, depth, stk_p, bp, tp, zero_state)
        st_p.append(state)
        xs, stk_s, state = _path_layer(xs, wts, l, depth, stk_s, bs, ts, state_gla[l],
                                       cache_fox_k[l], cache_fox_v[l], cache_fox_logf[l])
        st_s.append(state)

    def unstack(stk, b, t):
        k, v, lf = stk
        return (k.reshape(depth, b, t, FOX_HEADS, FOX_HD), v.reshape(depth, b, t, FOX_HEADS, FOX_HD),
                lf.reshape(depth, b, t, FOX_HEADS))

    return (xp.reshape(bp, tp, D_MODEL), xs.reshape(bs, ts, D_MODEL),
            *unstack(stk_p, bp, tp), jnp.stack(st_p), *unstack(stk_s, bs, ts), jnp.stack(st_s))
```
